```python
import math
import jax, jax.numpy as jnp
from jax import lax
import numpy as np

D_MODEL = 2048
BATCH = 4
SEQ = 2048
DEPTH = 2
DEC_BATCH = 32
DEC_SEQ = 1
PAST_LEN = 8192
PAGE_SIZE = 128

SB_HEAD_DIM = 128
SB_HEADS = D_MODEL // 256
SB_WIDTH = SB_HEADS * SB_HEAD_DIM
Q_BLOCK = 128
SSM_WIDTH = D_MODEL // 2
SSM_GROUP = 16
SSM_GROUPS = SSM_WIDTH // SSM_GROUP
SSM_STATE = 64
GDN_HEADS = D_MODEL // 256
GDN_DK = 128
GDN_DV = 128
GDN_KW = GDN_HEADS * GDN_DK
GDN_VW = GDN_HEADS * GDN_DV
CONV_WIDTH = 2 * GDN_KW + GDN_VW
GDN_CONV = 4
GDN_CHUNK = 64
FFN_DIM = ((8 * D_MODEL // 3 + 255) // 256) * 256
MIX_WIDTH = SB_WIDTH + SSM_WIDTH + GDN_VW
IN_SPLITS = (SB_WIDTH, SB_WIDTH, SB_WIDTH, SSM_WIDTH, CONV_WIDTH, GDN_VW, GDN_HEADS, GDN_HEADS, D_MODEL, D_MODEL, D_MODEL)
IN_COLS = sum(IN_SPLITS)
NORM_EPS = 1e-6

kernel_name = 'hybrid_stickbreak_s5_gdelta_decoder_step'


def rms_norm(x, gain):
    xf = x.astype(jnp.float32)
    y = xf * lax.rsqrt(jnp.mean(xf * xf, axis=-1, keepdims=True) + NORM_EPS)
    return (y * gain.astype(jnp.float32)).astype(x.dtype)


def l2_normalize(x):
    xf = x.astype(jnp.float32)
    return xf * lax.rsqrt(jnp.sum(xf * xf, axis=-1, keepdims=True) + NORM_EPS)


def swiglu(x, w_gate, w_up, w_down):
    return (jax.nn.silu(x @ w_gate) * (x @ w_up)) @ w_down


def split_points(sizes):
    points, acc = [], 0
    for n in sizes[:-1]:
        acc += n
        points.append(acc)
    return points


def _sb_block(q, keys, values, q_pos, bias):
    z = jnp.concatenate([jnp.einsum('bqhd,bkhd->bhqk', q, k, preferred_element_type=jnp.float32) for k in keys], axis=-1) * (SB_HEAD_DIM ** -0.5)
    z = z + bias.astype(jnp.float32)[None, :, None, None]
    k_pos = jnp.arange(z.shape[-1])
    mask = k_pos[None, :] < q_pos[:, None]
    log_fail = jnp.where(mask, jax.nn.log_sigmoid(-z), 0.0)
    log_between = lax.cumsum(log_fail, axis=3, reverse=True) - log_fail
    weights = jnp.where(mask, jnp.exp(jax.nn.log_sigmoid(z) + log_between), 0.0).astype(q.dtype)
    out, start = None, 0
    for v in values:
        n = v.shape[1]
        part = jnp.einsum('bhqk,bkhd->bqhd', weights[..., start:start + n], v)
        out = part if out is None else out + part
        start += n
    return out


def stick_breaking_attention(q, keys, values, bias):
    bsz, sq, h, dh = q.shape
    sk = sum(k.shape[1] for k in keys)
    n_blocks = -(-sq // Q_BLOCK)
    q_pos = (sk - sq) + jnp.arange(n_blocks * Q_BLOCK)
    if n_blocks == 1:
        return _sb_block(q, keys, values, q_pos[:sq], bias)
    pad = n_blocks * Q_BLOCK - sq
    qb = jnp.pad(q, ((0, 0), (0, pad), (0, 0), (0, 0))).reshape(bsz, n_blocks, Q_BLOCK, h, dh)
    qb = jnp.moveaxis(qb, 1, 0)
    out = lax.map(lambda blk: _sb_block(blk[0], keys, values, blk[1], bias), (qb, q_pos.reshape(n_blocks, Q_BLOCK)))
    return jnp.moveaxis(out, 0, 1).reshape(bsz, n_blocks * Q_BLOCK, h, dh)[:, :sq]


def s5_ssm(u, h0, a_re, a_im, log_dt, b_re, b_im, c_re, c_im, d):
    f32 = jnp.float32
    bsz, s, _ = u.shape
    uf = u.astype(f32).reshape(bsz, s, SSM_GROUPS, SSM_GROUP)
    ar, ai = a_re.astype(f32), a_im.astype(f32)
    dt = jnp.exp(log_dt.astype(f32))[:, None]
    mag = jnp.exp(dt * ar)
    abar_re, abar_im = mag * jnp.cos(dt * ai), mag * jnp.sin(dt * ai)
    den = ar * ar + ai * ai
    coef_re = ((abar_re - 1.0) * ar + abar_im * ai) / den
    coef_im = (abar_im * ar - (abar_re - 1.0) * ai) / den
    br, bi = b_re.astype(f32), b_im.astype(f32)
    bbar_re = coef_re[..., None] * br - coef_im[..., None] * bi
    bbar_im = coef_re[..., None] * bi + coef_im[..., None] * br
    bu_re = jnp.einsum('bsgc,gpc->bsgp', uf, bbar_re)
    bu_im = jnp.einsum('bsgc,gpc->bsgp', uf, bbar_im)
    h_re, h_im = h0[..., 0].astype(f32), h0[..., 1].astype(f32)
    bu_re = bu_re.at[:, 0].add(abar_re * h_re - abar_im * h_im)
    bu_im = bu_im.at[:, 0].add(abar_re * h_im + abar_im * h_re)

    def combine(e1, e2):
        a1r, a1i, b1r, b1i = e1
        a2r, a2i, b2r, b2i = e2
        return (a2r * a1r - a2i * a1i, a2r * a1i + a2i * a1r,
                a2r * b1r - a2i * b1i + b2r, a2r * b1i + a2i * b1r + b2i)

    _, _, x_re, x_im = lax.associative_scan(
        combine, (jnp.broadcast_to(abar_re, bu_re.shape), jnp.broadcast_to(abar_im, bu_im.shape), bu_re, bu_im), axis=1)
    y = jnp.einsum('bsgp,gcp->bsgc', x_re, c_re.astype(f32)) - jnp.einsum('bsgp,gcp->bsgc', x_im, c_im.astype(f32))
    y = y + d.astype(f32).reshape(SSM_GROUPS, SSM_GROUP) * uf
    h_new = jnp.stack([x_re[:, -1], x_im[:, -1]], axis=-1)
    return y.reshape(bsz, s, SSM_WIDTH).astype(u.dtype), h_new


def causal_short_conv(x, buf, w):
    s = x.shape[1]
    xp = jnp.concatenate([buf.astype(x.dtype), x], axis=1)
    w = w.astype(x.dtype)
    out = sum(xp[:, i:i + s] * w[i] for i in range(GDN_CONV))
    return jax.nn.silu(out), xp[:, xp.shape[1] - (GDN_CONV - 1):]


def gated_delta_rule(q, k, v, g, beta, s0):
    f32 = jnp.float32
    bsz, s, h, dk = q.shape
    dv = v.shape[-1]
    n = -(-s // GDN_CHUNK)
    pad = n * GDN_CHUNK - s

    def chunks(t):
        t = jnp.pad(t.astype(f32), [(0, 0), (0, pad)] + [(0, 0)] * (t.ndim - 2))
        t = t.reshape((bsz, n, GDN_CHUNK) + t.shape[2:])
        return jnp.swapaxes(jnp.moveaxis(t, 1, 0), 2, 3)

    qc = chunks(q) * (dk ** -0.5)
    kc, vc, bc = chunks(k), chunks(v), chunks(beta)
    gc = jnp.cumsum(chunks(g), axis=-1)
    idx = jnp.arange(GDN_CHUNK)
    incl = idx[:, None] >= idx[None, :]
    strict = idx[:, None] > idx[None, :]
    decay = jnp.exp(jnp.where(incl, gc[..., :, None] - gc[..., None, :], -jnp.inf))
    kb = kc * bc[..., None]
    m = jnp.where(strict, jnp.einsum('nbhik,nbhjk->nbhij', kb, kc) * decay, 0.0)
    rhs = jnp.concatenate([vc * bc[..., None], kb * jnp.exp(gc)[..., None]], axis=-1)
    sol = lax.linalg.triangular_solve(m + jnp.eye(GDN_CHUNK, dtype=f32), rhs, left_side=True, lower=True)
    u_c, w_c = sol[..., :dv], sol[..., dv:]
    qk = jnp.einsum('nbhik,nbhjk->nbhij', qc, kc) * decay
    q_dec = qc * jnp.exp(gc)[..., None]
    k_dec = kc * jnp.exp(gc[..., -1:] - gc)[..., None]
    g_last = jnp.exp(gc[..., -1])

    def step(state, xs):
        u_i, w_i, qk_i, qd_i, kd_i, gl_i = xs
        v_new = u_i - jnp.einsum('bhck,bhkv->bhcv', w_i, state)
        o_i = jnp.einsum('bhck,bhkv->bhcv', qd_i, state) + jnp.einsum('bhij,bhjv->bhiv', qk_i, v_new)
        state = state * gl_i[..., None, None] + jnp.einsum('bhck,bhcv->bhkv', kd_i, v_new)
        return state, o_i

    s_final, o = lax.scan(step, s0.astype(f32), (u_c, w_c, qk, q_dec, k_dec, g_last))
    o = jnp.moveaxis(jnp.swapaxes(o, 2, 3), 0, 1).reshape(bsz, n * GDN_CHUNK, h, dv)[:, :s]
    return o, s_final


def gated_deltanet(qkv, z, b, a, conv_buf, s0, conv_w, a_log, dt_bias, out_norm):
    bsz, s, _ = qkv.shape
    xc, conv_new = causal_short_conv(qkv, conv_buf, conv_w)
    q, k, v = jnp.split(xc, [GDN_KW, 2 * GDN_KW], axis=-1)
    q = l2_normalize(q.reshape(bsz, s, GDN_HEADS, GDN_DK))
    k = l2_normalize(k.reshape(bsz, s, GDN_HEADS, GDN_DK))
    v = v.reshape(bsz, s, GDN_HEADS, GDN_DV)
    beta = jax.nn.sigmoid(b.astype(jnp.float32))
    g = -jnp.exp(a_log.astype(jnp.float32)) * jax.nn.softplus(a.astype(jnp.float32) + dt_bias.astype(jnp.float32))
    o, s_new = gated_delta_rule(q, k, v, g, beta, s0)
    o = rms_norm(o, out_norm) * jax.nn.silu(z.astype(jnp.float32).reshape(bsz, s, GDN_HEADS, GDN_DV))
    return o.reshape(bsz, s, GDN_VW).astype(qkv.dtype), conv_new, s_new


def decoder_layer(x, p, kv_past, ssm_h0, conv_buf, gdn_s0):
    bsz, s, _ = x.shape
    x = x + 0.5 * swiglu(rms_norm(x, p['ffn1_norm']), p['ffn1_w_gate'], p['ffn1_w_up'], p['ffn1_w_down'])
    h = rms_norm(x, p['mix_norm'])
    proj = h @ p['w_in']
    (sb_q, sb_k, sb_v, ssm_u, gdn_qkv, gdn_z, gdn_b, gdn_a,
     gate_a, gate_b, gate_c) = jnp.split(proj, split_points(IN_SPLITS), axis=-1)
    sb_q = sb_q.reshape(bsz, s, SB_HEADS, SB_HEAD_DIM)
    sb_k = sb_k.reshape(bsz, s, SB_HEADS, SB_HEAD_DIM)
    sb_v = sb_v.reshape(bsz, s, SB_HEADS, SB_HEAD_DIM)
    if kv_past is None:
        keys, values = (sb_k,), (sb_v,)
    else:
        keys = (kv_past[0].astype(sb_k.dtype), sb_k)
        values = (kv_past[1].astype(sb_v.dtype), sb_v)
    o_a = stick_breaking_attention(sb_q, keys, values, p['sb_bias']).reshape(bsz, s, SB_WIDTH)
    y_b, ssm_h = s5_ssm(ssm_u, ssm_h0, p['ssm_a_re'], p['ssm_a_im'], p['ssm_log_dt'], p['ssm_b_re'],
                        p['ssm_b_im'], p['ssm_c_re'], p['ssm_c_im'], p['ssm_d'])
    g_b = jax.nn.gelu(y_b)
    o_b = g_b * jax.nn.sigmoid(g_b @ p['ssm_w_glu'])
    o_c, conv_new, gdn_s = gated_deltanet(gdn_qkv, gdn_z, gdn_b, gdn_a, conv_buf, gdn_s0, p['gdn_conv_w'],
                                          p['gdn_a_log'], p['gdn_dt_bias'], p['gdn_out_norm'])
    w_a, w_b, w_c = jnp.split(p['w_branch'], [SB_WIDTH, SB_WIDTH + SSM_WIDTH], axis=0)
    merged = (jax.nn.sigmoid(gate_a) * (o_a @ w_a) + jax.nn.sigmoid(gate_b) * (o_b @ w_b)
              + jax.nn.sigmoid(gate_c) * (o_c @ w_c))
    x = x + merged @ p['w_out']
    x = x + 0.5 * swiglu(rms_norm(x, p['ffn2_norm']), p['ffn2_w_gate'], p['ffn2_w_up'], p['ffn2_w_down'])
    return x, (sb_k, sb_v, ssm_h, conv_new, gdn_s)


def setup_inputs(seed: int = 0) -> dict:
    key = jax.random.key(seed)
    ks = iter(jax.random.split(key, 48))
    f32 = jnp.float32
    n_pages = PAST_LEN // PAGE_SIZE
    used = DEC_BATCH * n_pages
    n_phys = used + max(1, used // 4)

    def nrm(shape, scale):
        return scale * jax.random.normal(next(ks), shape, f32)

    def gain(shape):
        return 1.0 + nrm(shape, 0.01)

    x_prompt = nrm((BATCH, SEQ, D_MODEL), 1.0)
    x_sample = nrm((DEC_BATCH, DEC_SEQ, D_MODEL), 1.0)
    cache_k = nrm((DEPTH, n_phys, PAGE_SIZE, SB_HEADS, SB_HEAD_DIM), 1.0)
    cache_v = nrm((DEPTH, n_phys, PAGE_SIZE, SB_HEADS, SB_HEAD_DIM), 1.0)
    page_table = jax.random.permutation(next(ks), n_phys)[:used].reshape(DEC_BATCH, n_pages).astype(jnp.int32)
    state_ssm = nrm((DEPTH, DEC_BATCH, SSM_GROUPS, SSM_STATE, 2), 0.1)
    state_conv = nrm((DEPTH, DEC_BATCH, GDN_CONV - 1, CONV_WIDTH), 1.0)
    state_gdn = nrm((DEPTH, DEC_BATCH, GDN_HEADS, GDN_DK, GDN_DV), 0.1)
    ffn1_norm = gain((DEPTH, D_MODEL))
    ffn1_w_gate = nrm((DEPTH, D_MODEL, FFN_DIM), D_MODEL ** -0.5)
    ffn1_w_up = nrm((DEPTH, D_MODEL, FFN_DIM), D_MODEL ** -0.5)
    ffn1_w_down = nrm((DEPTH, FFN_DIM, D_MODEL), FFN_DIM ** -0.5)
    mix_norm = gain((DEPTH, D_MODEL))
    w_in = nrm((DEPTH, D_MODEL, IN_COLS), D_MODEL ** -0.5)
    sb_bias = jax.random.uniform(next(ks), (DEPTH, SB_HEADS), f32, -11.0, -9.0)
    ssm_a_re = -0.5 + nrm((DEPTH, SSM_GROUPS, SSM_STATE), 0.01)
    ssm_a_im = math.pi * jnp.arange(SSM_STATE, dtype=f32) + nrm((DEPTH, SSM_GROUPS, SSM_STATE), 0.01)
    ssm_log_dt = jax.random.uniform(next(ks), (DEPTH, SSM_GROUPS), f32, math.log(1e-3), math.log(1e-1))
    ssm_b_re = nrm((DEPTH, SSM_GROUPS, SSM_STATE, SSM_GROUP), (2 * SSM_GROUP) ** -0.5)
    ssm_b_im = nrm((DEPTH, SSM_GROUPS, SSM_STATE, SSM_GROUP), (2 * SSM_GROUP) ** -0.5)
    ssm_c_re = nrm((DEPTH, SSM_GROUPS, SSM_GROUP, SSM_STATE), SSM_STATE ** -0.5)
    ssm_c_im = nrm((DEPTH, SSM_GROUPS, SSM_GROUP, SSM_STATE), SSM_STATE ** -0.5)
    ssm_d = nrm((DEPTH, SSM_WIDTH), 1.0)
    ssm_w_glu = nrm((DEPTH, SSM_WIDTH, SSM_WIDTH), SSM_WIDTH ** -0.5)
    gdn_conv_w = nrm((DEPTH, GDN_CONV, CONV_WIDTH), GDN_CONV ** -0.5)
    gdn_a_log = jnp.log(jax.random.uniform(next(ks), (DEPTH, GDN_HEADS), f32, 1.0, 16.0))
    gdn_dt = jnp.exp(jax.random.uniform(next(ks), (DEPTH, GDN_HEADS), f32, math.log(1e-3), math.log(1e-1)))
    gdn_dt_bias = gdn_dt + jnp.log(-jnp.expm1(-gdn_dt))
    gdn_out_norm = gain((DEPTH, GDN_DV))
    w_branch = nrm((DEPTH, MIX_WIDTH, D_MODEL), SB_WIDTH ** -0.5)
    w_out = nrm((DEPTH, D_MODEL, D_MODEL), D_MODEL ** -0.5)
    ffn2_norm = gain((DEPTH, D_MODEL))
    ffn2_w_gate = nrm((DEPTH, D_MODEL, FFN_DIM), D_MODEL ** -0.5)
    ffn2_w_up = nrm((DEPTH, D_MODEL, FFN_DIM), D_MODEL ** -0.5)
    ffn2_w_down = nrm((DEPTH, FFN_DIM, D_MODEL), FFN_DIM ** -0.5)
    final_norm = gain((D_MODEL,))
    return {'x_prompt': x_prompt, 'x_sample': x_sample, 'cache_k': cache_k, 'cache_v': cache_v,
            'page_table': page_table, 'state_ssm': state_ssm, 'state_conv': state_conv, 'state_gdn': state_gdn,
            'ffn1_norm': ffn1_norm, 'ffn1_w_gate': ffn1_w_gate, 'ffn1_w_up': ffn1_w_up, 'ffn1_w_down': ffn1_w_down,
            'mix_norm': mix_norm, 'w_in': w_in, 'sb_bias': sb_bias, 'ssm_a_re': ssm_a_re, 'ssm_a_im': ssm_a_im,
            'ssm_log_dt': ssm_log_dt, 'ssm_b_re': ssm_b_re, 'ssm_b_im': ssm_b_im, 'ssm_c_re': ssm_c_re,
            'ssm_c_im': ssm_c_im, 'ssm_d': ssm_d, 'ssm_w_glu': ssm_w_glu, 'gdn_conv_w': gdn_conv_w,
            'gdn_a_log': gdn_a_log, 'gdn_dt_bias': gdn_dt_bias, 'gdn_out_norm': gdn_out_norm, 'w_branch': w_branch,
            'w_out': w_out, 'ffn2_norm': ffn2_norm, 'ffn2_w_gate': ffn2_w_gate, 'ffn2_w_up': ffn2_w_up,
            'ffn2_w_down': ffn2_w_down, 'final_norm': final_norm}


def reference(x_prompt, x_sample, cache_k, cache_v, page_table, state_ssm, state_conv, state_gdn,
              ffn1_norm, ffn1_w_gate, ffn1_w_up, ffn1_w_down, mix_norm, w_in, sb_bias,
              ssm_a_re, ssm_a_im, ssm_log_dt, ssm_b_re, ssm_b_im, ssm_c_re, ssm_c_im, ssm_d, ssm_w_glu,
              gdn_conv_w, gdn_a_log, gdn_dt_bias, gdn_out_norm, w_branch, w_out,
              ffn2_norm, ffn2_w_gate, ffn2_w_up, ffn2_w_down, final_norm):
    layers = [dict(ffn1_norm=ffn1_norm[l], ffn1_w_gate=ffn1_w_gate[l], ffn1_w_up=ffn1_w_up[l],
                   ffn1_w_down=ffn1_w_down[l], mix_norm=mix_norm[l], w_in=w_in[l], sb_bias=sb_bias[l],
                   ssm_a_re=ssm_a_re[l], ssm_a_im=ssm_a_im[l], ssm_log_dt=ssm_log_dt[l],
                   ssm_b_re=ssm_b_re[l], ssm_b_im=ssm_b_im[l], ssm_c_re=ssm_c_re[l], ssm_c_im=ssm_c_im[l],
                   ssm_d=ssm_d[l], ssm_w_glu=ssm_w_glu[l], gdn_conv_w=gdn_conv_w[l], gdn_a_log=gdn_a_log[l],
                   gdn_dt_bias=gdn_dt_bias[l], gdn_out_norm=gdn_out_norm[l], w_branch=w_branch[l],
                   w_out=w_out[l], ffn2_norm=ffn2_norm[l], ffn2_w_gate=ffn2_w_gate[l], ffn2_w_up=ffn2_w_up[l],
                   ffn2_w_down=ffn2_w_down[l]) for l in range(DEPTH)]

    bp = x_prompt.shape[0]
    h = x_prompt
    kp, vp, sp, cp, gp = [], [], [], [], []
    for l in range(DEPTH):
        h, (k_, v_, s_, c_, g_) = decoder_layer(
            h, layers[l], None,
            jnp.zeros((bp, SSM_GROUPS, SSM_STATE, 2), jnp.float32),
            jnp.zeros((bp, GDN_CONV - 1, CONV_WIDTH), x_prompt.dtype),
            jnp.zeros((bp, GDN_HEADS, GDN_DK, GDN_DV), jnp.float32))
        kp.append(k_); vp.append(v_); sp.append(s_); cp.append(c_); gp.append(g_)
    y_prompt = rms_norm(h, final_norm)

    bd = x_sample.shape[0]
    n_pages = page_table.shape[1]
    h = x_sample
    ks_, vs_, ss_, cs_, gs_ = [], [], [], [], []
    for l in range(DEPTH):
        k_past = cache_k[l][page_table].reshape(bd, n_pages * PAGE_SIZE, SB_HEADS, SB_HEAD_DIM)
        v_past = cache_v[l][page_table].reshape(bd, n_pages * PAGE_SIZE, SB_HEADS, SB_HEAD_DIM)
        h, (k_, v_, s_, c_, g_) = decoder_layer(h, layers[l], (k_past, v_past), state_ssm[l], state_conv[l], state_gdn[l])
        ks_.append(k_); vs_.append(v_); ss_.append(s_); cs_.append(c_); gs_.append(g_)
    y_sample = rms_norm(h, final_norm)

    return (y_prompt, y_sample,
            jnp.stack(kp), jnp.stack(vp), jnp.stack(ks_), jnp.stack(vs_),
            jnp.stack(sp), jnp.stack(ss_), jnp.stack(cp), jnp.stack(cs_),
            jnp.stack(gp), jnp.stack(gs_))
```

```python
import functools
import math

import jax
import jax.numpy as jnp
from jax import lax
from jax.experimental import pallas as pl
from jax.experimental.pallas import tpu as pltpu

F32 = jnp.float32
BF16 = jnp.bfloat16
HIGHEST = lax.Precision.HIGHEST

D_MODEL = 2048
BATCH = 4
SEQ = 2048
DEPTH = 2
DEC_BATCH = 32
PAGE = 128
N_PAGES = 64
HEADS = 8
HEAD_DIM = 128
WIDTH = HEADS * HEAD_DIM
SSM_GROUP = 16
SSM_STATE = 64
CONV_TAPS = 4
CONV_WIDTH = 3 * WIDTH
CHUNK = 64
FFN_DIM = 5632
EPS = 1e-6

N_PROMPT = BATCH * SEQ
TM = 528
ROWS = 16 * TM
SAMPLE_ROW_BLOCK = N_PROMPT // DEC_BATCH

C_Q, C_K, C_V, C_U, C_QKV, C_Z, C_BA, C_GATE = 0, 1024, 2048, 3072, 4096, 7168, 8192, 9216
PROJ_COLS = 15360
TN_PROJ = 1024
TF = 512
TN_MERGE = 512

VMEM_LIMIT = 56 * 1024 * 1024


def _params(*sem):
    return pltpu.CompilerParams(dimension_semantics=sem, vmem_limit_bytes=VMEM_LIMIT)


def _rms(x, gain):
    ms = jnp.mean(x * x, axis=-1, keepdims=True)
    return x * lax.rsqrt(ms + EPS) * gain


def _softplus(x):
    return jnp.maximum(x, 0.0) + jnp.log(1.0 + jnp.exp(-jnp.abs(x)))


def _sigmoid(x):
    return 1.0 / (1.0 + jnp.exp(-x))


def _dot(a, b, **kw):
    return jnp.dot(a, b, preferred_element_type=F32, **kw)


def _dot_nt(a, b, **kw):
    return lax.dot_general(a, b, (((1,), (1,)), ((), ())), preferred_element_type=F32, **kw)


def _ffn_body(x_ref, g_ref, wg_ref, wu_ref, wd_ref, o_ref, h_scr):
    @pl.when(pl.program_id(1) == 0)
    def _():
        x = x_ref[...]
        h_scr[...] = _rms(x, g_ref[...]).astype(BF16)
        o_ref[...] = x

    h = h_scr[...]
    a = _dot(h, wg_ref[...])
    b = _dot(h, wu_ref[...])
    act = (a * _sigmoid(a)) * b * 0.5
    o_ref[...] += _dot(act.astype(BF16), wd_ref[...])


def _ffn(x, gain, wg, wu, wd, layer):
    return pl.pallas_call(
        _ffn_body,
        grid=(ROWS // TM, FFN_DIM // TF),
        in_specs=[
            pl.BlockSpec((TM, D_MODEL), lambda i, f: (i, 0)),
            pl.BlockSpec((None, 1, D_MODEL), lambda i, f: (layer, 0, 0)),
            pl.BlockSpec((None, D_MODEL, TF), lambda i, f: (layer, 0, f)),
            pl.BlockSpec((None, D_MODEL, TF), lambda i, f: (layer, 0, f)),
            pl.BlockSpec((None, TF, D_MODEL), lambda i, f: (layer, f, 0)),
        ],
        out_specs=pl.BlockSpec((TM, D_MODEL), lambda i, f: (i, 0)),
        out_shape=jax.ShapeDtypeStruct((ROWS, D_MODEL), F32),
        scratch_shapes=[pltpu.VMEM((TM, D_MODEL), BF16)],
        compiler_params=_params("parallel", "arbitrary"),
        name="ffn",
    )(x, gain, wg, wu, wd)


def _proj_body(x_ref, g_ref, w_ref, o_ref, h_scr):
    @pl.when(pl.program_id(1) == 0)
    def _():
        h_scr[...] = _rms(x_ref[...], g_ref[...]).astype(BF16)

    o_ref[...] = _dot(h_scr[...], w_ref[...])


def _in_proj(x, gain, w, layer):
    return pl.pallas_call(
        _proj_body,
        grid=(ROWS // TM, PROJ_COLS // TN_PROJ),
        in_specs=[
            pl.BlockSpec((TM, D_MODEL), lambda i, n: (i, 0)),
            pl.BlockSpec((None, 1, D_MODEL), lambda i, n: (layer, 0, 0)),
            pl.BlockSpec((None, D_MODEL, TN_PROJ), lambda i, n: (layer, 0, n)),
        ],
        out_specs=pl.BlockSpec((TM, TN_PROJ), lambda i, n: (i, n)),
        out_shape=jax.ShapeDtypeStruct((ROWS, PROJ_COLS), F32),
        scratch_shapes=[pltpu.VMEM((TM, D_MODEL), BF16)],
        compiler_params=_params("parallel", "arbitrary"),
        name="in_proj",
    )(x, gain, w)


def _glu_body(g_ref, w_ref, o_ref):
    g = g_ref[...]
    o_ref[...] = (g * _sigmoid(_dot(g.astype(BF16), w_ref[...]))).astype(BF16)


def _glu(g, w, layer):
    return pl.pallas_call(
        _glu_body,
        grid=(ROWS // TM,),
        in_specs=[
            pl.BlockSpec((TM, WIDTH), lambda i: (i, 0)),
            pl.BlockSpec((None, WIDTH, WIDTH), lambda i: (layer, 0, 0)),
        ],
        out_specs=pl.BlockSpec((TM, WIDTH), lambda i: (i, 0)),
        out_shape=jax.ShapeDtypeStruct((ROWS, WIDTH), BF16),
        compiler_params=_params("parallel"),
        name="glu",
    )(g, w)


def _merge_body(x_ref, oa_ref, ob_ref, oc_ref, ga_ref, gb_ref, gc_ref, wa_ref, wb_ref, wc_ref, wo_ref, o_ref):
    @pl.when(pl.program_id(1) == 0)
    def _():
        o_ref[...] = x_ref[...]

    m = (_sigmoid(ga_ref[...]) * _dot(oa_ref[...], wa_ref[...])
         + _sigmoid(gb_ref[...]) * _dot(ob_ref[...], wb_ref[...])
         + _sigmoid(gc_ref[...]) * _dot(oc_ref[...], wc_ref[...]))
    o_ref[...] += _dot(m.astype(BF16), wo_ref[...])


def _merge(x, oa, ob, oc, proj, w_branch, w_out, layer):
    gate_blk = C_GATE // TN_MERGE
    per_gate = D_MODEL // TN_MERGE
    o_spec = pl.BlockSpec((TM, WIDTH), lambda i, n: (i, 0))

    def gate_spec(which):
        return pl.BlockSpec((TM, TN_MERGE), lambda i, n: (i, gate_blk + which * per_gate + n))

    def w_spec(which):
        return pl.BlockSpec((None, None, WIDTH, TN_MERGE), lambda i, n: (layer, which, 0, n))

    return pl.pallas_call(
        _merge_body,
        grid=(ROWS // TM, D_MODEL // TN_MERGE),
        in_specs=[
            pl.BlockSpec((TM, D_MODEL), lambda i, n: (i, 0)),
            o_spec, o_spec, o_spec,
            gate_spec(0), gate_spec(1), gate_spec(2),
            w_spec(0), w_spec(1), w_spec(2),
            pl.BlockSpec((None, TN_MERGE, D_MODEL), lambda i, n: (layer, n, 0)),
        ],
        out_specs=pl.BlockSpec((TM, D_MODEL), lambda i, n: (i, 0)),
        out_shape=jax.ShapeDtypeStruct((ROWS, D_MODEL), F32),
        compiler_params=_params("parallel", "arbitrary"),
        name="merge",
    )(x, oa, ob, oc, proj, proj, proj, w_branch, w_branch, w_branch, w_out)


def _final_norm_body(x_ref, g_ref, o_ref):
    o_ref[...] = _rms(x_ref[...], g_ref[...])


def _final_norm(x, gain):
    return pl.pallas_call(
        _final_norm_body,
        grid=(ROWS // TM,),
        in_specs=[pl.BlockSpec((TM, D_MODEL), lambda i: (i, 0)),
                  pl.BlockSpec((1, D_MODEL), lambda i: (0, 0))],
        out_specs=pl.BlockSpec((TM, D_MODEL), lambda i: (i, 0)),
        out_shape=jax.ShapeDtypeStruct((ROWS, D_MODEL), F32),
        compiler_params=_params("parallel"),
        name="final_norm",
    )(x, gain)


SB_BQ = 256
SB_BK = 128
SB_HB = 4
SB_SCALE = HEAD_DIM ** -0.5


def _stick_terms(z):
    t = jnp.log(1.0 + jnp.exp(-jnp.abs(z)))
    return jnp.minimum(z, 0.0) - t, jnp.minimum(-z, 0.0) - t


def _sb_prompt_body(bias_ref, q_ref, k_ref, v_ref, o_ref, acc_scr, run_scr, *, layer):
    hg = pl.program_id(1)
    qi = pl.program_id(2)
    row = lax.broadcasted_iota(jnp.int32, (SB_BQ, SB_BK), 0)
    col = lax.broadcasted_iota(jnp.int32, (SB_BQ, SB_BK), 1)
    q_pos = qi * SB_BQ + row
    r2 = lax.broadcasted_iota(jnp.int32, (SB_BK, 2 * SB_BK), 0)
    c2 = lax.broadcasted_iota(jnp.int32, (SB_BK, 2 * SB_BK), 1)
    suffix_ones = jnp.where((r2 > c2) | (c2 >= SB_BK), 1.0, 0.0).astype(BF16)

    acc_scr[...] = jnp.zeros_like(acc_scr)
    run_scr[...] = jnp.zeros_like(run_scr)
    n_blocks = (qi + 1) * (SB_BQ // SB_BK)

    def block(it, carry):
        j = n_blocks - 1 - it
        k_rows = pl.ds(pl.multiple_of(j * SB_BK, SB_BK), SB_BK)
        mask = (j * SB_BK + col) < q_pos
        for hh in range(SB_HB):
            cs = slice(hh * HEAD_DIM, (hh + 1) * HEAD_DIM)
            bias = bias_ref[layer * HEADS + hg * SB_HB + hh]
            q = q_ref[:, cs].astype(BF16)
            kb = k_ref[k_rows, cs].astype(BF16)
            vb = v_ref[k_rows, cs].astype(BF16)
            z = _dot_nt(q, kb) * SB_SCALE + bias
            log_hit, log_fail = _stick_terms(z)
            log_fail = jnp.where(mask, log_fail, 0.0)
            hi = log_fail.astype(BF16)
            lo = (log_fail - hi.astype(F32)).astype(BF16)
            sums = _dot(hi, suffix_ones) + _dot(lo, suffix_ones)
            run = run_scr[hh]
            log_between = sums[:, :SB_BK] + run
            run_scr[hh] = run + sums[:, SB_BK:]
            w = jnp.where(mask, jnp.exp(log_hit + log_between), 0.0)
            acc_scr[hh] += _dot(w.astype(BF16), vb)
        return carry

    lax.fori_loop(0, n_blocks, block, 0)
    for hh in range(SB_HB):
        o_ref[:, hh * HEAD_DIM:(hh + 1) * HEAD_DIM] = acc_scr[hh].astype(BF16)


def _sb_prompt(proj, bias_flat, layer):
    hw = SB_HB * HEAD_DIM
    n_hg = HEADS // SB_HB
    nq = SEQ // SB_BQ
    grid_spec = pltpu.PrefetchScalarGridSpec(
        num_scalar_prefetch=1,
        grid=(BATCH, n_hg, nq),
        in_specs=[
            pl.BlockSpec((SB_BQ, hw), lambda b, h, i, bias: (b * nq + i, C_Q // hw + h)),
            pl.BlockSpec((SEQ, hw), lambda b, h, i, bias: (b, C_K // hw + h)),
            pl.BlockSpec((SEQ, hw), lambda b, h, i, bias: (b, C_V // hw + h)),
        ],
        out_specs=pl.BlockSpec((SB_BQ, hw), lambda b, h, i, bias: (b * nq + i, h)),
        scratch_shapes=[pltpu.VMEM((SB_HB, SB_BQ, HEAD_DIM), F32),
                        pltpu.VMEM((SB_HB, SB_BQ, SB_BK), F32)],
    )
    return pl.pallas_call(
        functools.partial(_sb_prompt_body, layer=layer),
        grid_spec=grid_spec,
        out_shape=jax.ShapeDtypeStruct((N_PROMPT, WIDTH), BF16),
        compiler_params=_params("parallel", "parallel", "arbitrary"),
        name="sb_prompt",
    )(bias_flat, proj, proj, proj)


DEC_PAGES = 8
DEC_STEPS = N_PAGES // DEC_PAGES
DEC_LANES = PAGE * HEADS


def _lane_shift_up(x, s):
    n = x.shape[1]
    lane = lax.broadcasted_iota(jnp.int32, x.shape, 1)
    return jnp.where(lane < n - s, pltpu.roll(x, n - s, 1), 0.0)


def _sb_decode_body(pt_ref, q_ref, bias_ref, *refs):
    k_refs = refs[:DEC_PAGES]
    v_refs = refs[DEC_PAGES:2 * DEC_PAGES]
    o_ref, acc_scr, run_scr = refs[2 * DEC_PAGES:]
    g = pl.program_id(1)

    @pl.when(g == 0)
    def _():
        acc_scr[...] = jnp.zeros_like(acc_scr)
        run_scr[...] = jnp.zeros_like(run_scr)

    sub = lax.broadcasted_iota(jnp.int32, (HEADS, DEC_LANES), 0)
    lane = lax.broadcasted_iota(jnp.int32, (HEADS, DEC_LANES), 1)
    own_head = (lane & (HEADS - 1)) == sub
    q = q_ref[...].astype(BF16)
    rows = []
    for i in range(DEC_PAGES):
        k2 = k_refs[i][...].reshape(DEC_LANES, HEAD_DIM).astype(BF16)
        zt = _dot_nt(q, k2)
        rows.append(jnp.sum(jnp.where(own_head, zt, 0.0), axis=0, keepdims=True))
    z = jnp.concatenate(rows, axis=0) * SB_SCALE + bias_ref[...]
    log_hit, log_fail = _stick_terms(z)

    suffix = log_fail
    s = HEADS
    while s < DEC_LANES:
        suffix = suffix + _lane_shift_up(suffix, s)
        s *= 2
    page_lane = lax.broadcasted_iota(jnp.int32, (DEC_PAGES, DEC_LANES), 1)
    tot = jnp.where(page_lane < HEADS, suffix, 0.0)
    s = HEADS
    while s < DEC_LANES:
        tot = tot + pltpu.roll(tot, s, 1)
        s *= 2
    page_r = lax.broadcasted_iota(jnp.int32, (DEC_PAGES, DEC_PAGES), 0)
    page_c = lax.broadcasted_iota(jnp.int32, (DEC_PAGES, DEC_PAGES), 1)
    later = jnp.where(page_c > page_r, 1.0, 0.0)
    after = _dot(later, tot, precision=HIGHEST)
    run = run_scr[...]
    log_between = (suffix - log_fail) + after + run
    run_scr[...] = run + jnp.sum(tot, axis=0, keepdims=True)
    w = jnp.exp(log_hit + log_between)

    acc = acc_scr[...]
    for i in range(DEC_PAGES):
        wm = jnp.where(own_head, jnp.broadcast_to(w[i:i + 1, :], (HEADS, DEC_LANES)), 0.0).astype(BF16)
        v2 = v_refs[i][...].reshape(DEC_LANES, HEAD_DIM).astype(BF16)
        acc = acc + _dot(wm, v2)
    acc_scr[...] = acc

    @pl.when(g == DEC_STEPS - 1)
    def _():
        o_ref[...] = acc


def _sb_decode(q, cache_k, cache_v, page_table_flat, bias_lanes, layer):
    def page_spec(i):
        def index(b, g, pt):
            return (layer, pt[b * N_PAGES + (DEC_STEPS - 1 - g) * DEC_PAGES + i], 0, 0, 0)
        return pl.BlockSpec((None, None, PAGE, HEADS, HEAD_DIM), index)

    grid_spec = pltpu.PrefetchScalarGridSpec(
        num_scalar_prefetch=1,
        grid=(DEC_BATCH, DEC_STEPS),
        in_specs=[pl.BlockSpec((None, HEADS, HEAD_DIM), lambda b, g, pt: (b, 0, 0)),
                  pl.BlockSpec((None, 1, DEC_LANES), lambda b, g, pt: (layer, 0, 0))]
        + [page_spec(i) for i in range(DEC_PAGES)] * 2,
        out_specs=pl.BlockSpec((None, HEADS, HEAD_DIM), lambda b, g, pt: (b, 0, 0)),
        scratch_shapes=[pltpu.VMEM((HEADS, HEAD_DIM), F32), pltpu.VMEM((1, DEC_LANES), F32)],
    )
    return pl.pallas_call(
        _sb_decode_body,
        grid_spec=grid_spec,
        out_shape=jax.ShapeDtypeStruct((DEC_BATCH, HEADS, HEAD_DIM), F32),
        compiler_params=_params("parallel", "arbitrary"),
        name="sb_decode",
    )(page_table_flat, q, bias_lanes, *([cache_k] * DEC_PAGES), *([cache_v] * DEC_PAGES))


S5_GB = 8
S5_CH = S5_GB * SSM_GROUP
S5_ST = S5_GB * SSM_STATE
S5_NGB = WIDTH // S5_CH
S5_LANES = 8
S5_SEG = SEQ // S5_LANES


def _s5_discretize(par):
    ar, ai, ldt = par[0:1], par[1:2], par[2:3]
    dt = jnp.exp(ldt)
    mag = jnp.exp(dt * ar)
    abr = mag * jnp.cos(dt * ai)
    abi = mag * jnp.sin(dt * ai)
    den = ar * ar + ai * ai
    cr = ((abr - 1.0) * ar + abi * ai) / den
    ci = (abi * ar - (abr - 1.0) * ai) / den
    return abr, abi, cr, ci


def _gelu(y):
    return 0.5 * y * (1.0 + jnp.tanh(math.sqrt(2.0 / math.pi) * (y + 0.044715 * (y * y * y))))


def _s5_prompt_body(u_ref, par_ref, bre_ref, bim_ref, cre_ref, cim_ref, d_ref, h0r_ref, h0i_ref,
                    o_ref, hr_ref, hi_ref, xr_scr, xi_scr):
    abr, abi, cr, ci = _s5_discretize(par_ref[...])
    bre, bim = bre_ref[...], bim_ref[...]
    bbr = (cr * bre - ci * bim).astype(BF16)
    bbi = (cr * bim + ci * bre).astype(BF16)
    u = u_ref[...]
    ub = u.astype(BF16)
    xr_scr[...] = _dot(ub, bbr)
    xi_scr[...] = _dot(ub, bbi)

    shape = (S5_LANES, S5_ST)
    ar8 = jnp.broadcast_to(abr, shape)
    ai8 = jnp.broadcast_to(abi, shape)

    def local_step(t, carry):
        xr, xi = carry
        rows = pl.ds(pl.multiple_of(t * S5_LANES, S5_LANES), S5_LANES)
        nr = ar8 * xr - ai8 * xi + xr_scr[rows, :]
        ni = ar8 * xi + ai8 * xr + xi_scr[rows, :]
        xr_scr[rows, :] = nr
        xi_scr[rows, :] = ni
        return nr, ni

    zero = jnp.zeros(shape, F32)
    fr, fi = lax.fori_loop(0, S5_SEG, local_step, (zero, zero), unroll=8)

    pr, pi = abr, abi
    for _ in range(int(math.log2(S5_SEG))):
        pr, pi = pr * pr - pi * pi, 2.0 * pr * pi
    hr, hi = h0r_ref[...], h0i_ref[...]
    in_r, in_i = [], []
    for j in range(S5_LANES):
        in_r.append(hr)
        in_i.append(hi)
        hr, hi = fr[j:j + 1] + pr * hr - pi * hi, fi[j:j + 1] + pr * hi + pi * hr
    hr_ref[...] = hr
    hi_ref[...] = hi
    cin_r = jnp.concatenate(in_r, axis=0)
    cin_i = jnp.concatenate(in_i, axis=0)

    def fix_step(t, carry):
        wr, wi = carry
        rows = pl.ds(pl.multiple_of(t * S5_LANES, S5_LANES), S5_LANES)
        xr_scr[rows, :] = xr_scr[rows, :] + wr * cin_r - wi * cin_i
        xi_scr[rows, :] = xi_scr[rows, :] + wr * cin_i + wi * cin_r
        return wr * ar8 - wi * ai8, wr * ai8 + wi * ar8

    lax.fori_loop(0, S5_SEG, fix_step, (ar8, ai8), unroll=8)

    y = (_dot(xr_scr[...].astype(BF16), cre_ref[...].astype(BF16))
         - _dot(xi_scr[...].astype(BF16), cim_ref[...].astype(BF16)) + d_ref[...] * u)
    o_ref[...] = _gelu(y)


def _s5_param_specs(layer):
    lead = (lambda *a: a[-1])

    def spec(shape):
        return pl.BlockSpec((None, None) + shape, lambda *a: (layer, lead(*a), 0, 0))

    return [spec((8, S5_ST)), spec((S5_CH, S5_ST)), spec((S5_CH, S5_ST)),
            spec((S5_ST, S5_CH)), spec((S5_ST, S5_CH)), spec((1, S5_CH))]


def _s5_prompt(u_perm, s5p, layer):
    zeros = jnp.zeros((BATCH, S5_NGB, 1, S5_ST), F32)
    state_spec = pl.BlockSpec((None, None, 1, S5_ST), lambda b, g: (b, g, 0, 0))
    out, hr, hi = pl.pallas_call(
        _s5_prompt_body,
        grid=(BATCH, S5_NGB),
        in_specs=[pl.BlockSpec((SEQ, S5_CH), lambda b, g: (b, g))] + _s5_param_specs(layer)
        + [state_spec, state_spec],
        out_specs=[pl.BlockSpec((SEQ, S5_CH), lambda b, g: (b, g)), state_spec, state_spec],
        out_shape=[jax.ShapeDtypeStruct((N_PROMPT, WIDTH), F32),
                   jax.ShapeDtypeStruct((BATCH, S5_NGB, 1, S5_ST), F32),
                   jax.ShapeDtypeStruct((BATCH, S5_NGB, 1, S5_ST), F32)],
        scratch_shapes=[pltpu.VMEM((SEQ, S5_ST), F32), pltpu.VMEM((SEQ, S5_ST), F32)],
        compiler_params=_params("parallel", "parallel"),
        name="s5_prompt",
    )(u_perm, *s5p, zeros, zeros)
    return out, hr, hi


def _s5_sample_body(u_ref, par_ref, bre_ref, bim_ref, cre_ref, cim_ref, d_ref, h0r_ref, h0i_ref,
                    o_ref, hr_ref, hi_ref):
    abr, abi, cr, ci = _s5_discretize(par_ref[...])
    bre, bim = bre_ref[...], bim_ref[...]
    bbr = (cr * bre - ci * bim).astype(BF16)
    bbi = (cr * bim + ci * bre).astype(BF16)
    u = u_ref[...]
    ub = u.astype(BF16)
    h0r, h0i = h0r_ref[...], h0i_ref[...]
    xr = _dot(ub, bbr) + abr * h0r - abi * h0i
    xi = _dot(ub, bbi) + abr * h0i + abi * h0r
    hr_ref[...] = xr
    hi_ref[...] = xi
    y = (_dot(xr.astype(BF16), cre_ref[...].astype(BF16))
         - _dot(xi.astype(BF16), cim_ref[...].astype(BF16)) + d_ref[...] * u)
    o_ref[...] = _gelu(y)


def _s5_sample(proj, s5p, h0r, h0i, layer):
    state_spec = pl.BlockSpec((DEC_BATCH, S5_ST), lambda g: (0, g))
    return pl.pallas_call(
        _s5_sample_body,
        grid=(S5_NGB,),
        in_specs=[pl.BlockSpec((DEC_BATCH, S5_CH), lambda g: (SAMPLE_ROW_BLOCK, C_U // S5_CH + g))]
        + _s5_param_specs(layer) + [state_spec, state_spec],
        out_specs=[pl.BlockSpec((DEC_BATCH, S5_CH), lambda g: (0, g)), state_spec, state_spec],
        out_shape=[jax.ShapeDtypeStruct((DEC_BATCH, WIDTH), F32),
                   jax.ShapeDtypeStruct((DEC_BATCH, S5_NGB * S5_ST), F32),
                   jax.ShapeDtypeStruct((DEC_BATCH, S5_NGB * S5_ST), F32)],
        compiler_params=_params("parallel"),
        name="s5_sample",
    )(proj, *s5p, h0r, h0i)


GDN_TS = 512
GDN_HB = 4
GDN_SCALE = HEAD_DIM ** -0.5


def _l2norm_heads(xc, o_ref):
    for h in range(HEADS):
        cs = slice(h * HEAD_DIM, (h + 1) * HEAD_DIM)
        seg = xc[:, cs]
        o_ref[:, cs] = seg * lax.rsqrt(jnp.sum(seg * seg, axis=-1, keepdims=True) + EPS)


def _conv_prompt_body(x_ref, prev_ref, w_ref, o_ref, ext_scr):
    i = pl.program_id(1)
    part = pl.program_id(2)
    x = x_ref[...]
    ext_scr[0:8, :] = jnp.where(i == 0, 0.0, prev_ref[...])
    ext_scr[8:8 + GDN_TS, :] = x
    w = w_ref[...]
    acc = x * w[3:4]
    for tap in range(CONV_TAPS - 1):
        back = CONV_TAPS - 1 - tap
        acc = acc + ext_scr[8 - back:8 - back + GDN_TS, :] * w[tap:tap + 1]
    xc = acc * _sigmoid(acc)

    @pl.when(part < 2)
    def _():
        _l2norm_heads(xc, o_ref)

    @pl.when(part == 2)
    def _():
        o_ref[...] = xc


def _conv_prompt(proj, conv_w, layer):
    n_t = SEQ // GDN_TS
    col0 = C_QKV // WIDTH

    def prev_index(b, i, p):
        return (jnp.maximum(b * (SEQ // 8) + i * (GDN_TS // 8) - 1, 0), col0 + p)

    return pl.pallas_call(
        _conv_prompt_body,
        grid=(BATCH, n_t, 3),
        in_specs=[
            pl.BlockSpec((GDN_TS, WIDTH), lambda b, i, p: (b * n_t + i, col0 + p)),
            pl.BlockSpec((8, WIDTH), prev_index),
            pl.BlockSpec((None, 8, WIDTH), lambda b, i, p: (layer, 0, p)),
        ],
        out_specs=pl.BlockSpec((GDN_TS, WIDTH), lambda b, i, p: (b * n_t + i, p)),
        out_shape=jax.ShapeDtypeStruct((N_PROMPT, CONV_WIDTH), F32),
        scratch_shapes=[pltpu.VMEM((GDN_TS + 8, WIDTH), F32)],
        compiler_params=_params("parallel", "parallel", "parallel"),
        name="gdn_conv_prompt",
    )(proj, proj, conv_w)


def _conv_sample_body(x_ref, hist_ref, w_ref, o_ref):
    part = pl.program_id(0)
    w = w_ref[...]
    acc = x_ref[...] * w[3:4]
    for tap in range(CONV_TAPS - 1):
        acc = acc + hist_ref[tap] * w[tap:tap + 1]
    xc = acc * _sigmoid(acc)

    @pl.when(part < 2)
    def _():
        _l2norm_heads(xc, o_ref)

    @pl.when(part == 2)
    def _():
        o_ref[...] = xc


def _conv_sample(proj, hist, conv_w, layer):
    col0 = C_QKV // WIDTH
    return pl.pallas_call(
        _conv_sample_body,
        grid=(3,),
        in_specs=[
            pl.BlockSpec((DEC_BATCH, WIDTH), lambda p: (SAMPLE_ROW_BLOCK, col0 + p)),
            pl.BlockSpec((None, CONV_TAPS - 1, DEC_BATCH, WIDTH), lambda p: (layer, 0, 0, p)),
            pl.BlockSpec((None, 8, WIDTH), lambda p: (layer, 0, p)),
        ],
        out_specs=pl.BlockSpec((DEC_BATCH, WIDTH), lambda p: (0, p)),
        out_shape=jax.ShapeDtypeStruct((DEC_BATCH, CONV_WIDTH), F32),
        compiler_params=_params("parallel"),
        name="gdn_conv_sample",
    )(proj, hist, conv_w)


def _gdn_gates_body(ba_ref, par_ref, beta_ref, gc_ref, *, rows, chunk):
    ba = ba_ref[...]
    par = par_ref[...]
    beta = _sigmoid(ba)
    g = -jnp.exp(par[0:1]) * _softplus(ba + par[1:2])
    if chunk > 1:
        r = lax.broadcasted_iota(jnp.int32, (rows, rows), 0)
        c = lax.broadcasted_iota(jnp.int32, (rows, rows), 1)
        tri = jnp.where((r >= c) & ((r & -chunk) == (c & -chunk)), 1.0, 0.0)
        g = _dot(tri, g, precision=HIGHEST)
    for h in range(HEADS):
        cs = slice(h * HEAD_DIM, (h + 1) * HEAD_DIM)
        beta_ref[:, cs] = jnp.broadcast_to(beta[:, h:h + 1], (rows, HEAD_DIM))
        gc_ref[:, cs] = jnp.broadcast_to(g[:, HEADS + h:HEADS + h + 1], (rows, HEAD_DIM))


def _gdn_gates(proj, gate_par, layer, rows, n_blocks, first_block, chunk):
    out = jax.ShapeDtypeStruct((rows * n_blocks, WIDTH), F32)
    return pl.pallas_call(
        functools.partial(_gdn_gates_body, rows=rows, chunk=chunk),
        grid=(n_blocks,),
        in_specs=[pl.BlockSpec((rows, 128), lambda i: (first_block + i, C_BA // 128)),
                  pl.BlockSpec((None, 8, 128), lambda i: (layer, 0, 0))],
        out_specs=[pl.BlockSpec((rows, WIDTH), lambda i: (i, 0))] * 2,
        out_shape=[out, out],
        compiler_params=_params("parallel"),
        name="gdn_gates",
    )(proj, gate_par)


def _gdn_chunk_body(q_ref, k_ref, v_ref, beta_ref, gc_ref, z_ref, s0_ref, gn_ref, o_ref, sfin_ref, s_scr,
                    *, n_tiles, tile_rows):
    ti = pl.program_id(2)

    @pl.when(ti == 0)
    def _():
        s_scr[...] = s0_ref[...]

    row = lax.broadcasted_iota(jnp.int32, (CHUNK, CHUNK), 0)
    col = lax.broadcasted_iota(jnp.int32, (CHUNK, CHUNK), 1)
    incl = row >= col
    strict = row > col
    eye = jnp.where(row == col, 1.0, 0.0)
    lane = lax.broadcasted_iota(jnp.int32, (CHUNK, HEAD_DIM), 1)
    gain = gn_ref[...]

    def chunk(c, carry):
        rows = pl.ds(pl.multiple_of(c * CHUNK, CHUNK), CHUNK)
        for hh in range(GDN_HB):
            cs = slice(hh * HEAD_DIM, (hh + 1) * HEAD_DIM)
            k = k_ref[rows, cs]
            q = q_ref[rows, cs] * GDN_SCALE
            v = v_ref[rows, cs]
            beta = beta_ref[rows, cs]
            gc = gc_ref[rows, cs]
            left = jnp.where(lane == 0, gc, jnp.where(lane == 1, 1.0, 0.0))
            right = jnp.where(lane == 0, 1.0, jnp.where(lane == 1, -gc, 0.0))
            diff = _dot_nt(left, right, precision=HIGHEST)
            decay = jnp.where(incl, jnp.exp(jnp.where(incl, diff, 0.0)), 0.0)
            kb = k * beta
            k16 = k.astype(BF16)
            m = jnp.where(strict, _dot_nt(kb.astype(BF16), k16) * decay, 0.0)
            qk = _dot_nt(q.astype(BF16), k16) * decay
            inv = eye - m
            power = _dot(m, m, precision=HIGHEST)
            for step in range(5):
                inv = inv + _dot(inv, power, precision=HIGHEST)
                if step < 4:
                    power = _dot(power, power, precision=HIGHEST)
            egc = jnp.exp(gc)
            rhs = jnp.concatenate([v * beta, kb * egc], axis=-1)
            sol = _dot(inv, rhs, precision=HIGHEST)
            u = sol[:, :HEAD_DIM]
            w = sol[:, HEAD_DIM:]
            gc_last = jnp.broadcast_to(gc[CHUNK - 1:CHUNK, :], (CHUNK, HEAD_DIM))
            q_dec = (q * egc).astype(BF16)
            k_dec = (k * jnp.exp(gc_last - gc)).astype(BF16)
            g_last = jnp.exp(gc_last[0:1, :])

            state = s_scr[hh]
            s16 = state.astype(BF16)
            v_new = u - _dot(w.astype(BF16), s16)
            vn16 = v_new.astype(BF16)
            o = _dot(q_dec, s16) + _dot(qk.astype(BF16), vn16)
            s_scr[hh] = state * g_last + lax.dot_general(
                k_dec, vn16, (((0,), (0,)), ((), ())), preferred_element_type=F32)
            zg = z_ref[rows, cs]
            o = _rms(o, gain) * (zg * _sigmoid(zg))
            o_ref[rows, cs] = o.astype(BF16)
        return carry

    lax.fori_loop(0, tile_rows // CHUNK, chunk, 0)

    @pl.when(ti == n_tiles - 1)
    def _():
        sfin_ref[...] = s_scr[...]


def _gdn_chunks(qkv, beta, gc, z_src, z_col_block, s0, out_norm, layer, n_seq, seq_rows):
    tile_rows = min(seq_rows, GDN_TS)
    n_tiles = seq_rows // tile_rows
    hw = GDN_HB * HEAD_DIM
    n_hg = HEADS // GDN_HB

    def act(col_block):
        return pl.BlockSpec((tile_rows, hw), lambda b, h, t: (b * n_tiles + t, col_block + h))

    state_spec = pl.BlockSpec((None, GDN_HB, HEAD_DIM, HEAD_DIM), lambda b, h, t: (b, h, 0, 0))
    return pl.pallas_call(
        functools.partial(_gdn_chunk_body, n_tiles=n_tiles, tile_rows=tile_rows),
        grid=(n_seq, n_hg, n_tiles),
        in_specs=[act(0), act(n_hg), act(2 * n_hg), act(0), act(0), act(z_col_block), state_spec,
                  pl.BlockSpec((None, 1, HEAD_DIM), lambda b, h, t: (layer, 0, 0))],
        out_specs=[act(0), state_spec],
        out_shape=[jax.ShapeDtypeStruct((n_seq * seq_rows, WIDTH), BF16),
                   jax.ShapeDtypeStruct((n_seq, HEADS, HEAD_DIM, HEAD_DIM), F32)],
        scratch_shapes=[pltpu.VMEM((GDN_HB, HEAD_DIM, HEAD_DIM), F32)],
        compiler_params=_params("parallel", "parallel", "arbitrary"),
        name="gdn_chunks",
    )(qkv, qkv, qkv, beta, gc, z_src, s0, out_norm)


def _pack_w_in(w_in):
    pad = jnp.zeros((DEPTH, D_MODEL, C_GATE - C_BA - 2 * HEADS), w_in.dtype)
    return jnp.concatenate([w_in[..., :C_BA], w_in[..., C_BA:C_BA + 2 * HEADS], pad,
                            w_in[..., C_BA + 2 * HEADS:]], axis=-1).astype(BF16)


def _s5_params(a_re, a_im, log_dt, b_re, b_im, c_re, c_im, d):
    eye = jnp.eye(S5_GB, dtype=F32)

    def state_rows(x):
        return x.reshape(DEPTH, S5_NGB, 1, S5_ST)

    par = jnp.concatenate(
        [state_rows(a_re), state_rows(a_im),
         state_rows(jnp.repeat(log_dt, SSM_STATE, axis=-1)),
         jnp.zeros((DEPTH, S5_NGB, 5, S5_ST), F32)], axis=2)

    def b_blockdiag(b):
        b = b.reshape(DEPTH, S5_NGB, S5_GB, SSM_STATE, SSM_GROUP)
        return jnp.einsum('dbgpc,gh->dbgchp', b, eye).reshape(DEPTH, S5_NGB, S5_CH, S5_ST)

    def c_blockdiag(c):
        c = c.reshape(DEPTH, S5_NGB, S5_GB, SSM_GROUP, SSM_STATE)
        return jnp.einsum('dbgcp,gh->dbgphc', c, eye).reshape(DEPTH, S5_NGB, S5_ST, S5_CH)

    return (par, b_blockdiag(b_re), b_blockdiag(b_im), c_blockdiag(c_re), c_blockdiag(c_im),
            d.reshape(DEPTH, S5_NGB, 1, S5_CH))


def _segment_major(x):
    w = x.shape[-1]
    return x.reshape(BATCH, S5_LANES, S5_SEG, w).transpose(0, 2, 1, 3).reshape(N_PROMPT, w)


def _token_major(x):
    w = x.shape[-1]
    return x.reshape(BATCH, S5_SEG, S5_LANES, w).transpose(0, 2, 1, 3).reshape(N_PROMPT, w)


def _pad_rows(x):
    return jnp.concatenate([x, jnp.zeros((ROWS - x.shape[0], x.shape[1]), x.dtype)], axis=0)


def _one_row_chunks(x):
    return jnp.pad(x[:, None, :], ((0, 0), (0, CHUNK - 1), (0, 0))).reshape(DEC_BATCH * CHUNK, x.shape[-1])


def kernel(x_prompt, x_sample, cache_k, cache_v, page_table, state_ssm, state_conv, state_gdn, ffn1_norm, ffn1_w_gate, ffn1_w_up, ffn1_w_down, mix_norm, w_in, sb_bias, ssm_a_re, ssm_a_im, ssm_log_dt, ssm_b_re, ssm_b_im, ssm_c_re, ssm_c_im, ssm_d, ssm_w_glu, gdn_conv_w, gdn_a_log, gdn_dt_bias, gdn_out_norm, w_branch, w_out, ffn2_norm, ffn2_w_gate, ffn2_w_up, ffn2_w_down, final_norm):
    bf = lambda w: w.astype(BF16)
    gain3 = lambda g: g.reshape(DEPTH, 1, D_MODEL)
    w_in_p = _pack_w_in(w_in)
    ffn1 = (gain3(ffn1_norm), bf(ffn1_w_gate), bf(ffn1_w_up), bf(ffn1_w_down))
    ffn2 = (gain3(ffn2_norm), bf(ffn2_w_gate), bf(ffn2_w_up), bf(ffn2_w_down))
    mix_gain = gain3(mix_norm)
    w_glu, w_br, w_o = bf(ssm_w_glu), bf(w_branch).reshape(DEPTH, 3, WIDTH, D_MODEL), bf(w_out)
    s5p = _s5_params(ssm_a_re, ssm_a_im, ssm_log_dt, ssm_b_re, ssm_b_im, ssm_c_re, ssm_c_im, ssm_d)
    bias_flat = sb_bias.reshape(DEPTH * HEADS)
    bias_lanes = jnp.tile(sb_bias, (1, PAGE)).reshape(DEPTH, 1, DEC_LANES)
    page_flat = page_table.reshape(DEC_BATCH * N_PAGES)
    conv_w8 = jnp.pad(gdn_conv_w, ((0, 0), (0, 8 - CONV_TAPS), (0, 0)))
    conv_hist = state_conv.transpose(0, 2, 1, 3)
    lane_pad = jnp.zeros((DEPTH, 128 - 2 * HEADS), F32)
    gate_par = jnp.stack([jnp.concatenate([jnp.zeros((DEPTH, HEADS), F32), gdn_a_log, lane_pad], axis=1),
                          jnp.concatenate([jnp.zeros((DEPTH, HEADS), F32), gdn_dt_bias, lane_pad], axis=1)]
                         + [jnp.zeros((DEPTH, 128), F32)] * 6, axis=1)
    out_norm = gdn_out_norm.reshape(DEPTH, 1, HEAD_DIM)
    ssm_h0r = state_ssm[..., 0].reshape(DEPTH, DEC_BATCH, WIDTH * SSM_STATE // SSM_GROUP)
    ssm_h0i = state_ssm[..., 1].reshape(DEPTH, DEC_BATCH, WIDTH * SSM_STATE // SSM_GROUP)

    x = _pad_rows(jnp.concatenate([x_prompt.reshape(N_PROMPT, D_MODEL),
                                   x_sample.reshape(DEC_BATCH, D_MODEL)], axis=0))
    sample = slice(N_PROMPT, N_PROMPT + DEC_BATCH)
    kp, vp, ks, vs, sp, ss, cp, cs_, gp, gs = ([] for _ in range(10))
    for l in range(DEPTH):
        x = _ffn(x, *ffn1, l)
        proj = _in_proj(x, mix_gain, w_in_p, l)

        oa_p = _sb_prompt(proj, bias_flat, l)
        q_s = proj[sample, C_Q:C_Q + WIDTH].reshape(DEC_BATCH, HEADS, HEAD_DIM)
        oa_s = _sb_decode(q_s, cache_k, cache_v, page_flat, bias_lanes, l)
        oa = _pad_rows(jnp.concatenate([oa_p, oa_s.reshape(DEC_BATCH, WIDTH).astype(BF16)], axis=0))

        gb_p, hr_p, hi_p = _s5_prompt(_segment_major(proj[:N_PROMPT, C_U:C_U + WIDTH]), s5p, l)
        gb_s, hr_s, hi_s = _s5_sample(proj, s5p, ssm_h0r[l], ssm_h0i[l], l)
        ob = _glu(_pad_rows(jnp.concatenate([_token_major(gb_p), gb_s], axis=0)), w_glu, l)

        qkv_p = _conv_prompt(proj, conv_w8, l)
        beta_p, gc_p = _gdn_gates(proj, gate_par, l, GDN_TS, N_PROMPT // GDN_TS, 0, CHUNK)
        oc_p, gdn_p = _gdn_chunks(qkv_p, beta_p, gc_p, proj, C_Z // (GDN_HB * HEAD_DIM),
                                  jnp.zeros((BATCH, HEADS, HEAD_DIM, HEAD_DIM), F32), out_norm, l, BATCH, SEQ)
        qkv_s = _conv_sample(proj, conv_hist, conv_w8, l)
        beta_s, g_s = _gdn_gates(proj, gate_par, l, DEC_BATCH, 1, SAMPLE_ROW_BLOCK, 1)
        gc_s = jnp.broadcast_to(g_s[:, None, :], (DEC_BATCH, CHUNK, WIDTH)).reshape(DEC_BATCH * CHUNK, WIDTH)
        oc_s, gdn_s = _gdn_chunks(_one_row_chunks(qkv_s), _one_row_chunks(beta_s), gc_s,
                                  _one_row_chunks(proj[sample, C_Z:C_Z + WIDTH]), 0,
                                  state_gdn[l], out_norm, l, DEC_BATCH, CHUNK)
        oc = _pad_rows(jnp.concatenate([oc_p, oc_s[::CHUNK]], axis=0))

        x = _merge(x, oa, ob, oc, proj, w_br, w_o, l)
        x = _ffn(x, *ffn2, l)

        kv = lambda c, rows, n: proj[rows, c:c + WIDTH].reshape(n, -1, HEADS, HEAD_DIM)
        kp.append(kv(C_K, slice(0, N_PROMPT), BATCH))
        vp.append(kv(C_V, slice(0, N_PROMPT), BATCH))
        ks.append(kv(C_K, sample, DEC_BATCH))
        vs.append(kv(C_V, sample, DEC_BATCH))
        state = lambda r, i, n: jnp.stack([r.reshape(n, WIDTH // SSM_GROUP, SSM_STATE),
                                           i.reshape(n, WIDTH // SSM_GROUP, SSM_STATE)], axis=-1)
        sp.append(state(hr_p, hi_p, BATCH))
        ss.append(state(hr_s, hi_s, DEC_BATCH))
        pre_conv = proj[:, C_QKV:C_QKV + CONV_WIDTH]
        cp.append(pre_conv[:N_PROMPT].reshape(BATCH, SEQ, CONV_WIDTH)[:, SEQ - (CONV_TAPS - 1):])
        cs_.append(jnp.concatenate([state_conv[l][:, 1:], pre_conv[sample][:, None, :]], axis=1))
        gp.append(gdn_p)
        gs.append(gdn_s)

    y = _final_norm(x, final_norm.reshape(1, D_MODEL))
    return (y[:N_PROMPT].reshape(BATCH, SEQ, D_MODEL), y[sample].reshape(DEC_BATCH, 1, D_MODEL),
            jnp.stack(kp), jnp.stack(vp), jnp.stack(ks), jnp.stack(vs),
            jnp.stack(sp), jnp.stack(ss), jnp.stack(cp), jnp.stack(cs_),
            jnp.stack(gp), jnp.stack(gs))
```

```python
import functools
import math

import jax
import jax.numpy as jnp
from jax import lax
from jax.experimental import pallas as pl
from jax.experimental.pallas import tpu as pltpu

F32 = jnp.float32
BF16 = jnp.bfloat16
HIGHEST = lax.Precision.HIGHEST

D_MODEL = 2048
BATCH = 4
SEQ = 2048
DEPTH = 2
DEC_BATCH = 32
PAGE = 128
N_PAGES = 64
HEADS = 8
HEAD_DIM = 128
WIDTH = HEADS * HEAD_DIM
SSM_GROUP = 16
SSM_STATE = 64
CONV_TAPS = 4
CONV_WIDTH = 3 * WIDTH
FFN_DIM = 5632
EPS = 1e-6

N_PROMPT = BATCH * SEQ
TM = 528
ROWS = 16 * TM
SAMPLE_ROW_BLOCK = N_PROMPT // DEC_BATCH

C_Q, C_K, C_V, C_U, C_QKV, C_Z, C_BA, C_GATE = 0, 1024, 2048, 3072, 4096, 7168, 8192, 9216
PROJ_COLS = 15360
TN_PROJ = 1024
TF = 512
TM_BIG = 2 * TM
TN_MERGE = 512

VMEM_LIMIT = 56 * 1024 * 1024


def _params(*sem):
    return pltpu.CompilerParams(dimension_semantics=sem, vmem_limit_bytes=VMEM_LIMIT)


def _rms(x, gain):
    ms = jnp.mean(x * x, axis=-1, keepdims=True)
    return x * lax.rsqrt(ms + EPS) * gain


def _softplus(x):
    return jnp.maximum(x, 0.0) + jnp.log(1.0 + jnp.exp(-jnp.abs(x)))


def _sigmoid(x):
    return 1.0 / (1.0 + jnp.exp(-x))


def _dot(a, b, **kw):
    return jnp.dot(a, b, preferred_element_type=F32, **kw)


def _dot_nt(a, b, **kw):
    return lax.dot_general(a, b, (((1,), (1,)), ((), ())), preferred_element_type=F32, **kw)


def _ffn_body(x_ref, g_ref, wg_ref, wu_ref, wd_ref, o_ref, h_scr):
    @pl.when(pl.program_id(1) == 0)
    def _():
        x = x_ref[...]
        h_scr[...] = _rms(x, g_ref[...]).astype(BF16)
        o_ref[...] = x

    h = h_scr[...]
    a = _dot(h, wg_ref[...])
    b = _dot(h, wu_ref[...])
    act = (a * _sigmoid(a)) * b * 0.5
    o_ref[...] += _dot(act.astype(BF16), wd_ref[...])


def _ffn(x, gain, wg, wu, wd, layer):
    return pl.pallas_call(
        _ffn_body,
        grid=(ROWS // TM_BIG, FFN_DIM // TF),
        in_specs=[
            pl.BlockSpec((TM_BIG, D_MODEL), lambda i, f: (i, 0), pipeline_mode=pl.Buffered(1)),
            pl.BlockSpec((None, 1, D_MODEL), lambda i, f: (layer, 0, 0)),
            pl.BlockSpec((None, D_MODEL, TF), lambda i, f: (layer, 0, f)),
            pl.BlockSpec((None, D_MODEL, TF), lambda i, f: (layer, 0, f)),
            pl.BlockSpec((None, TF, D_MODEL), lambda i, f: (layer, f, 0)),
        ],
        out_specs=pl.BlockSpec((TM_BIG, D_MODEL), lambda i, f: (i, 0)),
        out_shape=jax.ShapeDtypeStruct((ROWS, D_MODEL), F32),
        scratch_shapes=[pltpu.VMEM((TM_BIG, D_MODEL), BF16)],
        compiler_params=_params("parallel", "arbitrary"),
        name="ffn",
    )(x, gain, wg, wu, wd)


def _proj_body(x_ref, g_ref, w_ref, o_ref, h_scr):
    @pl.when(pl.program_id(1) == 0)
    def _():
        h_scr[...] = _rms(x_ref[...], g_ref[...]).astype(BF16)

    o_ref[...] = _dot(h_scr[...], w_ref[...])


def _in_proj(x, gain, w, layer):
    return pl.pallas_call(
        _proj_body,
        grid=(ROWS // TM_BIG, PROJ_COLS // TN_PROJ),
        in_specs=[
            pl.BlockSpec((TM_BIG, D_MODEL), lambda i, n: (i, 0), pipeline_mode=pl.Buffered(1)),
            pl.BlockSpec((None, 1, D_MODEL), lambda i, n: (layer, 0, 0)),
            pl.BlockSpec((None, D_MODEL, TN_PROJ), lambda i, n: (layer, 0, n)),
        ],
        out_specs=pl.BlockSpec((TM_BIG, TN_PROJ), lambda i, n: (i, n)),
        out_shape=jax.ShapeDtypeStruct((ROWS, PROJ_COLS), F32),
        scratch_shapes=[pltpu.VMEM((TM_BIG, D_MODEL), BF16)],
        compiler_params=_params("parallel", "arbitrary"),
        name="in_proj",
    )(x, gain, w)


def _glu_body(g_ref, w_ref, o_ref):
    g = g_ref[...]
    o_ref[...] = (g * _sigmoid(_dot(g.astype(BF16), w_ref[...]))).astype(BF16)


def _glu(g, w, layer):
    return pl.pallas_call(
        _glu_body,
        grid=(ROWS // TM,),
        in_specs=[
            pl.BlockSpec((TM, WIDTH), lambda i: (i, 0)),
            pl.BlockSpec((None, WIDTH, WIDTH), lambda i: (layer, 0, 0)),
        ],
        out_specs=pl.BlockSpec((TM, WIDTH), lambda i: (i, 0)),
        out_shape=jax.ShapeDtypeStruct((ROWS, WIDTH), BF16),
        compiler_params=_params("parallel"),
        name="glu",
    )(g, w)


def _merge_body(x_ref, oa_ref, ob_ref, oc_ref, ga_ref, gb_ref, gc_ref, wa_ref, wb_ref, wc_ref, wo_ref, o_ref):
    @pl.when(pl.program_id(1) == 0)
    def _():
        o_ref[...] = x_ref[...]

    m = (_sigmoid(ga_ref[...]) * _dot(oa_ref[...], wa_ref[...])
         + _sigmoid(gb_ref[...]) * _dot(ob_ref[...], wb_ref[...])
         + _sigmoid(gc_ref[...]) * _dot(oc_ref[...], wc_ref[...]))
    o_ref[...] += _dot(m.astype(BF16), wo_ref[...])


def _merge(x, oa, ob, oc, proj, w_branch, w_out, layer):
    gate_blk = C_GATE // TN_MERGE
    per_gate = D_MODEL // TN_MERGE
    o_spec = pl.BlockSpec((TM, WIDTH), lambda i, n: (i, 0))

    def gate_spec(which):
        return pl.BlockSpec((TM, TN_MERGE), lambda i, n: (i, gate_blk + which * per_gate + n))

    def w_spec(which):
        return pl.BlockSpec((None, None, WIDTH, TN_MERGE), lambda i, n: (layer, which, 0, n))

    return pl.pallas_call(
        _merge_body,
        grid=(ROWS // TM, D_MODEL // TN_MERGE),
        in_specs=[
            pl.BlockSpec((TM, D_MODEL), lambda i, n: (i, 0)),
            o_spec, o_spec, o_spec,
            gate_spec(0), gate_spec(1), gate_spec(2),
            w_spec(0), w_spec(1), w_spec(2),
            pl.BlockSpec((None, TN_MERGE, D_MODEL), lambda i, n: (layer, n, 0)),
        ],
        out_specs=pl.BlockSpec((TM, D_MODEL), lambda i, n: (i, 0)),
        out_shape=jax.ShapeDtypeStruct((ROWS, D_MODEL), F32),
        compiler_params=_params("parallel", "arbitrary"),
        name="merge",
    )(x, oa, ob, oc, proj, proj, proj, w_branch, w_branch, w_branch, w_out)


def _final_norm_body(x_ref, g_ref, o_ref):
    o_ref[...] = _rms(x_ref[...], g_ref[...])


def _final_norm(x, gain):
    return pl.pallas_call(
        _final_norm_body,
        grid=(ROWS // TM,),
        in_specs=[pl.BlockSpec((TM, D_MODEL), lambda i: (i, 0)),
                  pl.BlockSpec((1, D_MODEL), lambda i: (0, 0))],
        out_specs=pl.BlockSpec((TM, D_MODEL), lambda i: (i, 0)),
        out_shape=jax.ShapeDtypeStruct((ROWS, D_MODEL), F32),
        compiler_params=_params("parallel"),
        name="final_norm",
    )(x, gain)


SB_BQ = 256
SB_BK = 128
SB_HB = 4
SB_SCALE = HEAD_DIM ** -0.5


def _stick_terms(z):
    t = jnp.log(1.0 + jnp.exp(-jnp.abs(z)))
    return jnp.minimum(z, 0.0) - t, jnp.minimum(-z, 0.0) - t


LOG2E = 1.0 / math.log(2.0)


def _sb_prompt_body(bias_ref, q_ref, k_ref, v_ref, o_ref, q_scr, acc_scr, run_scr, *, layer):
    hg = pl.program_id(1)
    qi = pl.program_id(2)
    heads = range(SB_HB)
    lanes = [slice(hh * HEAD_DIM, (hh + 1) * HEAD_DIM) for hh in heads]
    row = lax.broadcasted_iota(jnp.int32, (SB_BQ, SB_BK), 0)
    col = lax.broadcasted_iota(jnp.int32, (SB_BQ, SB_BK), 1)
    q_pos = qi * SB_BQ + row
    r2 = lax.broadcasted_iota(jnp.int32, (SB_BK, 2 * SB_BK), 0)
    c2 = lax.broadcasted_iota(jnp.int32, (SB_BK, 2 * SB_BK), 1)
    suffix_ones = jnp.where((r2 > c2) | (c2 >= SB_BK), 1.0, 0.0).astype(BF16)
    bias2 = [bias_ref[layer * HEADS + hg * SB_HB + hh] * LOG2E for hh in heads]

    for hh in heads:
        q_scr[hh] = (q_ref[:, lanes[hh]] * (SB_SCALE * LOG2E)).astype(BF16)
    acc_scr[...] = jnp.zeros_like(acc_scr)
    run_scr[...] = jnp.zeros_like(run_scr)
    n_blocks = (qi + 1) * (SB_BQ // SB_BK)

    def visit(j, masked):
        k_rows = pl.ds(pl.multiple_of(j * SB_BK, SB_BK), SB_BK)
        mask = (j * SB_BK + col) < q_pos
        z = [_dot_nt(q_scr[hh], k_ref[k_rows, lanes[hh]].astype(BF16)) + bias2[hh] for hh in heads]
        t = [jnp.log(1.0 + jnp.exp2(-jnp.abs(z[hh]))) * LOG2E for hh in heads]
        log_hit = [jnp.minimum(z[hh], 0.0) - t[hh] for hh in heads]
        log_fail = [jnp.minimum(-z[hh], 0.0) - t[hh] for hh in heads]
        if masked:
            log_fail = [jnp.where(mask, log_fail[hh], 0.0) for hh in heads]
        hi = [log_fail[hh].astype(BF16) for hh in heads]
        lo = [(log_fail[hh] - hi[hh].astype(F32)).astype(BF16) for hh in heads]
        sums = [_dot(hi[hh], suffix_ones) + _dot(lo[hh], suffix_ones) for hh in heads]
        w = []
        for hh in heads:
            run = run_scr[hh]
            w_h = jnp.exp2(log_hit[hh] + sums[hh][:, :SB_BK] + run)
            run_scr[hh] = run + sums[hh][:, SB_BK:]
            w.append(jnp.where(mask, w_h, 0.0) if masked else w_h)
        for hh in heads:
            acc_scr[hh] += _dot(w[hh].astype(BF16), v_ref[k_rows, lanes[hh]].astype(BF16))

    visit(n_blocks - 1, True)
    visit(n_blocks - 2, True)

    def below_diagonal(it, carry):
        visit(n_blocks - 3 - it, False)
        return carry

    lax.fori_loop(0, n_blocks - 2, below_diagonal, 0)
    for hh in heads:
        o_ref[:, lanes[hh]] = acc_scr[hh].astype(BF16)


def _sb_prompt(proj, bias_flat, layer):
    hw = SB_HB * HEAD_DIM
    n_hg = HEADS // SB_HB
    nq = SEQ // SB_BQ
    grid_spec = pltpu.PrefetchScalarGridSpec(
        num_scalar_prefetch=1,
        grid=(BATCH, n_hg, nq),
        in_specs=[
            pl.BlockSpec((SB_BQ, hw), lambda b, h, i, bias: (b * nq + i, C_Q // hw + h)),
            pl.BlockSpec((SEQ, hw), lambda b, h, i, bias: (b, C_K // hw + h)),
            pl.BlockSpec((SEQ, hw), lambda b, h, i, bias: (b, C_V // hw + h)),
        ],
        out_specs=pl.BlockSpec((SB_BQ, hw), lambda b, h, i, bias: (b * nq + i, h)),
        scratch_shapes=[pltpu.VMEM((SB_HB, SB_BQ, HEAD_DIM), BF16),
                        pltpu.VMEM((SB_HB, SB_BQ, HEAD_DIM), F32),
                        pltpu.VMEM((SB_HB, SB_BQ, SB_BK), F32)],
    )
    return pl.pallas_call(
        functools.partial(_sb_prompt_body, layer=layer),
        grid_spec=grid_spec,
        out_shape=jax.ShapeDtypeStruct((N_PROMPT, WIDTH), BF16),
        compiler_params=_params("parallel", "parallel", "arbitrary"),
        name="sb_prompt",
    )(bias_flat, proj, proj, proj)


DEC_PAGES = 8
DEC_STEPS = N_PAGES // DEC_PAGES
DEC_LANES = PAGE * HEADS


def _lane_shift_up(x, s):
    n = x.shape[1]
    lane = lax.broadcasted_iota(jnp.int32, x.shape, 1)
    return jnp.where(lane < n - s, pltpu.roll(x, n - s, 1), 0.0)


def _sb_decode_body(pt_ref, q_ref, bias_ref, *refs):
    k_refs = refs[:DEC_PAGES]
    v_refs = refs[DEC_PAGES:2 * DEC_PAGES]
    o_ref, acc_scr, run_scr = refs[2 * DEC_PAGES:]
    g = pl.program_id(1)

    @pl.when(g == 0)
    def _():
        acc_scr[...] = jnp.zeros_like(acc_scr)
        run_scr[...] = jnp.zeros_like(run_scr)

    sub = lax.broadcasted_iota(jnp.int32, (HEADS, DEC_LANES), 0)
    lane = lax.broadcasted_iota(jnp.int32, (HEADS, DEC_LANES), 1)
    own_head = (lane & (HEADS - 1)) == sub
    q = q_ref[...].astype(BF16)
    rows = []
    for i in range(DEC_PAGES):
        k2 = k_refs[i][...].reshape(DEC_LANES, HEAD_DIM).astype(BF16)
        zt = _dot_nt(q, k2)
        rows.append(jnp.sum(jnp.where(own_head, zt, 0.0), axis=0, keepdims=True))
    z = jnp.concatenate(rows, axis=0) * SB_SCALE + bias_ref[...]
    log_hit, log_fail = _stick_terms(z)

    suffix = log_fail
    for s in (HEADS, 4 * HEADS):
        suffix = (suffix + _lane_shift_up(suffix, s)) + (_lane_shift_up(suffix, 2 * s) + _lane_shift_up(suffix, 3 * s))
    s = 16 * HEADS
    while s < DEC_LANES:
        suffix = suffix + jnp.concatenate([suffix[:, s:], jnp.zeros((DEC_PAGES, s), F32)], axis=1)
        s *= 2
    tile_lane = lax.broadcasted_iota(jnp.int32, (DEC_PAGES, 16 * HEADS), 1)
    tot = jnp.where(tile_lane < HEADS, suffix[:, :16 * HEADS], 0.0)
    for s in (HEADS, 4 * HEADS):
        tot = (tot + pltpu.roll(tot, s, 1)) + (pltpu.roll(tot, 2 * s, 1) + pltpu.roll(tot, 3 * s, 1))
    tot = jnp.concatenate([tot] * (DEC_LANES // (16 * HEADS)), axis=1)
    page_r = lax.broadcasted_iota(jnp.int32, (DEC_PAGES, DEC_PAGES), 0)
    page_c = lax.broadcasted_iota(jnp.int32, (DEC_PAGES, DEC_PAGES), 1)
    later = jnp.where(page_c > page_r, 1.0, 0.0)
    after = _dot(later, tot, precision=HIGHEST)
    run = run_scr[...]
    log_between = (suffix - log_fail) + after + run
    run_scr[...] = run + jnp.sum(tot, axis=0, keepdims=True)
    w = jnp.exp(log_hit + log_between)

    acc = acc_scr[...]
    for i in range(DEC_PAGES):
        wm = jnp.where(own_head, jnp.broadcast_to(w[i:i + 1, :], (HEADS, DEC_LANES)), 0.0).astype(BF16)
        v2 = v_refs[i][...].reshape(DEC_LANES, HEAD_DIM).astype(BF16)
        acc = acc + _dot(wm, v2)
    acc_scr[...] = acc

    @pl.when(g == DEC_STEPS - 1)
    def _():
        o_ref[...] = acc


def _sb_decode(q, cache_k, cache_v, page_table_flat, bias_lanes, layer):
    def page_spec(i):
        def index(b, g, pt):
            return (layer, pt[b * N_PAGES + (DEC_STEPS - 1 - g) * DEC_PAGES + i], 0, 0, 0)
        return pl.BlockSpec((None, None, PAGE, HEADS, HEAD_DIM), index)

    grid_spec = pltpu.PrefetchScalarGridSpec(
        num_scalar_prefetch=1,
        grid=(DEC_BATCH, DEC_STEPS),
        in_specs=[pl.BlockSpec((None, HEADS, HEAD_DIM), lambda b, g, pt: (b, 0, 0)),
                  pl.BlockSpec((None, 1, DEC_LANES), lambda b, g, pt: (layer, 0, 0))]
        + [page_spec(i) for i in range(DEC_PAGES)] * 2,
        out_specs=pl.BlockSpec((None, HEADS, HEAD_DIM), lambda b, g, pt: (b, 0, 0)),
        scratch_shapes=[pltpu.VMEM((HEADS, HEAD_DIM), F32), pltpu.VMEM((1, DEC_LANES), F32)],
    )
    return pl.pallas_call(
        _sb_decode_body,
        grid_spec=grid_spec,
        out_shape=jax.ShapeDtypeStruct((DEC_BATCH, HEADS, HEAD_DIM), F32),
        compiler_params=_params("parallel", "arbitrary"),
        name="sb_decode",
    )(page_table_flat, q, bias_lanes, *([cache_k] * DEC_PAGES), *([cache_v] * DEC_PAGES))


S5_GB = 8
S5_CH = S5_GB * SSM_GROUP
S5_ST = S5_GB * SSM_STATE
S5_NGB = WIDTH // S5_CH
S5_LANES = 8
S5_SEG = SEQ // S5_LANES


def _s5_discretize(par):
    ar, ai, ldt = par[0:1], par[1:2], par[2:3]
    dt = jnp.exp(ldt)
    mag = jnp.exp(dt * ar)
    abr = mag * jnp.cos(dt * ai)
    abi = mag * jnp.sin(dt * ai)
    den = ar * ar + ai * ai
    cr = ((abr - 1.0) * ar + abi * ai) / den
    ci = (abi * ar - (abr - 1.0) * ai) / den
    return abr, abi, cr, ci


def _gelu(y):
    return 0.5 * y * (1.0 + jnp.tanh(math.sqrt(2.0 / math.pi) * (y + 0.044715 * (y * y * y))))


def _s5_prompt_body(u_ref, par_ref, bre_ref, bim_ref, cre_ref, cim_ref, d_ref, h0r_ref, h0i_ref,
                    o_ref, hr_ref, hi_ref, xr_scr, xi_scr):
    abr, abi, cr, ci = _s5_discretize(par_ref[...])
    bre, bim = bre_ref[...], bim_ref[...]
    bbr = (cr * bre - ci * bim).astype(BF16)
    bbi = (cr * bim + ci * bre).astype(BF16)
    u = u_ref[...]
    ub = u.astype(BF16)
    xr_scr[...] = _dot(ub, bbr)
    xi_scr[...] = _dot(ub, bbi)

    shape = (S5_LANES, S5_ST)
    ar8 = jnp.broadcast_to(abr, shape)
    ai8 = jnp.broadcast_to(abi, shape)

    def local_step(t, carry):
        xr, xi = carry
        rows = pl.ds(pl.multiple_of(t * S5_LANES, S5_LANES), S5_LANES)
        nr = ar8 * xr - ai8 * xi + xr_scr[rows, :]
        ni = ar8 * xi + ai8 * xr + xi_scr[rows, :]
        xr_scr[rows, :] = nr
        xi_scr[rows, :] = ni
        return nr, ni

    zero = jnp.zeros(shape, F32)
    fr, fi = lax.fori_loop(0, S5_SEG, local_step, (zero, zero), unroll=8)

    pr, pi = abr, abi
    for _ in range(int(math.log2(S5_SEG))):
        pr, pi = pr * pr - pi * pi, 2.0 * pr * pi
    hr, hi = h0r_ref[...], h0i_ref[...]
    in_r, in_i = [], []
    for j in range(S5_LANES):
        in_r.append(hr)
        in_i.append(hi)
        hr, hi = fr[j:j + 1] + pr * hr - pi * hi, fi[j:j + 1] + pr * hi + pi * hr
    hr_ref[...] = hr
    hi_ref[...] = hi
    cin_r = jnp.concatenate(in_r, axis=0)
    cin_i = jnp.concatenate(in_i, axis=0)

    def fix_step(t, carry):
        wr, wi = carry
        rows = pl.ds(pl.multiple_of(t * S5_LANES, S5_LANES), S5_LANES)
        xr_scr[rows, :] = xr_scr[rows, :] + wr * cin_r - wi * cin_i
        xi_scr[rows, :] = xi_scr[rows, :] + wr * cin_i + wi * cin_r
        return wr * ar8 - wi * ai8, wr * ai8 + wi * ar8

    lax.fori_loop(0, S5_SEG, fix_step, (ar8, ai8), unroll=8)

    y = (_dot(xr_scr[...].astype(BF16), cre_ref[...].astype(BF16))
         - _dot(xi_scr[...].astype(BF16), cim_ref[...].astype(BF16)) + d_ref[...] * u)
    o_ref[...] = _gelu(y)


def _s5_param_specs(layer):
    lead = (lambda *a: a[-1])

    def spec(shape):
        return pl.BlockSpec((None, None) + shape, lambda *a: (layer, lead(*a), 0, 0))

    return [spec((8, S5_ST)), spec((S5_CH, S5_ST)), spec((S5_CH, S5_ST)),
            spec((S5_ST, S5_CH)), spec((S5_ST, S5_CH)), spec((1, S5_CH))]


def _s5_prompt(u_perm, s5p, layer):
    zeros = jnp.zeros((BATCH, S5_NGB, 1, S5_ST), F32)
    state_spec = pl.BlockSpec((None, None, 1, S5_ST), lambda b, g: (b, g, 0, 0))
    out, hr, hi = pl.pallas_call(
        _s5_prompt_body,
        grid=(BATCH, S5_NGB),
        in_specs=[pl.BlockSpec((SEQ, S5_CH), lambda b, g: (b, g))] + _s5_param_specs(layer)
        + [state_spec, state_spec],
        out_specs=[pl.BlockSpec((SEQ, S5_CH), lambda b, g: (b, g)), state_spec, state_spec],
        out_shape=[jax.ShapeDtypeStruct((N_PROMPT, WIDTH), F32),
                   jax.ShapeDtypeStruct((BATCH, S5_NGB, 1, S5_ST), F32),
                   jax.ShapeDtypeStruct((BATCH, S5_NGB, 1, S5_ST), F32)],
        scratch_shapes=[pltpu.VMEM((SEQ, S5_ST), F32), pltpu.VMEM((SEQ, S5_ST), F32)],
        compiler_params=_params("parallel", "parallel"),
        name="s5_prompt",
    )(u_perm, *s5p, zeros, zeros)
    return out, hr, hi


def _s5_sample_body(u_ref, par_ref, bre_ref, bim_ref, cre_ref, cim_ref, d_ref, h0r_ref, h0i_ref,
                    o_ref, hr_ref, hi_ref):
    abr, abi, cr, ci = _s5_discretize(par_ref[...])
    bre, bim = bre_ref[...], bim_ref[...]
    bbr = (cr * bre - ci * bim).astype(BF16)
    bbi = (cr * bim + ci * bre).astype(BF16)
    u = u_ref[...]
    ub = u.astype(BF16)
    h0r, h0i = h0r_ref[...], h0i_ref[...]
    xr = _dot(ub, bbr) + abr * h0r - abi * h0i
    xi = _dot(ub, bbi) + abr * h0i + abi * h0r
    hr_ref[...] = xr
    hi_ref[...] = xi
    y = (_dot(xr.astype(BF16), cre_ref[...].astype(BF16))
         - _dot(xi.astype(BF16), cim_ref[...].astype(BF16)) + d_ref[...] * u)
    o_ref[...] = _gelu(y)


def _s5_sample(proj, s5p, h0r, h0i, layer):
    state_spec = pl.BlockSpec((DEC_BATCH, S5_ST), lambda g: (0, g))
    return pl.pallas_call(
        _s5_sample_body,
        grid=(S5_NGB,),
        in_specs=[pl.BlockSpec((DEC_BATCH, S5_CH), lambda g: (SAMPLE_ROW_BLOCK, C_U // S5_CH + g))]
        + _s5_param_specs(layer) + [state_spec, state_spec],
        out_specs=[pl.BlockSpec((DEC_BATCH, S5_CH), lambda g: (0, g)), state_spec, state_spec],
        out_shape=[jax.ShapeDtypeStruct((DEC_BATCH, WIDTH), F32),
                   jax.ShapeDtypeStruct((DEC_BATCH, S5_NGB * S5_ST), F32),
                   jax.ShapeDtypeStruct((DEC_BATCH, S5_NGB * S5_ST), F32)],
        compiler_params=_params("parallel"),
        name="s5_sample",
    )(proj, *s5p, h0r, h0i)


GDN_TS = 512
GDN_SCALE = HEAD_DIM ** -0.5


def _l2norm_heads(xc, o_ref):
    for h in range(HEADS):
        cs = slice(h * HEAD_DIM, (h + 1) * HEAD_DIM)
        seg = xc[:, cs]
        o_ref[:, cs] = seg * lax.rsqrt(jnp.sum(seg * seg, axis=-1, keepdims=True) + EPS)


def _conv_prompt_body(x_ref, prev_ref, w_ref, o_ref, ext_scr):
    i = pl.program_id(1)
    part = pl.program_id(2)
    x = x_ref[...]
    ext_scr[0:8, :] = jnp.where(i == 0, 0.0, prev_ref[...])
    ext_scr[8:8 + GDN_TS, :] = x
    w = w_ref[...]
    acc = x * w[3:4]
    for tap in range(CONV_TAPS - 1):
        back = CONV_TAPS - 1 - tap
        acc = acc + ext_scr[8 - back:8 - back + GDN_TS, :] * w[tap:tap + 1]
    xc = acc * _sigmoid(acc)

    @pl.when(part < 2)
    def _():
        _l2norm_heads(xc, o_ref)

    @pl.when(part == 2)
    def _():
        o_ref[...] = xc


def _conv_prompt(proj, conv_w, layer):
    n_t = SEQ // GDN_TS
    col0 = C_QKV // WIDTH

    def prev_index(b, i, p):
        return (jnp.maximum(b * (SEQ // 8) + i * (GDN_TS // 8) - 1, 0), col0 + p)

    return pl.pallas_call(
        _conv_prompt_body,
        grid=(BATCH, n_t, 3),
        in_specs=[
            pl.BlockSpec((GDN_TS, WIDTH), lambda b, i, p: (b * n_t + i, col0 + p)),
            pl.BlockSpec((8, WIDTH), prev_index),
            pl.BlockSpec((None, 8, WIDTH), lambda b, i, p: (layer, 0, p)),
        ],
        out_specs=pl.BlockSpec((GDN_TS, WIDTH), lambda b, i, p: (b * n_t + i, p)),
        out_shape=jax.ShapeDtypeStruct((N_PROMPT, CONV_WIDTH), F32),
        scratch_shapes=[pltpu.VMEM((GDN_TS + 8, WIDTH), F32)],
        compiler_params=_params("parallel", "parallel", "parallel"),
        name="gdn_conv_prompt",
    )(proj, proj, conv_w)


def _conv_sample_body(x_ref, hist_ref, w_ref, o_ref):
    part = pl.program_id(0)
    w = w_ref[...]
    acc = x_ref[...] * w[3:4]
    for tap in range(CONV_TAPS - 1):
        acc = acc + hist_ref[tap] * w[tap:tap + 1]
    xc = acc * _sigmoid(acc)

    @pl.when(part < 2)
    def _():
        _l2norm_heads(xc, o_ref)

    @pl.when(part == 2)
    def _():
        o_ref[...] = xc


def _conv_sample(proj, hist, conv_w, layer):
    col0 = C_QKV // WIDTH
    return pl.pallas_call(
        _conv_sample_body,
        grid=(3,),
        in_specs=[
            pl.BlockSpec((DEC_BATCH, WIDTH), lambda p: (SAMPLE_ROW_BLOCK, col0 + p)),
            pl.BlockSpec((None, CONV_TAPS - 1, DEC_BATCH, WIDTH), lambda p: (layer, 0, 0, p)),
            pl.BlockSpec((None, 8, WIDTH), lambda p: (layer, 0, p)),
        ],
        out_specs=pl.BlockSpec((DEC_BATCH, WIDTH), lambda p: (0, p)),
        out_shape=jax.ShapeDtypeStruct((DEC_BATCH, CONV_WIDTH), F32),
        compiler_params=_params("parallel"),
        name="gdn_conv_sample",
    )(proj, hist, conv_w)


GDN_CHUNK = 128
GDN_ROW_SLOTS = 16


def _gdn_gates_body(ba_ref, par_ref, beta_ref, gc_ref, gcrow_ref, *, rows, chunk):
    ba = ba_ref[...]
    par = par_ref[...]
    beta = _sigmoid(ba)
    g = -jnp.exp(par[0:1]) * _softplus(ba + par[1:2])
    if chunk > 1:
        r = lax.broadcasted_iota(jnp.int32, (rows, rows), 0)
        c = lax.broadcasted_iota(jnp.int32, (rows, rows), 1)
        tri = jnp.where((r >= c) & ((r & -chunk) == (c & -chunk)), 1.0, 0.0)
        g = _dot(tri, g, precision=HIGHEST)
        g_t = g.T
        for cc in range(rows // chunk):
            gcrow_ref[cc * GDN_ROW_SLOTS:(cc + 1) * GDN_ROW_SLOTS, :] = (
                g_t[0:GDN_ROW_SLOTS, cc * chunk:(cc + 1) * chunk])
    for h in range(HEADS):
        cs = slice(h * HEAD_DIM, (h + 1) * HEAD_DIM)
        beta_ref[:, cs] = jnp.broadcast_to(beta[:, h:h + 1], (rows, HEAD_DIM))
        gc_ref[:, cs] = jnp.broadcast_to(g[:, HEADS + h:HEADS + h + 1], (rows, HEAD_DIM))


def _gdn_gates(proj, gate_par, layer, rows, n_blocks, first_block, chunk):
    out = jax.ShapeDtypeStruct((rows * n_blocks, WIDTH), F32)
    out_specs = [pl.BlockSpec((rows, WIDTH), lambda i: (i, 0))] * 2
    out_shape = [out, out]
    if chunk > 1:
        slots = rows // chunk * GDN_ROW_SLOTS
        out_specs.append(pl.BlockSpec((slots, chunk), lambda i: (i, 0)))
        out_shape.append(jax.ShapeDtypeStruct((slots * n_blocks, chunk), F32))
        body = functools.partial(_gdn_gates_body, rows=rows, chunk=chunk)
    else:
        body = functools.partial(_gdn_gates_body, gcrow_ref=None, rows=rows, chunk=chunk)
    return pl.pallas_call(
        body,
        grid=(n_blocks,),
        in_specs=[pl.BlockSpec((rows, 128), lambda i: (first_block + i, C_BA // 128)),
                  pl.BlockSpec((None, 8, 128), lambda i: (layer, 0, 0))],
        out_specs=out_specs,
        out_shape=out_shape,
        compiler_params=_params("parallel"),
        name="gdn_gates",
    )(proj, gate_par)


GDN_BASE = 8


def _gdn_chunk_body(q_ref, k_ref, v_ref, beta_ref, gc_ref, gcrow_ref, z_ref, s0_ref, gn_ref, o_ref, sfin_ref,
                    s_scr, mask_scr, u_scr, w_scr, qk_scr, qd_scr, kd_scr, *, n_tiles, tile_rows):
    ti = pl.program_id(1)
    n_chunks = tile_rows // GDN_CHUNK
    lanes = [slice(h * HEAD_DIM, (h + 1) * HEAD_DIM) for h in range(HEADS)]

    @pl.when(ti == 0)
    def _():
        s_scr[...] = s0_ref[...]

    row = lax.broadcasted_iota(jnp.int32, (GDN_CHUNK, GDN_CHUNK), 0)
    col = lax.broadcasted_iota(jnp.int32, (GDN_CHUNK, GDN_CHUNK), 1)

    def same_block(size):
        return (row & -size) == (col & -size)

    strict = row > col
    eye = jnp.where(row == col, 1.0, 0.0)
    mask_scr[0] = jnp.where(row >= col, 1.0, 0.0)
    mask_scr[1] = jnp.where(strict & same_block(GDN_BASE), 1.0, 0.0)
    merge_sizes = []
    size = GDN_BASE
    while size < GDN_CHUNK:
        mask_scr[2 + len(merge_sizes)] = jnp.where(strict & same_block(2 * size) & ~same_block(size), 1.0, 0.0)
        merge_sizes.append(size)
        size *= 2
    gain = gn_ref[...]

    def prepare(c, carry):
        rows = pl.ds(pl.multiple_of(c * GDN_CHUNK, GDN_CHUNK), GDN_CHUNK)
        gc_by_head = gcrow_ref[pl.ds(pl.multiple_of(c * GDN_ROW_SLOTS + HEADS, HEADS), HEADS), :]
        heads = range(HEADS)
        m = []
        for h in heads:
            cs = lanes[h]
            k = k_ref[rows, cs]
            gc = gc_ref[rows, cs]
            gc_lanes = gc_by_head[h:h + 1, :]
            decay = mask_scr[0] * jnp.exp(jnp.minimum(gc - gc_lanes, 0.0))
            k16 = k.astype(BF16)
            m.append(_dot_nt((k * beta_ref[rows, cs]).astype(BF16), k16) * decay)
            qk_scr[rows, cs] = (_dot_nt((q_ref[rows, cs] * GDN_SCALE).astype(BF16), k16) * decay).astype(BF16)
        base = [m[h] * mask_scr[1] for h in heads]
        b16 = [base[h].astype(BF16) for h in heads]
        power = [_dot(b16[h], b16[h]).astype(BF16) for h in heads]
        inv = [eye - base[h] for h in heads]
        inv = [inv[h] + _dot(inv[h].astype(BF16), power[h]) for h in heads]
        for _ in range(int(math.log2(GDN_BASE)) - 2):
            power = [_dot(power[h], power[h]).astype(BF16) for h in heads]
            inv = [inv[h] + _dot(inv[h].astype(BF16), power[h]) for h in heads]
        for n in range(len(merge_sizes)):
            i16 = [inv[h].astype(BF16) for h in heads]
            half = [_dot(i16[h], (m[h] * mask_scr[2 + n]).astype(BF16)).astype(BF16) for h in heads]
            inv = [inv[h] - _dot(half[h], i16[h]) for h in heads]
        for h in heads:
            cs = lanes[h]
            k = k_ref[rows, cs]
            beta = beta_ref[rows, cs]
            gc = gc_ref[rows, cs]
            egc = jnp.exp(gc)
            rhs = jnp.concatenate([v_ref[rows, cs] * beta, k * beta * egc], axis=-1)
            sol = rhs + _dot((inv[h] - eye).astype(BF16), rhs.astype(BF16))
            u_scr[rows, cs] = sol[:, :HEAD_DIM]
            w_scr[rows, cs] = sol[:, HEAD_DIM:].astype(BF16)
            gc_last = jnp.broadcast_to(gc[GDN_CHUNK - 1:GDN_CHUNK, :], (GDN_CHUNK, HEAD_DIM))
            qd_scr[rows, cs] = (q_ref[rows, cs] * GDN_SCALE * egc).astype(BF16)
            kd_scr[rows, cs] = (k * jnp.exp(gc_last - gc)).astype(BF16)
        return carry

    lax.fori_loop(0, n_chunks, prepare, 0)

    def advance(c, carry):
        rows = pl.ds(pl.multiple_of(c * GDN_CHUNK, GDN_CHUNK), GDN_CHUNK)
        tail = pl.ds(pl.multiple_of(c * GDN_CHUNK + GDN_CHUNK - 8, 8), 8)
        heads = range(HEADS)
        state = [s_scr[h] for h in heads]
        s16 = [state[h].astype(BF16) for h in heads]
        vn16 = [(u_scr[rows, lanes[h]] - _dot(w_scr[rows, lanes[h]], s16[h])).astype(BF16) for h in heads]
        for h in heads:
            s_scr[h] = state[h] * jnp.exp(gc_ref[tail, lanes[h]][7:8, :]) + lax.dot_general(
                kd_scr[rows, lanes[h]], vn16[h], (((0,), (0,)), ((), ())), preferred_element_type=F32)
        for h in heads:
            cs = lanes[h]
            o = _dot(qd_scr[rows, cs], s16[h]) + _dot(qk_scr[rows, cs], vn16[h])
            zg = z_ref[rows, cs]
            o_ref[rows, cs] = (_rms(o, gain) * (zg * _sigmoid(zg))).astype(BF16)
        return carry

    lax.fori_loop(0, n_chunks, advance, 0)

    @pl.when(ti == n_tiles - 1)
    def _():
        sfin_ref[...] = s_scr[...]


def _gdn_chunks(qkv, beta, gc, gcrow, z_src, z_col_block, s0, out_norm, layer, n_seq, seq_rows):
    tile_rows = min(seq_rows, GDN_TS)
    n_tiles = seq_rows // tile_rows
    n_masks = 2 + int(math.log2(GDN_CHUNK // GDN_BASE))

    def act(col_block):
        return pl.BlockSpec((tile_rows, WIDTH), lambda b, t: (b * n_tiles + t, col_block))

    slots = tile_rows // GDN_CHUNK * GDN_ROW_SLOTS
    state_spec = pl.BlockSpec((None, HEADS, HEAD_DIM, HEAD_DIM), lambda b, t: (b, 0, 0, 0))
    return pl.pallas_call(
        functools.partial(_gdn_chunk_body, n_tiles=n_tiles, tile_rows=tile_rows),
        grid=(n_seq, n_tiles),
        in_specs=[act(0), act(1), act(2), act(0), act(0),
                  pl.BlockSpec((slots, GDN_CHUNK), lambda b, t: (b * n_tiles + t, 0)),
                  act(z_col_block), state_spec,
                  pl.BlockSpec((None, 1, HEAD_DIM), lambda b, t: (layer, 0, 0))],
        out_specs=[act(0), state_spec],
        out_shape=[jax.ShapeDtypeStruct((n_seq * seq_rows, WIDTH), BF16),
                   jax.ShapeDtypeStruct((n_seq, HEADS, HEAD_DIM, HEAD_DIM), F32)],
        scratch_shapes=[pltpu.VMEM((HEADS, HEAD_DIM, HEAD_DIM), F32),
                        pltpu.VMEM((n_masks, GDN_CHUNK, GDN_CHUNK), F32),
                        pltpu.VMEM((tile_rows, WIDTH), F32),
                        pltpu.VMEM((tile_rows, WIDTH), BF16),
                        pltpu.VMEM((tile_rows, WIDTH), BF16),
                        pltpu.VMEM((tile_rows, WIDTH), BF16),
                        pltpu.VMEM((tile_rows, WIDTH), BF16)],
        compiler_params=_params("parallel", "arbitrary"),
        name="gdn_chunks",
    )(qkv, qkv, qkv, beta, gc, gcrow, z_src, s0, out_norm)


def _pack_w_in(w_in):
    pad = jnp.zeros((DEPTH, D_MODEL, C_GATE - C_BA - 2 * HEADS), w_in.dtype)
    return jnp.concatenate([w_in[..., :C_BA], w_in[..., C_BA:C_BA + 2 * HEADS], pad,
                            w_in[..., C_BA + 2 * HEADS:]], axis=-1).astype(BF16)


def _s5_params(a_re, a_im, log_dt, b_re, b_im, c_re, c_im, d):
    eye = jnp.eye(S5_GB, dtype=F32)

    def state_rows(x):
        return x.reshape(DEPTH, S5_NGB, 1, S5_ST)

    par = jnp.concatenate(
        [state_rows(a_re), state_rows(a_im),
         state_rows(jnp.repeat(log_dt, SSM_STATE, axis=-1)),
         jnp.zeros((DEPTH, S5_NGB, 5, S5_ST), F32)], axis=2)

    def b_blockdiag(b):
        b = b.reshape(DEPTH, S5_NGB, S5_GB, SSM_STATE, SSM_GROUP)
        return jnp.einsum('dbgpc,gh->dbgchp', b, eye).reshape(DEPTH, S5_NGB, S5_CH, S5_ST)

    def c_blockdiag(c):
        c = c.reshape(DEPTH, S5_NGB, S5_GB, SSM_GROUP, SSM_STATE)
        return jnp.einsum('dbgcp,gh->dbgphc', c, eye).reshape(DEPTH, S5_NGB, S5_ST, S5_CH)

    return (par, b_blockdiag(b_re), b_blockdiag(b_im), c_blockdiag(c_re), c_blockdiag(c_im),
            d.reshape(DEPTH, S5_NGB, 1, S5_CH))


def _segment_major(x):
    w = x.shape[-1]
    return x.reshape(BATCH, S5_LANES, S5_SEG, w).transpose(0, 2, 1, 3).reshape(N_PROMPT, w)


def _token_major(x):
    w = x.shape[-1]
    return x.reshape(BATCH, S5_SEG, S5_LANES, w).transpose(0, 2, 1, 3).reshape(N_PROMPT, w)


def _pad_rows(x):
    return jnp.concatenate([x, jnp.zeros((ROWS - x.shape[0], x.shape[1]), x.dtype)], axis=0)


def _one_row_chunks(x):
    return jnp.pad(x[:, None, :], ((0, 0), (0, GDN_CHUNK - 1), (0, 0))).reshape(DEC_BATCH * GDN_CHUNK, x.shape[-1])


def kernel(x_prompt, x_sample, cache_k, cache_v, page_table, state_ssm, state_conv, state_gdn, ffn1_norm, ffn1_w_gate, ffn1_w_up, ffn1_w_down, mix_norm, w_in, sb_bias, ssm_a_re, ssm_a_im, ssm_log_dt, ssm_b_re, ssm_b_im, ssm_c_re, ssm_c_im, ssm_d, ssm_w_glu, gdn_conv_w, gdn_a_log, gdn_dt_bias, gdn_out_norm, w_branch, w_out, ffn2_norm, ffn2_w_gate, ffn2_w_up, ffn2_w_down, final_norm):
    bf = lambda w: w.astype(BF16)
    gain3 = lambda g: g.reshape(DEPTH, 1, D_MODEL)
    w_in_p = _pack_w_in(w_in)
    ffn1 = (gain3(ffn1_norm), bf(ffn1_w_gate), bf(ffn1_w_up), bf(ffn1_w_down))
    ffn2 = (gain3(ffn2_norm), bf(ffn2_w_gate), bf(ffn2_w_up), bf(ffn2_w_down))
    mix_gain = gain3(mix_norm)
    w_glu, w_br, w_o = bf(ssm_w_glu), bf(w_branch).reshape(DEPTH, 3, WIDTH, D_MODEL), bf(w_out)
    s5p = _s5_params(ssm_a_re, ssm_a_im, ssm_log_dt, ssm_b_re, ssm_b_im, ssm_c_re, ssm_c_im, ssm_d)
    bias_flat = sb_bias.reshape(DEPTH * HEADS)
    bias_lanes = jnp.tile(sb_bias, (1, PAGE)).reshape(DEPTH, 1, DEC_LANES)
    page_flat = page_table.reshape(DEC_BATCH * N_PAGES)
    conv_w8 = jnp.pad(gdn_conv_w, ((0, 0), (0, 8 - CONV_TAPS), (0, 0)))
    conv_hist = state_conv.transpose(0, 2, 1, 3)
    lane_pad = jnp.zeros((DEPTH, 128 - 2 * HEADS), F32)
    gate_par = jnp.stack([jnp.concatenate([jnp.zeros((DEPTH, HEADS), F32), gdn_a_log, lane_pad], axis=1),
                          jnp.concatenate([jnp.zeros((DEPTH, HEADS), F32), gdn_dt_bias, lane_pad], axis=1)]
                         + [jnp.zeros((DEPTH, 128), F32)] * 6, axis=1)
    out_norm = gdn_out_norm.reshape(DEPTH, 1, HEAD_DIM)
    ssm_h0r = state_ssm[..., 0].reshape(DEPTH, DEC_BATCH, WIDTH * SSM_STATE // SSM_GROUP)
    ssm_h0i = state_ssm[..., 1].reshape(DEPTH, DEC_BATCH, WIDTH * SSM_STATE // SSM_GROUP)

    x = _pad_rows(jnp.concatenate([x_prompt.reshape(N_PROMPT, D_MODEL),
                                   x_sample.reshape(DEC_BATCH, D_MODEL)], axis=0))
    sample = slice(N_PROMPT, N_PROMPT + DEC_BATCH)
    kp, vp, ks, vs, sp, ss, cp, cs_, gp, gs = ([] for _ in range(10))
    for l in range(DEPTH):
        x = _ffn(x, *ffn1, l)
        proj = _in_proj(x, mix_gain, w_in_p, l)

        oa_p = _sb_prompt(proj, bias_flat, l)
        q_s = proj[sample, C_Q:C_Q + WIDTH].reshape(DEC_BATCH, HEADS, HEAD_DIM)
        oa_s = _sb_decode(q_s, cache_k, cache_v, page_flat, bias_lanes, l)
        oa = _pad_rows(jnp.concatenate([oa_p, oa_s.reshape(DEC_BATCH, WIDTH).astype(BF16)], axis=0))

        gb_p, hr_p, hi_p = _s5_prompt(_segment_major(proj[:N_PROMPT, C_U:C_U + WIDTH]), s5p, l)
        gb_s, hr_s, hi_s = _s5_sample(proj, s5p, ssm_h0r[l], ssm_h0i[l], l)
        ob = _glu(_pad_rows(jnp.concatenate([_token_major(gb_p), gb_s], axis=0)), w_glu, l)

        qkv_p = _conv_prompt(proj, conv_w8, l)
        beta_p, gc_p, gcrow_p = _gdn_gates(proj, gate_par, l, GDN_TS, N_PROMPT // GDN_TS, 0, GDN_CHUNK)
        oc_p, gdn_p = _gdn_chunks(qkv_p, beta_p, gc_p, gcrow_p, proj, C_Z // WIDTH,
                                  jnp.zeros((BATCH, HEADS, HEAD_DIM, HEAD_DIM), F32), out_norm, l, BATCH, SEQ)
        qkv_s = _conv_sample(proj, conv_hist, conv_w8, l)
        beta_s, g_s = _gdn_gates(proj, gate_par, l, DEC_BATCH, 1, SAMPLE_ROW_BLOCK, 1)
        gc_s = jnp.broadcast_to(g_s[:, None, :], (DEC_BATCH, GDN_CHUNK, WIDTH)).reshape(DEC_BATCH * GDN_CHUNK, WIDTH)
        g_heads = jnp.broadcast_to(g_s[:, ::HEAD_DIM, None], (DEC_BATCH, HEADS, GDN_CHUNK))
        gcrow_s = jnp.concatenate([jnp.zeros_like(g_heads), g_heads], axis=1).reshape(
            DEC_BATCH * GDN_ROW_SLOTS, GDN_CHUNK)
        oc_s, gdn_s = _gdn_chunks(_one_row_chunks(qkv_s), _one_row_chunks(beta_s), gc_s, gcrow_s,
                                  _one_row_chunks(proj[sample, C_Z:C_Z + WIDTH]), 0,
                                  state_gdn[l], out_norm, l, DEC_BATCH, GDN_CHUNK)
        oc = _pad_rows(jnp.concatenate([oc_p, oc_s[::GDN_CHUNK]], axis=0))

        x = _merge(x, oa, ob, oc, proj, w_br, w_o, l)
        x = _ffn(x, *ffn2, l)

        kv = lambda c, rows, n: proj[rows, c:c + WIDTH].reshape(n, -1, HEADS, HEAD_DIM)
        kp.append(kv(C_K, slice(0, N_PROMPT), BATCH))
        vp.append(kv(C_V, slice(0, N_PROMPT), BATCH))
        ks.append(kv(C_K, sample, DEC_BATCH))
        vs.append(kv(C_V, sample, DEC_BATCH))
        state = lambda r, i, n: jnp.stack([r.reshape(n, WIDTH // SSM_GROUP, SSM_STATE),
                                           i.reshape(n, WIDTH // SSM_GROUP, SSM_STATE)], axis=-1)
        sp.append(state(hr_p, hi_p, BATCH))
        ss.append(state(hr_s, hi_s, DEC_BATCH))
        pre_conv = proj[:, C_QKV:C_QKV + CONV_WIDTH]
        cp.append(pre_conv[:N_PROMPT].reshape(BATCH, SEQ, CONV_WIDTH)[:, SEQ - (CONV_TAPS - 1):])
        cs_.append(jnp.concatenate([state_conv[l][:, 1:], pre_conv[sample][:, None, :]], axis=1))
        gp.append(gdn_p)
        gs.append(gdn_s)

    y = _final_norm(x, final_norm.reshape(1, D_MODEL))
    return (y[:N_PROMPT].reshape(BATCH, SEQ, D_MODEL), y[sample].reshape(DEC_BATCH, 1, D_MODEL),
            jnp.stack(kp), jnp.stack(vp), jnp.stack(ks), jnp.stack(vs),
            jnp.stack(sp), jnp.stack(ss), jnp.stack(cp), jnp.stack(cs_),
            jnp.stack(gp), jnp.stack(gs))
```

```python
import functools
import math

import jax
import jax.numpy as jnp
from jax import lax
from jax.experimental import pallas as pl
from jax.experimental.pallas import tpu as pltpu

F32 = jnp.float32
BF16 = jnp.bfloat16
HIGHEST = lax.Precision.HIGHEST

D_MODEL = 2048
BATCH = 4
SEQ = 2048
DEPTH = 2
DEC_BATCH = 32
PAGE = 128
N_PAGES = 64
HEADS = 8
HEAD_DIM = 128
WIDTH = HEADS * HEAD_DIM
SSM_GROUP = 16
SSM_STATE = 64
CONV_TAPS = 4
CONV_WIDTH = 3 * WIDTH
FFN_DIM = 5632
EPS = 1e-6

N_PROMPT = BATCH * SEQ
TM = 528
ROWS = 16 * TM
SAMPLE_ROW_BLOCK = N_PROMPT // DEC_BATCH

C_Q, C_K, C_V, C_U, C_QKV, C_Z, C_BA, C_GATE = 0, 1024, 2048, 3072, 4096, 7168, 8192, 9216
PROJ_COLS = 15360
TN_PROJ = 1024
TF = 512
TM_BIG = 2 * TM
TN_MERGE = 512

VMEM_LIMIT = 56 * 1024 * 1024


def _params(*sem):
    return pltpu.CompilerParams(dimension_semantics=sem, vmem_limit_bytes=VMEM_LIMIT)


def _rms(x, gain):
    ms = jnp.mean(x * x, axis=-1, keepdims=True)
    return x * lax.rsqrt(ms + EPS) * gain


def _softplus(x):
    return jnp.maximum(x, 0.0) + jnp.log(1.0 + jnp.exp(-jnp.abs(x)))


def _sigmoid(x):
    return 1.0 / (1.0 + jnp.exp(-x))


def _dot(a, b, **kw):
    return jnp.dot(a, b, preferred_element_type=F32, **kw)


def _dot_nt(a, b, **kw):
    return lax.dot_general(a, b, (((1,), (1,)), ((), ())), preferred_element_type=F32, **kw)


def _ffn_body(x_ref, g_ref, wg_ref, wu_ref, wd_ref, o_ref, h_scr):
    @pl.when(pl.program_id(1) == 0)
    def _():
        x = x_ref[...]
        h_scr[...] = _rms(x, g_ref[...]).astype(BF16)
        o_ref[...] = x

    h = h_scr[...]
    a = _dot(h, wg_ref[...])
    b = _dot(h, wu_ref[...])
    act = (a * _sigmoid(a)) * b * 0.5
    o_ref[...] += _dot(act.astype(BF16), wd_ref[...])


def _ffn(x, gain, wg, wu, wd, layer):
    return pl.pallas_call(
        _ffn_body,
        grid=(ROWS // TM, FFN_DIM // TF),
        in_specs=[
            pl.BlockSpec((TM, D_MODEL), lambda i, f: (i, 0)),
            pl.BlockSpec((None, 1, D_MODEL), lambda i, f: (layer, 0, 0)),
            pl.BlockSpec((None, D_MODEL, TF), lambda i, f: (layer, 0, f)),
            pl.BlockSpec((None, D_MODEL, TF), lambda i, f: (layer, 0, f)),
            pl.BlockSpec((None, TF, D_MODEL), lambda i, f: (layer, f, 0)),
        ],
        out_specs=pl.BlockSpec((TM, D_MODEL), lambda i, f: (i, 0)),
        out_shape=jax.ShapeDtypeStruct((ROWS, D_MODEL), F32),
        scratch_shapes=[pltpu.VMEM((TM, D_MODEL), BF16)],
        compiler_params=_params("parallel", "arbitrary"),
        name="ffn",
    )(x, gain, wg, wu, wd)


def _proj_body(x_ref, g_ref, w_ref, o_ref, h_scr):
    @pl.when(pl.program_id(1) == 0)
    def _():
        h_scr[...] = _rms(x_ref[...], g_ref[...]).astype(BF16)

    o_ref[...] = _dot(h_scr[...], w_ref[...])


def _in_proj(x, gain, w, layer):
    return pl.pallas_call(
        _proj_body,
        grid=(ROWS // TM_BIG, PROJ_COLS // TN_PROJ),
        in_specs=[
            pl.BlockSpec((TM_BIG, D_MODEL), lambda i, n: (i, 0), pipeline_mode=pl.Buffered(1)),
            pl.BlockSpec((None, 1, D_MODEL), lambda i, n: (layer, 0, 0)),
            pl.BlockSpec((None, D_MODEL, TN_PROJ), lambda i, n: (layer, 0, n)),
        ],
        out_specs=pl.BlockSpec((TM_BIG, TN_PROJ), lambda i, n: (i, n)),
        out_shape=jax.ShapeDtypeStruct((ROWS, PROJ_COLS), F32),
        scratch_shapes=[pltpu.VMEM((TM_BIG, D_MODEL), BF16)],
        compiler_params=_params("parallel", "arbitrary"),
        name="in_proj",
    )(x, gain, w)


def _glu_body(g_ref, w_ref, o_ref):
    g = g_ref[...]
    o_ref[...] = (g * _sigmoid(_dot(g.astype(BF16), w_ref[...]))).astype(BF16)


def _glu(g, w, layer):
    return pl.pallas_call(
        _glu_body,
        grid=(ROWS // TM,),
        in_specs=[
            pl.BlockSpec((TM, WIDTH), lambda i: (i, 0)),
            pl.BlockSpec((None, WIDTH, WIDTH), lambda i: (layer, 0, 0)),
        ],
        out_specs=pl.BlockSpec((TM, WIDTH), lambda i: (i, 0)),
        out_shape=jax.ShapeDtypeStruct((ROWS, WIDTH), BF16),
        compiler_params=_params("parallel"),
        name="glu",
    )(g, w)


def _merge_body(x_ref, oa_ref, ob_ref, oc_ref, ga_ref, gb_ref, gc_ref, wa_ref, wb_ref, wc_ref, wo_ref, o_ref):
    @pl.when(pl.program_id(1) == 0)
    def _():
        o_ref[...] = x_ref[...]

    m = (_sigmoid(ga_ref[...]) * _dot(oa_ref[...], wa_ref[...])
         + _sigmoid(gb_ref[...]) * _dot(ob_ref[...], wb_ref[...])
         + _sigmoid(gc_ref[...]) * _dot(oc_ref[...], wc_ref[...]))
    o_ref[...] += _dot(m.astype(BF16), wo_ref[...])


def _merge(x, oa, ob, oc, proj, w_branch, w_out, layer):
    gate_blk = C_GATE // TN_MERGE
    per_gate = D_MODEL // TN_MERGE
    o_spec = pl.BlockSpec((TM, WIDTH), lambda i, n: (i, 0))

    def gate_spec(which):
        return pl.BlockSpec((TM, TN_MERGE), lambda i, n: (i, gate_blk + which * per_gate + n))

    def w_spec(which):
        return pl.BlockSpec((None, None, WIDTH, TN_MERGE), lambda i, n: (layer, which, 0, n))

    return pl.pallas_call(
        _merge_body,
        grid=(ROWS // TM, D_MODEL // TN_MERGE),
        in_specs=[
            pl.BlockSpec((TM, D_MODEL), lambda i, n: (i, 0)),
            o_spec, o_spec, o_spec,
            gate_spec(0), gate_spec(1), gate_spec(2),
            w_spec(0), w_spec(1), w_spec(2),
            pl.BlockSpec((None, TN_MERGE, D_MODEL), lambda i, n: (layer, n, 0)),
        ],
        out_specs=pl.BlockSpec((TM, D_MODEL), lambda i, n: (i, 0)),
        out_shape=jax.ShapeDtypeStruct((ROWS, D_MODEL), F32),
        compiler_params=_params("parallel", "arbitrary"),
        name="merge",
    )(x, oa, ob, oc, proj, proj, proj, w_branch, w_branch, w_branch, w_out)


def _final_norm_body(x_ref, g_ref, o_ref):
    o_ref[...] = _rms(x_ref[...], g_ref[...])


def _final_norm(x, gain, rows, n_blocks, first_block):
    return pl.pallas_call(
        _final_norm_body,
        grid=(n_blocks,),
        in_specs=[pl.BlockSpec((rows, D_MODEL), lambda i: (first_block + i, 0)),
                  pl.BlockSpec((1, D_MODEL), lambda i: (0, 0))],
        out_specs=pl.BlockSpec((rows, D_MODEL), lambda i: (i, 0)),
        out_shape=jax.ShapeDtypeStruct((rows * n_blocks, D_MODEL), F32),
        compiler_params=_params("parallel"),
        name="final_norm",
    )(x, gain)


SB_BQ = 256
SB_BK = 256
SB_HB = 4
SB_SCALE = HEAD_DIM ** -0.5


def _stick_terms(z):
    t = jnp.log(1.0 + jnp.exp(-jnp.abs(z)))
    return jnp.minimum(z, 0.0) - t, jnp.minimum(-z, 0.0) - t


LOG2E = 1.0 / math.log(2.0)


def _sb_prompt_body(bias_ref, q_ref, k_ref, v_ref, *rest, layer):
    o_ref, k_all_ref, v_all_ref, q_scr, acc_scr, run_scr = rest[2:]
    hg = pl.program_id(1)
    qi = pl.program_id(2)

    @pl.when(qi == 0)
    def _():
        k_all_ref[...] = k_ref[...]
        v_all_ref[...] = v_ref[...]

    heads = range(SB_HB)
    lanes = [slice(hh * HEAD_DIM, (hh + 1) * HEAD_DIM) for hh in heads]
    row = lax.broadcasted_iota(jnp.int32, (SB_BQ, SB_BK), 0)
    col = lax.broadcasted_iota(jnp.int32, (SB_BQ, SB_BK), 1)
    q_pos = qi * SB_BQ + row
    r2 = lax.broadcasted_iota(jnp.int32, (SB_BK, SB_BK), 0)
    c2 = lax.broadcasted_iota(jnp.int32, (SB_BK, SB_BK), 1)
    later_key = jnp.where(r2 > c2, 1.0, 0.0).astype(BF16)
    bias2 = [bias_ref[layer * HEADS + hg * SB_HB + hh] * LOG2E for hh in heads]

    for hh in heads:
        q_scr[hh] = (q_ref[:, lanes[hh]] * (SB_SCALE * LOG2E)).astype(BF16)
    acc_scr[...] = jnp.zeros_like(acc_scr)
    run_scr[...] = jnp.zeros_like(run_scr)

    def visit(j, masked):
        k_rows = pl.ds(pl.multiple_of(j * SB_BK, SB_BK), SB_BK)
        mask = (j * SB_BK + col) < q_pos
        z = [_dot_nt(q_scr[hh], k_ref[k_rows, lanes[hh]].astype(BF16)) + bias2[hh] for hh in heads]
        t = [jnp.log(1.0 + jnp.exp2(-jnp.abs(z[hh]))) * LOG2E for hh in heads]
        log_hit = [jnp.minimum(z[hh], 0.0) - t[hh] for hh in heads]
        log_fail = [jnp.minimum(-z[hh], 0.0) - t[hh] for hh in heads]
        if masked:
            log_fail = [jnp.where(mask, log_fail[hh], 0.0) for hh in heads]
        hi = [log_fail[hh].astype(BF16) for hh in heads]
        lo = [(log_fail[hh] - hi[hh].astype(F32)).astype(BF16) for hh in heads]
        suffix = [_dot(hi[hh], later_key) + _dot(lo[hh], later_key) for hh in heads]
        w = []
        for hh in heads:
            run = run_scr[hh]
            w_h = jnp.exp2(log_hit[hh] + suffix[hh] + run)
            block_total = suffix[hh][:, 0:1] + log_fail[hh][:, 0:1]
            run_scr[hh] = run + jnp.broadcast_to(block_total, (SB_BQ, SB_BK))
            w.append(jnp.where(mask, w_h, 0.0) if masked else w_h)
        for hh in heads:
            acc_scr[hh] += _dot(w[hh].astype(BF16), v_ref[k_rows, lanes[hh]].astype(BF16))

    visit(qi, True)

    def below_diagonal(it, carry):
        visit(qi - 1 - it, False)
        return carry

    lax.fori_loop(0, qi, below_diagonal, 0)
    for hh in heads:
        o_ref[:, lanes[hh]] = acc_scr[hh].astype(BF16)


def _sb_prompt(proj, bias_flat, layer, k_all, v_all):
    hw = SB_HB * HEAD_DIM
    n_hg = HEADS // SB_HB
    nq = SEQ // SB_BQ
    all_spec = pl.BlockSpec((None, SEQ, hw), lambda b, h, i, bias: (layer, b, h))
    all_shape = jax.ShapeDtypeStruct((DEPTH, N_PROMPT, WIDTH), F32)
    grid_spec = pltpu.PrefetchScalarGridSpec(
        num_scalar_prefetch=1,
        grid=(BATCH, n_hg, nq),
        in_specs=[
            pl.BlockSpec((SB_BQ, hw), lambda b, h, i, bias: (b * nq + i, C_Q // hw + h)),
            pl.BlockSpec((SEQ, hw), lambda b, h, i, bias: (b, C_K // hw + h)),
            pl.BlockSpec((SEQ, hw), lambda b, h, i, bias: (b, C_V // hw + h)),
            pl.BlockSpec(memory_space=pl.ANY),
            pl.BlockSpec(memory_space=pl.ANY),
        ],
        out_specs=[pl.BlockSpec((SB_BQ, hw), lambda b, h, i, bias: (b * nq + i, h)), all_spec, all_spec],
        scratch_shapes=[pltpu.VMEM((SB_HB, SB_BQ, HEAD_DIM), BF16),
                        pltpu.VMEM((SB_HB, SB_BQ, HEAD_DIM), F32),
                        pltpu.VMEM((SB_HB, SB_BQ, SB_BK), F32)],
    )
    return pl.pallas_call(
        functools.partial(_sb_prompt_body, layer=layer),
        grid_spec=grid_spec,
        out_shape=[jax.ShapeDtypeStruct((N_PROMPT, WIDTH), BF16), all_shape, all_shape],
        input_output_aliases={4: 1, 5: 2},
        compiler_params=_params("parallel", "parallel", "arbitrary"),
        name="sb_prompt",
    )(bias_flat, proj, proj, proj, k_all, v_all)


DEC_PAGES = 16
DEC_STEPS = N_PAGES // DEC_PAGES
DEC_LANES = PAGE * HEADS


def _lane_shift_up(x, s):
    n = x.shape[1]
    lane = lax.broadcasted_iota(jnp.int32, x.shape, 1)
    return jnp.where(lane < n - s, pltpu.roll(x, n - s, 1), 0.0)


def _sb_decode_body(pt_ref, q_ref, bias_ref, *refs):
    k_refs = refs[:DEC_PAGES]
    v_refs = refs[DEC_PAGES:2 * DEC_PAGES]
    o_ref, acc_scr, run_scr = refs[2 * DEC_PAGES:]
    g = pl.program_id(1)

    @pl.when(g == 0)
    def _():
        acc_scr[...] = jnp.zeros_like(acc_scr)
        run_scr[...] = jnp.zeros_like(run_scr)

    sub = lax.broadcasted_iota(jnp.int32, (HEADS, DEC_LANES), 0)
    lane = lax.broadcasted_iota(jnp.int32, (HEADS, DEC_LANES), 1)
    own_head = (lane & (HEADS - 1)) == sub
    q = q_ref[...].astype(BF16)
    rows = []
    for i in range(DEC_PAGES):
        k2 = k_refs[i][...].reshape(DEC_LANES, HEAD_DIM).astype(BF16)
        zt = _dot_nt(q, k2)
        rows.append(jnp.sum(jnp.where(own_head, zt, 0.0), axis=0, keepdims=True))
    z = jnp.concatenate(rows, axis=0) * SB_SCALE + bias_ref[...]
    log_hit, log_fail = _stick_terms(z)

    suffix = log_fail
    for s in (HEADS, 4 * HEADS):
        suffix = (suffix + _lane_shift_up(suffix, s)) + (_lane_shift_up(suffix, 2 * s) + _lane_shift_up(suffix, 3 * s))
    s = 16 * HEADS
    while s < DEC_LANES:
        suffix = suffix + jnp.concatenate([suffix[:, s:], jnp.zeros((DEC_PAGES, s), F32)], axis=1)
        s *= 2
    tile_lane = lax.broadcasted_iota(jnp.int32, (DEC_PAGES, 16 * HEADS), 1)
    tot = jnp.where(tile_lane < HEADS, suffix[:, :16 * HEADS], 0.0)
    for s in (HEADS, 4 * HEADS):
        tot = (tot + pltpu.roll(tot, s, 1)) + (pltpu.roll(tot, 2 * s, 1) + pltpu.roll(tot, 3 * s, 1))
    tot = jnp.concatenate([tot] * (DEC_LANES // (16 * HEADS)), axis=1)
    page_r = lax.broadcasted_iota(jnp.int32, (DEC_PAGES, DEC_PAGES), 0)
    page_c = lax.broadcasted_iota(jnp.int32, (DEC_PAGES, DEC_PAGES), 1)
    later = jnp.where(page_c > page_r, 1.0, 0.0)
    after = _dot(later, tot, precision=HIGHEST)
    run = run_scr[...]
    log_between = (suffix - log_fail) + after + run
    run_scr[...] = run + jnp.sum(tot, axis=0, keepdims=True)
    w = jnp.exp(log_hit + log_between)

    acc = acc_scr[...]
    for i in range(DEC_PAGES):
        wm = jnp.where(own_head, jnp.broadcast_to(w[i:i + 1, :], (HEADS, DEC_LANES)), 0.0).astype(BF16)
        v2 = v_refs[i][...].reshape(DEC_LANES, HEAD_DIM).astype(BF16)
        acc = acc + _dot(wm, v2)
    acc_scr[...] = acc

    @pl.when(g == DEC_STEPS - 1)
    def _():
        o_ref[...] = acc


def _sb_decode(q, cache_k, cache_v, page_table_flat, bias_lanes, layer):
    def page_spec(i):
        def index(b, g, pt):
            return (layer, pt[b * N_PAGES + (DEC_STEPS - 1 - g) * DEC_PAGES + i], 0, 0, 0)
        return pl.BlockSpec((None, None, PAGE, HEADS, HEAD_DIM), index)

    grid_spec = pltpu.PrefetchScalarGridSpec(
        num_scalar_prefetch=1,
        grid=(DEC_BATCH, DEC_STEPS),
        in_specs=[pl.BlockSpec((None, HEADS, HEAD_DIM), lambda b, g, pt: (b, 0, 0)),
                  pl.BlockSpec((None, 1, DEC_LANES), lambda b, g, pt: (layer, 0, 0))]
        + [page_spec(i) for i in range(DEC_PAGES)] * 2,
        out_specs=pl.BlockSpec((None, HEADS, HEAD_DIM), lambda b, g, pt: (b, 0, 0)),
        scratch_shapes=[pltpu.VMEM((HEADS, HEAD_DIM), F32), pltpu.VMEM((1, DEC_LANES), F32)],
    )
    return pl.pallas_call(
        _sb_decode_body,
        grid_spec=grid_spec,
        out_shape=jax.ShapeDtypeStruct((DEC_BATCH, HEADS, HEAD_DIM), F32),
        compiler_params=_params("parallel", "arbitrary"),
        name="sb_decode",
    )(page_table_flat, q, bias_lanes, *([cache_k] * DEC_PAGES), *([cache_v] * DEC_PAGES))


S5_GB = 8
S5_CH = S5_GB * SSM_GROUP
S5_ST = S5_GB * SSM_STATE
S5_NGB = WIDTH // S5_CH
S5_LANES = 8
S5_SEG = SEQ // S5_LANES


def _s5_discretize(par):
    ar, ai, ldt = par[0:1], par[1:2], par[2:3]
    dt = jnp.exp(ldt)
    mag = jnp.exp(dt * ar)
    abr = mag * jnp.cos(dt * ai)
    abi = mag * jnp.sin(dt * ai)
    den = ar * ar + ai * ai
    cr = ((abr - 1.0) * ar + abi * ai) / den
    ci = (abi * ar - (abr - 1.0) * ai) / den
    return abr, abi, cr, ci


def _gelu(y):
    return 0.5 * y * (1.0 + jnp.tanh(math.sqrt(2.0 / math.pi) * (y + 0.044715 * (y * y * y))))


def _s5_prompt_body(u_ref, par_ref, bre_ref, bim_ref, cre_ref, cim_ref, d_ref, h0r_ref, h0i_ref,
                    o_ref, hr_ref, hi_ref, xr_scr, xi_scr):
    abr, abi, cr, ci = _s5_discretize(par_ref[...])
    bre, bim = bre_ref[...], bim_ref[...]
    bbr = (cr * bre - ci * bim).astype(BF16)
    bbi = (cr * bim + ci * bre).astype(BF16)
    u = u_ref[...]
    ub = u.astype(BF16)
    xr_scr[...] = _dot(ub, bbr)
    xi_scr[...] = _dot(ub, bbi)

    shape = (S5_LANES, S5_ST)
    ar8 = jnp.broadcast_to(abr, shape)
    ai8 = jnp.broadcast_to(abi, shape)

    def local_step(t, carry):
        xr, xi = carry
        rows = pl.ds(pl.multiple_of(t * S5_LANES, S5_LANES), S5_LANES)
        nr = ar8 * xr - ai8 * xi + xr_scr[rows, :]
        ni = ar8 * xi + ai8 * xr + xi_scr[rows, :]
        xr_scr[rows, :] = nr
        xi_scr[rows, :] = ni
        return nr, ni

    zero = jnp.zeros(shape, F32)
    fr, fi = lax.fori_loop(0, S5_SEG, local_step, (zero, zero), unroll=8)

    pr, pi = abr, abi
    for _ in range(int(math.log2(S5_SEG))):
        pr, pi = pr * pr - pi * pi, 2.0 * pr * pi
    hr, hi = h0r_ref[...], h0i_ref[...]
    in_r, in_i = [], []
    for j in range(S5_LANES):
        in_r.append(hr)
        in_i.append(hi)
        hr, hi = fr[j:j + 1] + pr * hr - pi * hi, fi[j:j + 1] + pr * hi + pi * hr
    hr_ref[...] = hr
    hi_ref[...] = hi
    cin_r = jnp.concatenate(in_r, axis=0)
    cin_i = jnp.concatenate(in_i, axis=0)

    def fix_step(t, carry):
        wr, wi = carry
        rows = pl.ds(pl.multiple_of(t * S5_LANES, S5_LANES), S5_LANES)
        xr_scr[rows, :] = xr_scr[rows, :] + wr * cin_r - wi * cin_i
        xi_scr[rows, :] = xi_scr[rows, :] + wr * cin_i + wi * cin_r
        return wr * ar8 - wi * ai8, wr * ai8 + wi * ar8

    lax.fori_loop(0, S5_SEG, fix_step, (ar8, ai8), unroll=8)

    y = (_dot(xr_scr[...].astype(BF16), cre_ref[...].astype(BF16))
         - _dot(xi_scr[...].astype(BF16), cim_ref[...].astype(BF16)) + d_ref[...] * u)
    o_ref[...] = _gelu(y)


def _s5_param_specs(layer):
    lead = (lambda *a: a[-1])

    def spec(shape):
        return pl.BlockSpec((None, None) + shape, lambda *a: (layer, lead(*a), 0, 0))

    return [spec((8, S5_ST)), spec((S5_CH, S5_ST)), spec((S5_CH, S5_ST)),
            spec((S5_ST, S5_CH)), spec((S5_ST, S5_CH)), spec((1, S5_CH))]


def _s5_prompt(u_perm, s5p, layer):
    zeros = jnp.zeros((BATCH, S5_NGB, 1, S5_ST), F32)
    state_spec = pl.BlockSpec((None, None, 1, S5_ST), lambda b, g: (b, g, 0, 0))
    out, hr, hi = pl.pallas_call(
        _s5_prompt_body,
        grid=(BATCH, S5_NGB),
        in_specs=[pl.BlockSpec((SEQ, S5_CH), lambda b, g: (b, g))] + _s5_param_specs(layer)
        + [state_spec, state_spec],
        out_specs=[pl.BlockSpec((SEQ, S5_CH), lambda b, g: (b, g)), state_spec, state_spec],
        out_shape=[jax.ShapeDtypeStruct((N_PROMPT, WIDTH), F32),
                   jax.ShapeDtypeStruct((BATCH, S5_NGB, 1, S5_ST), F32),
                   jax.ShapeDtypeStruct((BATCH, S5_NGB, 1, S5_ST), F32)],
        scratch_shapes=[pltpu.VMEM((SEQ, S5_ST), F32), pltpu.VMEM((SEQ, S5_ST), F32)],
        compiler_params=_params("parallel", "parallel"),
        name="s5_prompt",
    )(u_perm, *s5p, zeros, zeros)
    return out, hr, hi


def _s5_sample_body(u_ref, par_ref, bre_ref, bim_ref, cre_ref, cim_ref, d_ref, h0r_ref, h0i_ref,
                    o_ref, hr_ref, hi_ref):
    abr, abi, cr, ci = _s5_discretize(par_ref[...])
    bre, bim = bre_ref[...], bim_ref[...]
    bbr = (cr * bre - ci * bim).astype(BF16)
    bbi = (cr * bim + ci * bre).astype(BF16)
    u = u_ref[...]
    ub = u.astype(BF16)
    h0r, h0i = h0r_ref[...], h0i_ref[...]
    xr = _dot(ub, bbr) + abr * h0r - abi * h0i
    xi = _dot(ub, bbi) + abr * h0i + abi * h0r
    hr_ref[...] = xr
    hi_ref[...] = xi
    y = (_dot(xr.astype(BF16), cre_ref[...].astype(BF16))
         - _dot(xi.astype(BF16), cim_ref[...].astype(BF16)) + d_ref[...] * u)
    o_ref[...] = _gelu(y)


def _s5_sample(proj, s5p, h0r, h0i, layer):
    state_spec = pl.BlockSpec((DEC_BATCH, S5_ST), lambda g: (0, g))
    return pl.pallas_call(
        _s5_sample_body,
        grid=(S5_NGB,),
        in_specs=[pl.BlockSpec((DEC_BATCH, S5_CH), lambda g: (SAMPLE_ROW_BLOCK, C_U // S5_CH + g))]
        + _s5_param_specs(layer) + [state_spec, state_spec],
        out_specs=[pl.BlockSpec((DEC_BATCH, S5_CH), lambda g: (0, g)), state_spec, state_spec],
        out_shape=[jax.ShapeDtypeStruct((DEC_BATCH, WIDTH), F32),
                   jax.ShapeDtypeStruct((DEC_BATCH, S5_NGB * S5_ST), F32),
                   jax.ShapeDtypeStruct((DEC_BATCH, S5_NGB * S5_ST), F32)],
        compiler_params=_params("parallel"),
        name="s5_sample",
    )(proj, *s5p, h0r, h0i)


GDN_TS = 512
GDN_SCALE = HEAD_DIM ** -0.5


def _l2norm_heads(xc, o_ref):
    for h in range(HEADS):
        cs = slice(h * HEAD_DIM, (h + 1) * HEAD_DIM)
        seg = xc[:, cs]
        o_ref[:, cs] = seg * lax.rsqrt(jnp.sum(seg * seg, axis=-1, keepdims=True) + EPS)


def _conv_prompt_body(x_ref, prev_ref, w_ref, o_ref, ext_scr):
    i = pl.program_id(1)
    part = pl.program_id(2)
    x = x_ref[...]
    ext_scr[0:8, :] = jnp.where(i == 0, 0.0, prev_ref[...])
    ext_scr[8:8 + GDN_TS, :] = x
    w = w_ref[...]
    acc = x * w[3:4]
    for tap in range(CONV_TAPS - 1):
        back = CONV_TAPS - 1 - tap
        acc = acc + ext_scr[8 - back:8 - back + GDN_TS, :] * w[tap:tap + 1]
    xc = acc * _sigmoid(acc)

    @pl.when(part < 2)
    def _():
        _l2norm_heads(xc, o_ref)

    @pl.when(part == 2)
    def _():
        o_ref[...] = xc


def _conv_prompt(proj, conv_w, layer):
    n_t = SEQ // GDN_TS
    col0 = C_QKV // WIDTH

    def prev_index(b, i, p):
        return (jnp.maximum(b * (SEQ // 8) + i * (GDN_TS // 8) - 1, 0), col0 + p)

    return pl.pallas_call(
        _conv_prompt_body,
        grid=(BATCH, n_t, 3),
        in_specs=[
            pl.BlockSpec((GDN_TS, WIDTH), lambda b, i, p: (b * n_t + i, col0 + p)),
            pl.BlockSpec((8, WIDTH), prev_index),
            pl.BlockSpec((None, 8, WIDTH), lambda b, i, p: (layer, 0, p)),
        ],
        out_specs=pl.BlockSpec((GDN_TS, WIDTH), lambda b, i, p: (b * n_t + i, p)),
        out_shape=jax.ShapeDtypeStruct((N_PROMPT, CONV_WIDTH), F32),
        scratch_shapes=[pltpu.VMEM((GDN_TS + 8, WIDTH), F32)],
        compiler_params=_params("parallel", "parallel", "parallel"),
        name="gdn_conv_prompt",
    )(proj, proj, conv_w)


def _conv_sample_body(x_ref, hist_ref, w_ref, o_ref):
    part = pl.program_id(0)
    w = w_ref[...]
    acc = x_ref[...] * w[3:4]
    for tap in range(CONV_TAPS - 1):
        acc = acc + hist_ref[tap] * w[tap:tap + 1]
    xc = acc * _sigmoid(acc)

    @pl.when(part < 2)
    def _():
        _l2norm_heads(xc, o_ref)

    @pl.when(part == 2)
    def _():
        o_ref[...] = xc


def _conv_sample(proj, hist, conv_w, layer):
    col0 = C_QKV // WIDTH
    return pl.pallas_call(
        _conv_sample_body,
        grid=(3,),
        in_specs=[
            pl.BlockSpec((DEC_BATCH, WIDTH), lambda p: (SAMPLE_ROW_BLOCK, col0 + p)),
            pl.BlockSpec((None, CONV_TAPS - 1, DEC_BATCH, WIDTH), lambda p: (layer, 0, 0, p)),
            pl.BlockSpec((None, 8, WIDTH), lambda p: (layer, 0, p)),
        ],
        out_specs=pl.BlockSpec((DEC_BATCH, WIDTH), lambda p: (0, p)),
        out_shape=jax.ShapeDtypeStruct((DEC_BATCH, CONV_WIDTH), F32),
        compiler_params=_params("parallel"),
        name="gdn_conv_sample",
    )(proj, hist, conv_w)


GDN_CHUNK = 128
GDN_ROW_SLOTS = 16


def _gdn_gates_body(ba_ref, par_ref, beta_ref, gc_ref, gcrow_ref, *, rows, chunk):
    ba = ba_ref[...]
    par = par_ref[...]
    beta = _sigmoid(ba)
    g = -jnp.exp(par[0:1]) * _softplus(ba + par[1:2])
    if chunk > 1:
        r = lax.broadcasted_iota(jnp.int32, (rows, rows), 0)
        c = lax.broadcasted_iota(jnp.int32, (rows, rows), 1)
        tri = jnp.where((r >= c) & ((r & -chunk) == (c & -chunk)), 1.0, 0.0)
        g = _dot(tri, g, precision=HIGHEST)
        g_t = g.T
        for cc in range(rows // chunk):
            gcrow_ref[cc * GDN_ROW_SLOTS:(cc + 1) * GDN_ROW_SLOTS, :] = (
                g_t[0:GDN_ROW_SLOTS, cc * chunk:(cc + 1) * chunk])
    for h in range(HEADS):
        cs = slice(h * HEAD_DIM, (h + 1) * HEAD_DIM)
        beta_ref[:, cs] = jnp.broadcast_to(beta[:, h:h + 1], (rows, HEAD_DIM))
        gc_ref[:, cs] = jnp.broadcast_to(g[:, HEADS + h:HEADS + h + 1], (rows, HEAD_DIM))


def _gdn_gates(proj, gate_par, layer, rows, n_blocks, first_block, chunk):
    out = jax.ShapeDtypeStruct((rows * n_blocks, WIDTH), F32)
    out_specs = [pl.BlockSpec((rows, WIDTH), lambda i: (i, 0))] * 2
    out_shape = [out, out]
    if chunk > 1:
        slots = rows // chunk * GDN_ROW_SLOTS
        out_specs.append(pl.BlockSpec((slots, chunk), lambda i: (i, 0)))
        out_shape.append(jax.ShapeDtypeStruct((slots * n_blocks, chunk), F32))
        body = functools.partial(_gdn_gates_body, rows=rows, chunk=chunk)
    else:
        body = functools.partial(_gdn_gates_body, gcrow_ref=None, rows=rows, chunk=chunk)
    return pl.pallas_call(
        body,
        grid=(n_blocks,),
        in_specs=[pl.BlockSpec((rows, 128), lambda i: (first_block + i, C_BA // 128)),
                  pl.BlockSpec((None, 8, 128), lambda i: (layer, 0, 0))],
        out_specs=out_specs,
        out_shape=out_shape,
        compiler_params=_params("parallel"),
        name="gdn_gates",
    )(proj, gate_par)


GDN_BASE = 8


def _gdn_chunk_body(q_ref, k_ref, v_ref, beta_ref, gc_ref, gcrow_ref, z_ref, s0_ref, gn_ref, o_ref, sfin_ref,
                    s_scr, mask_scr, u_scr, w_scr, qk_scr, qd_scr, kd_scr, *, n_tiles, tile_rows):
    ti = pl.program_id(1)
    n_chunks = tile_rows // GDN_CHUNK
    lanes = [slice(h * HEAD_DIM, (h + 1) * HEAD_DIM) for h in range(HEADS)]

    @pl.when(ti == 0)
    def _():
        s_scr[...] = s0_ref[...]

    row = lax.broadcasted_iota(jnp.int32, (GDN_CHUNK, GDN_CHUNK), 0)
    col = lax.broadcasted_iota(jnp.int32, (GDN_CHUNK, GDN_CHUNK), 1)

    def same_block(size):
        return (row & -size) == (col & -size)

    strict = row > col
    eye = jnp.where(row == col, 1.0, 0.0)
    mask_scr[0] = jnp.where(row >= col, 1.0, 0.0)
    mask_scr[1] = jnp.where(strict & same_block(GDN_BASE), 1.0, 0.0)
    merge_sizes = []
    size = GDN_BASE
    while size < GDN_CHUNK:
        mask_scr[2 + len(merge_sizes)] = jnp.where(strict & same_block(2 * size) & ~same_block(size), 1.0, 0.0)
        merge_sizes.append(size)
        size *= 2
    gain = gn_ref[...]

    def prepare(c, carry):
        rows = pl.ds(pl.multiple_of(c * GDN_CHUNK, GDN_CHUNK), GDN_CHUNK)
        gc_by_head = gcrow_ref[pl.ds(pl.multiple_of(c * GDN_ROW_SLOTS + HEADS, HEADS), HEADS), :]
        heads = range(HEADS)
        m = []
        for h in heads:
            cs = lanes[h]
            k = k_ref[rows, cs]
            gc = gc_ref[rows, cs]
            gc_lanes = gc_by_head[h:h + 1, :]
            decay = mask_scr[0] * jnp.exp(jnp.minimum(gc - gc_lanes, 0.0))
            k16 = k.astype(BF16)
            m.append(_dot_nt((k * beta_ref[rows, cs]).astype(BF16), k16) * decay)
            qk_scr[rows, cs] = (_dot_nt((q_ref[rows, cs] * GDN_SCALE).astype(BF16), k16) * decay).astype(BF16)
        base = [m[h] * mask_scr[1] for h in heads]
        b16 = [base[h].astype(BF16) for h in heads]
        power = [_dot(b16[h], b16[h]).astype(BF16) for h in heads]
        inv = [eye - base[h] for h in heads]
        inv = [inv[h] + _dot(inv[h].astype(BF16), power[h]) for h in heads]
        for _ in range(int(math.log2(GDN_BASE)) - 2):
            power = [_dot(power[h], power[h]).astype(BF16) for h in heads]
            inv = [inv[h] + _dot(inv[h].astype(BF16), power[h]) for h in heads]
        for n in range(len(merge_sizes)):
            i16 = [inv[h].astype(BF16) for h in heads]
            half = [_dot(i16[h], (m[h] * mask_scr[2 + n]).astype(BF16)).astype(BF16) for h in heads]
            inv = [inv[h] - _dot(half[h], i16[h]) for h in heads]
        for h in heads:
            cs = lanes[h]
            k = k_ref[rows, cs]
            beta = beta_ref[rows, cs]
            gc = gc_ref[rows, cs]
            egc = jnp.exp(gc)
            rhs = jnp.concatenate([v_ref[rows, cs] * beta, k * beta * egc], axis=-1)
            sol = rhs + _dot((inv[h] - eye).astype(BF16), rhs.astype(BF16))
            u_scr[rows, cs] = sol[:, :HEAD_DIM]
            w_scr[rows, cs] = sol[:, HEAD_DIM:].astype(BF16)
            gc_last = jnp.broadcast_to(gc[GDN_CHUNK - 1:GDN_CHUNK, :], (GDN_CHUNK, HEAD_DIM))
            qd_scr[rows, cs] = (q_ref[rows, cs] * GDN_SCALE * egc).astype(BF16)
            kd_scr[rows, cs] = (k * jnp.exp(gc_last - gc)).astype(BF16)
        return carry

    lax.fori_loop(0, n_chunks, prepare, 0)

    def advance(c, carry):
        rows = pl.ds(pl.multiple_of(c * GDN_CHUNK, GDN_CHUNK), GDN_CHUNK)
        tail = pl.ds(pl.multiple_of(c * GDN_CHUNK + GDN_CHUNK - 8, 8), 8)
        heads = range(HEADS)
        state = [s_scr[h] for h in heads]
        s16 = [state[h].astype(BF16) for h in heads]
        vn16 = [(u_scr[rows, lanes[h]] - _dot(w_scr[rows, lanes[h]], s16[h])).astype(BF16) for h in heads]
        for h in heads:
            s_scr[h] = state[h] * jnp.exp(gc_ref[tail, lanes[h]][7:8, :]) + lax.dot_general(
                kd_scr[rows, lanes[h]], vn16[h], (((0,), (0,)), ((), ())), preferred_element_type=F32)
        for h in heads:
            cs = lanes[h]
            o = _dot(qd_scr[rows, cs], s16[h]) + _dot(qk_scr[rows, cs], vn16[h])
            zg = z_ref[rows, cs]
            o_ref[rows, cs] = (_rms(o, gain) * (zg * _sigmoid(zg))).astype(BF16)
        return carry

    lax.fori_loop(0, n_chunks, advance, 0)

    @pl.when(ti == n_tiles - 1)
    def _():
        sfin_ref[...] = s_scr[...]


def _gdn_chunks(qkv, beta, gc, gcrow, z_src, z_col_block, s0, out_norm, layer, n_seq, seq_rows):
    tile_rows = min(seq_rows, GDN_TS)
    n_tiles = seq_rows // tile_rows
    n_masks = 2 + int(math.log2(GDN_CHUNK // GDN_BASE))

    def act(col_block):
        return pl.BlockSpec((tile_rows, WIDTH), lambda b, t: (b * n_tiles + t, col_block))

    slots = tile_rows // GDN_CHUNK * GDN_ROW_SLOTS
    state_spec = pl.BlockSpec((None, HEADS, HEAD_DIM, HEAD_DIM), lambda b, t: (b, 0, 0, 0))
    return pl.pallas_call(
        functools.partial(_gdn_chunk_body, n_tiles=n_tiles, tile_rows=tile_rows),
        grid=(n_seq, n_tiles),
        in_specs=[act(0), act(1), act(2), act(0), act(0),
                  pl.BlockSpec((slots, GDN_CHUNK), lambda b, t: (b * n_tiles + t, 0)),
                  act(z_col_block), state_spec,
                  pl.BlockSpec((None, 1, HEAD_DIM), lambda b, t: (layer, 0, 0))],
        out_specs=[act(0), state_spec],
        out_shape=[jax.ShapeDtypeStruct((n_seq * seq_rows, WIDTH), BF16),
                   jax.ShapeDtypeStruct((n_seq, HEADS, HEAD_DIM, HEAD_DIM), F32)],
        scratch_shapes=[pltpu.VMEM((HEADS, HEAD_DIM, HEAD_DIM), F32),
                        pltpu.VMEM((n_masks, GDN_CHUNK, GDN_CHUNK), F32),
                        pltpu.VMEM((tile_rows, WIDTH), F32),
                        pltpu.VMEM((tile_rows, WIDTH), BF16),
                        pltpu.VMEM((tile_rows, WIDTH), BF16),
                        pltpu.VMEM((tile_rows, WIDTH), BF16),
                        pltpu.VMEM((tile_rows, WIDTH), BF16)],
        compiler_params=_params("parallel", "arbitrary"),
        name="gdn_chunks",
    )(qkv, qkv, qkv, beta, gc, gcrow, z_src, s0, out_norm)


def _pack_w_in(w_in):
    w16 = w_in.astype(BF16)
    pad = jnp.zeros((DEPTH, D_MODEL, C_GATE - C_BA - 2 * HEADS), BF16)
    return jnp.concatenate([w16[..., :C_BA + 2 * HEADS], pad, w16[..., C_BA + 2 * HEADS:]], axis=-1)


def _s5_params(a_re, a_im, log_dt, b_re, b_im, c_re, c_im, d):
    eye = jnp.eye(S5_GB, dtype=F32)

    def state_rows(x):
        return x.reshape(DEPTH, S5_NGB, 1, S5_ST)

    par = jnp.concatenate(
        [state_rows(a_re), state_rows(a_im),
         state_rows(jnp.repeat(log_dt, SSM_STATE, axis=-1)),
         jnp.zeros((DEPTH, S5_NGB, 5, S5_ST), F32)], axis=2)

    def b_blockdiag(b):
        b = b.reshape(DEPTH, S5_NGB, S5_GB, SSM_STATE, SSM_GROUP)
        return jnp.einsum('dbgpc,gh->dbgchp', b, eye).reshape(DEPTH, S5_NGB, S5_CH, S5_ST)

    def c_blockdiag(c):
        c = c.reshape(DEPTH, S5_NGB, S5_GB, SSM_GROUP, SSM_STATE)
        return jnp.einsum('dbgcp,gh->dbgphc', c, eye).reshape(DEPTH, S5_NGB, S5_ST, S5_CH)

    return (par, b_blockdiag(b_re), b_blockdiag(b_im), c_blockdiag(c_re), c_blockdiag(c_im),
            d.reshape(DEPTH, S5_NGB, 1, S5_CH))


def _segment_major(x):
    w = x.shape[-1]
    return x.reshape(BATCH, S5_LANES, S5_SEG, w).transpose(0, 2, 1, 3).reshape(N_PROMPT, w)


def _token_major(x):
    w = x.shape[-1]
    return x.reshape(BATCH, S5_SEG, S5_LANES, w).transpose(0, 2, 1, 3).reshape(N_PROMPT, w)


def _pad_rows(x):
    return jnp.concatenate([x, jnp.zeros((ROWS - x.shape[0], x.shape[1]), x.dtype)], axis=0)


def _one_row_chunks(x):
    return jnp.pad(x[:, None, :], ((0, 0), (0, GDN_CHUNK - 1), (0, 0))).reshape(DEC_BATCH * GDN_CHUNK, x.shape[-1])


def kernel(x_prompt, x_sample, cache_k, cache_v, page_table, state_ssm, state_conv, state_gdn, ffn1_norm, ffn1_w_gate, ffn1_w_up, ffn1_w_down, mix_norm, w_in, sb_bias, ssm_a_re, ssm_a_im, ssm_log_dt, ssm_b_re, ssm_b_im, ssm_c_re, ssm_c_im, ssm_d, ssm_w_glu, gdn_conv_w, gdn_a_log, gdn_dt_bias, gdn_out_norm, w_branch, w_out, ffn2_norm, ffn2_w_gate, ffn2_w_up, ffn2_w_down, final_norm):
    bf = lambda w: w.astype(BF16)
    gain3 = lambda g: g.reshape(DEPTH, 1, D_MODEL)
    w_in_p = _pack_w_in(w_in)
    ffn1 = (gain3(ffn1_norm), bf(ffn1_w_gate), bf(ffn1_w_up), bf(ffn1_w_down))
    ffn2 = (gain3(ffn2_norm), bf(ffn2_w_gate), bf(ffn2_w_up), bf(ffn2_w_down))
    mix_gain = gain3(mix_norm)
    w_glu, w_br, w_o = bf(ssm_w_glu), bf(w_branch).reshape(DEPTH, 3, WIDTH, D_MODEL), bf(w_out)
    s5p = _s5_params(ssm_a_re, ssm_a_im, ssm_log_dt, ssm_b_re, ssm_b_im, ssm_c_re, ssm_c_im, ssm_d)
    bias_flat = sb_bias.reshape(DEPTH * HEADS)
    bias_lanes = jnp.tile(sb_bias, (1, PAGE)).reshape(DEPTH, 1, DEC_LANES)
    page_flat = page_table.reshape(DEC_BATCH * N_PAGES)
    conv_w8 = jnp.pad(gdn_conv_w, ((0, 0), (0, 8 - CONV_TAPS), (0, 0)))
    conv_hist = state_conv.transpose(0, 2, 1, 3)
    lane_pad = jnp.zeros((DEPTH, 128 - 2 * HEADS), F32)
    gate_par = jnp.stack([jnp.concatenate([jnp.zeros((DEPTH, HEADS), F32), gdn_a_log, lane_pad], axis=1),
                          jnp.concatenate([jnp.zeros((DEPTH, HEADS), F32), gdn_dt_bias, lane_pad], axis=1)]
                         + [jnp.zeros((DEPTH, 128), F32)] * 6, axis=1)
    out_norm = gdn_out_norm.reshape(DEPTH, 1, HEAD_DIM)
    ssm_h0r = state_ssm[..., 0].reshape(DEPTH, DEC_BATCH, WIDTH * SSM_STATE // SSM_GROUP)
    ssm_h0i = state_ssm[..., 1].reshape(DEPTH, DEC_BATCH, WIDTH * SSM_STATE // SSM_GROUP)

    x = _pad_rows(jnp.concatenate([x_prompt.reshape(N_PROMPT, D_MODEL),
                                   x_sample.reshape(DEC_BATCH, D_MODEL)], axis=0))
    sample = slice(N_PROMPT, N_PROMPT + DEC_BATCH)
    ks, vs, sp, ss, cp, cs_, gp, gs = ([] for _ in range(8))
    k_all = jnp.zeros((DEPTH, N_PROMPT, WIDTH), F32)
    v_all = jnp.zeros((DEPTH, N_PROMPT, WIDTH), F32)
    for l in range(DEPTH):
        x = _ffn(x, *ffn1, l)
        proj = _in_proj(x, mix_gain, w_in_p, l)

        oa_p, k_all, v_all = _sb_prompt(proj, bias_flat, l, k_all, v_all)
        q_s = proj[sample, C_Q:C_Q + WIDTH].reshape(DEC_BATCH, HEADS, HEAD_DIM)
        oa_s = _sb_decode(q_s, cache_k, cache_v, page_flat, bias_lanes, l)
        oa = _pad_rows(jnp.concatenate([oa_p, oa_s.reshape(DEC_BATCH, WIDTH).astype(BF16)], axis=0))

        gb_p, hr_p, hi_p = _s5_prompt(_segment_major(proj[:N_PROMPT, C_U:C_U + WIDTH]), s5p, l)
        gb_s, hr_s, hi_s = _s5_sample(proj, s5p, ssm_h0r[l], ssm_h0i[l], l)
        ob = _glu(_pad_rows(jnp.concatenate([_token_major(gb_p), gb_s], axis=0)), w_glu, l)

        qkv_p = _conv_prompt(proj, conv_w8, l)
        beta_p, gc_p, gcrow_p = _gdn_gates(proj, gate_par, l, GDN_TS, N_PROMPT // GDN_TS, 0, GDN_CHUNK)
        oc_p, gdn_p = _gdn_chunks(qkv_p, beta_p, gc_p, gcrow_p, proj, C_Z // WIDTH,
                                  jnp.zeros((BATCH, HEADS, HEAD_DIM, HEAD_DIM), F32), out_norm, l, BATCH, SEQ)
        qkv_s = _conv_sample(proj, conv_hist, conv_w8, l)
        beta_s, g_s = _gdn_gates(proj, gate_par, l, DEC_BATCH, 1, SAMPLE_ROW_BLOCK, 1)
        gc_s = jnp.broadcast_to(g_s[:, None, :], (DEC_BATCH, GDN_CHUNK, WIDTH)).reshape(DEC_BATCH * GDN_CHUNK, WIDTH)
        g_heads = jnp.broadcast_to(g_s[:, ::HEAD_DIM, None], (DEC_BATCH, HEADS, GDN_CHUNK))
        gcrow_s = jnp.concatenate([jnp.zeros_like(g_heads), g_heads], axis=1).reshape(
            DEC_BATCH * GDN_ROW_SLOTS, GDN_CHUNK)
        oc_s, gdn_s = _gdn_chunks(_one_row_chunks(qkv_s), _one_row_chunks(beta_s), gc_s, gcrow_s,
                                  _one_row_chunks(proj[sample, C_Z:C_Z + WIDTH]), 0,
                                  state_gdn[l], out_norm, l, DEC_BATCH, GDN_CHUNK)
        oc = _pad_rows(jnp.concatenate([oc_p, oc_s[::GDN_CHUNK]], axis=0))

        x = _merge(x, oa, ob, oc, proj, w_br, w_o, l)
        x = _ffn(x, *ffn2, l)

        ks.append(proj[sample, C_K:C_K + WIDTH].reshape(DEC_BATCH, 1, HEADS, HEAD_DIM))
        vs.append(proj[sample, C_V:C_V + WIDTH].reshape(DEC_BATCH, 1, HEADS, HEAD_DIM))
        state = lambda r, i, n: jnp.stack([r.reshape(n, WIDTH // SSM_GROUP, SSM_STATE),
                                           i.reshape(n, WIDTH // SSM_GROUP, SSM_STATE)], axis=-1)
        sp.append(state(hr_p, hi_p, BATCH))
        ss.append(state(hr_s, hi_s, DEC_BATCH))
        qkv_cols = slice(C_QKV, C_QKV + CONV_WIDTH)
        cp.append(jnp.stack([proj[(b + 1) * SEQ - (CONV_TAPS - 1):(b + 1) * SEQ, qkv_cols] for b in range(BATCH)]))
        cs_.append(jnp.concatenate([state_conv[l][:, 1:], proj[sample, qkv_cols][:, None, :]], axis=1))
        gp.append(gdn_p)
        gs.append(gdn_s)

    gain = final_norm.reshape(1, D_MODEL)
    y_p = _final_norm(x, gain, GDN_TS, N_PROMPT // GDN_TS, 0)
    y_s = _final_norm(x, gain, DEC_BATCH, 1, SAMPLE_ROW_BLOCK)
    kv_shape = (DEPTH, BATCH, SEQ, HEADS, HEAD_DIM)
    return (y_p.reshape(BATCH, SEQ, D_MODEL), y_s.reshape(DEC_BATCH, 1, D_MODEL),
            k_all.reshape(kv_shape), v_all.reshape(kv_shape), jnp.stack(ks), jnp.stack(vs),
            jnp.stack(sp), jnp.stack(ss), jnp.stack(cp), jnp.stack(cs_),
            jnp.stack(gp), jnp.stack(gs))
```

```python
import functools
import math

import jax
import jax.numpy as jnp
from jax import lax
from jax.experimental import pallas as pl
from jax.experimental.pallas import tpu as pltpu

F32 = jnp.float32
BF16 = jnp.bfloat16
HIGHEST = lax.Precision.HIGHEST

D_MODEL = 2048
BATCH = 4
SEQ = 2048
DEPTH = 2
DEC_BATCH = 32
PAGE = 128
N_PAGES = 64
HEADS = 8
HEAD_DIM = 128
WIDTH = HEADS * HEAD_DIM
SSM_GROUP = 16
SSM_STATE = 64
CONV_TAPS = 4
CONV_WIDTH = 3 * WIDTH
FFN_DIM = 5632
EPS = 1e-6

N_PROMPT = BATCH * SEQ
TM = 528
ROWS = 16 * TM
SAMPLE_ROW_BLOCK = N_PROMPT // DEC_BATCH

C_Q, C_K, C_V, C_U, C_QKV, C_Z, C_GATE, C_BA = 0, 1024, 2048, 3072, 4096, 7168, 8192, 14336
PROJ_COLS = 14592
TN_PROJ = 768
PACK_TN = 256
TF = 512
TM_BIG = 2 * TM
TN_MERGE = 512

VMEM_LIMIT = 56 * 1024 * 1024


def _params(*sem):
    return pltpu.CompilerParams(dimension_semantics=sem, vmem_limit_bytes=VMEM_LIMIT)


def _rms(x, gain):
    ms = jnp.mean(x * x, axis=-1, keepdims=True)
    return x * lax.rsqrt(ms + EPS) * gain


def _softplus(x):
    return jnp.maximum(x, 0.0) + jnp.log(1.0 + jnp.exp(-jnp.abs(x)))


def _sigmoid(x):
    return 1.0 / (1.0 + jnp.exp(-x))


def _dot(a, b, **kw):
    return jnp.dot(a, b, preferred_element_type=F32, **kw)


def _dot_nt(a, b, **kw):
    return lax.dot_general(a, b, (((1,), (1,)), ((), ())), preferred_element_type=F32, **kw)


def _ffn_body(x_ref, g_ref, wg_ref, wu_ref, wd_ref, o_ref, h_scr):
    @pl.when(pl.program_id(1) == 0)
    def _():
        x = x_ref[...]
        h_scr[...] = _rms(x, g_ref[...]).astype(BF16)
        o_ref[...] = x

    h = h_scr[...]
    a = _dot(h, wg_ref[...])
    b = _dot(h, wu_ref[...])
    act = (a * _sigmoid(a)) * b * 0.5
    o_ref[...] += _dot(act.astype(BF16), wd_ref[...])


def _ffn(x, gain, wg, wu, wd, layer):
    return pl.pallas_call(
        _ffn_body,
        grid=(ROWS // TM, FFN_DIM // TF),
        in_specs=[
            pl.BlockSpec((TM, D_MODEL), lambda i, f: (i, 0)),
            pl.BlockSpec((None, 1, D_MODEL), lambda i, f: (layer, 0, 0)),
            pl.BlockSpec((None, D_MODEL, TF), lambda i, f: (layer, 0, f)),
            pl.BlockSpec((None, D_MODEL, TF), lambda i, f: (layer, 0, f)),
            pl.BlockSpec((None, TF, D_MODEL), lambda i, f: (layer, f, 0)),
        ],
        out_specs=pl.BlockSpec((TM, D_MODEL), lambda i, f: (i, 0)),
        out_shape=jax.ShapeDtypeStruct((ROWS, D_MODEL), F32),
        scratch_shapes=[pltpu.VMEM((TM, D_MODEL), BF16)],
        compiler_params=_params("parallel", "arbitrary"),
        name="ffn",
    )(x, gain, wg, wu, wd)


def _proj_body(x_ref, g_ref, w_ref, o_ref, h_scr):
    @pl.when(pl.program_id(1) == 0)
    def _():
        h_scr[...] = _rms(x_ref[...], g_ref[...]).astype(BF16)

    o_ref[...] = _dot(h_scr[...], w_ref[...])


def _in_proj(x, gain, w, layer):
    return pl.pallas_call(
        _proj_body,
        grid=(ROWS // TM_BIG, PROJ_COLS // TN_PROJ),
        in_specs=[
            pl.BlockSpec((TM_BIG, D_MODEL), lambda i, n: (i, 0), pipeline_mode=pl.Buffered(1)),
            pl.BlockSpec((None, 1, D_MODEL), lambda i, n: (layer, 0, 0)),
            pl.BlockSpec((None, D_MODEL, TN_PROJ), lambda i, n: (layer, 0, n)),
        ],
        out_specs=pl.BlockSpec((TM_BIG, TN_PROJ), lambda i, n: (i, n)),
        out_shape=jax.ShapeDtypeStruct((ROWS, PROJ_COLS), F32),
        scratch_shapes=[pltpu.VMEM((TM_BIG, D_MODEL), BF16)],
        compiler_params=_params("parallel", "arbitrary"),
        name="in_proj",
    )(x, gain, w)


def _glu_body(g_ref, w_ref, o_ref):
    g = g_ref[...]
    o_ref[...] = (g * _sigmoid(_dot(g.astype(BF16), w_ref[...]))).astype(BF16)


def _glu(g, w, layer):
    return pl.pallas_call(
        _glu_body,
        grid=(ROWS // TM,),
        in_specs=[
            pl.BlockSpec((TM, WIDTH), lambda i: (i, 0)),
            pl.BlockSpec((None, WIDTH, WIDTH), lambda i: (layer, 0, 0)),
        ],
        out_specs=pl.BlockSpec((TM, WIDTH), lambda i: (i, 0)),
        out_shape=jax.ShapeDtypeStruct((ROWS, WIDTH), BF16),
        compiler_params=_params("parallel"),
        name="glu",
    )(g, w)


def _merge_body(x_ref, oa_ref, ob_ref, oc_ref, ga_ref, gb_ref, gc_ref, wa_ref, wb_ref, wc_ref, wo_ref, o_ref):
    @pl.when(pl.program_id(1) == 0)
    def _():
        o_ref[...] = x_ref[...]

    m = (_sigmoid(ga_ref[...]) * _dot(oa_ref[...], wa_ref[...])
         + _sigmoid(gb_ref[...]) * _dot(ob_ref[...], wb_ref[...])
         + _sigmoid(gc_ref[...]) * _dot(oc_ref[...], wc_ref[...]))
    o_ref[...] += _dot(m.astype(BF16), wo_ref[...])


def _merge(x, oa, ob, oc, proj, w_branch, w_out, layer):
    gate_blk = C_GATE // TN_MERGE
    per_gate = D_MODEL // TN_MERGE
    o_spec = pl.BlockSpec((TM, WIDTH), lambda i, n: (i, 0))

    def gate_spec(which):
        return pl.BlockSpec((TM, TN_MERGE), lambda i, n: (i, gate_blk + which * per_gate + n))

    def w_spec(which):
        return pl.BlockSpec((None, None, WIDTH, TN_MERGE), lambda i, n: (layer, which, 0, n))

    return pl.pallas_call(
        _merge_body,
        grid=(ROWS // TM, D_MODEL // TN_MERGE),
        in_specs=[
            pl.BlockSpec((TM, D_MODEL), lambda i, n: (i, 0)),
            o_spec, o_spec, o_spec,
            gate_spec(0), gate_spec(1), gate_spec(2),
            w_spec(0), w_spec(1), w_spec(2),
            pl.BlockSpec((None, TN_MERGE, D_MODEL), lambda i, n: (layer, n, 0)),
        ],
        out_specs=pl.BlockSpec((TM, D_MODEL), lambda i, n: (i, 0)),
        out_shape=jax.ShapeDtypeStruct((ROWS, D_MODEL), F32),
        compiler_params=_params("parallel", "arbitrary"),
        name="merge",
    )(x, oa, ob, oc, proj, proj, proj, w_branch, w_branch, w_branch, w_out)


def _final_norm_body(x_ref, g_ref, o_ref):
    o_ref[...] = _rms(x_ref[...], g_ref[...])


def _final_norm(x, gain, rows, n_blocks, first_block):
    return pl.pallas_call(
        _final_norm_body,
        grid=(n_blocks,),
        in_specs=[pl.BlockSpec((rows, D_MODEL), lambda i: (first_block + i, 0)),
                  pl.BlockSpec((1, D_MODEL), lambda i: (0, 0))],
        out_specs=pl.BlockSpec((rows, D_MODEL), lambda i: (i, 0)),
        out_shape=jax.ShapeDtypeStruct((rows * n_blocks, D_MODEL), F32),
        compiler_params=_params("parallel"),
        name="final_norm",
    )(x, gain)


SB_BQ = 256
SB_BK = 256
SB_HB = 4
SB_SCALE = HEAD_DIM ** -0.5


def _stick_terms(z):
    t = jnp.log(1.0 + jnp.exp(-jnp.abs(z)))
    return jnp.minimum(z, 0.0) - t, jnp.minimum(-z, 0.0) - t


LOG2E = 1.0 / math.log(2.0)


def _sb_prompt_body(bias_ref, q_ref, k_ref, v_ref, *rest, layer):
    o_ref, k_all_ref, v_all_ref, q_scr, acc_scr, run_scr = rest[2:]
    hg = pl.program_id(1)
    qi = pl.program_id(2)

    @pl.when(qi == 0)
    def _():
        k_all_ref[...] = k_ref[...]
        v_all_ref[...] = v_ref[...]

    heads = range(SB_HB)
    lanes = [slice(hh * HEAD_DIM, (hh + 1) * HEAD_DIM) for hh in heads]
    row = lax.broadcasted_iota(jnp.int32, (SB_BQ, SB_BK), 0)
    col = lax.broadcasted_iota(jnp.int32, (SB_BQ, SB_BK), 1)
    q_pos = qi * SB_BQ + row
    r2 = lax.broadcasted_iota(jnp.int32, (SB_BK, SB_BK), 0)
    c2 = lax.broadcasted_iota(jnp.int32, (SB_BK, SB_BK), 1)
    later_key = jnp.where(r2 > c2, 1.0, 0.0).astype(BF16)
    bias2 = [bias_ref[layer * HEADS + hg * SB_HB + hh] * LOG2E for hh in heads]

    for hh in heads:
        q_scr[hh] = (q_ref[:, lanes[hh]] * (SB_SCALE * LOG2E)).astype(BF16)
    acc_scr[...] = jnp.zeros_like(acc_scr)
    run_scr[...] = jnp.zeros_like(run_scr)

    def visit(j, masked):
        k_rows = pl.ds(pl.multiple_of(j * SB_BK, SB_BK), SB_BK)
        mask = (j * SB_BK + col) < q_pos
        z = [_dot_nt(q_scr[hh], k_ref[k_rows, lanes[hh]].astype(BF16)) + bias2[hh] for hh in heads]
        t = [jnp.log(1.0 + jnp.exp2(-jnp.abs(z[hh]))) * LOG2E for hh in heads]
        log_hit = [jnp.minimum(z[hh], 0.0) - t[hh] for hh in heads]
        log_fail = [jnp.minimum(-z[hh], 0.0) - t[hh] for hh in heads]
        if masked:
            log_fail = [jnp.where(mask, log_fail[hh], 0.0) for hh in heads]
        suffix = [_dot(log_fail[hh].astype(BF16), later_key) for hh in heads]
        w = []
        for hh in heads:
            run = run_scr[hh]
            w_h = jnp.exp2(log_hit[hh] + suffix[hh] + run)
            block_total = suffix[hh][:, 0:1] + log_fail[hh][:, 0:1]
            run_scr[hh] = run + jnp.broadcast_to(block_total, (SB_BQ, SB_BK))
            w.append(jnp.where(mask, w_h, 0.0) if masked else w_h)
        for hh in heads:
            acc_scr[hh] += _dot(w[hh].astype(BF16), v_ref[k_rows, lanes[hh]].astype(BF16))

    visit(qi, True)

    def below_diagonal(it, carry):
        visit(qi - 1 - it, False)
        return carry

    lax.fori_loop(0, qi, below_diagonal, 0)
    for hh in heads:
        o_ref[:, lanes[hh]] = acc_scr[hh].astype(BF16)


def _sb_prompt(proj, bias_flat, layer, k_all, v_all):
    hw = SB_HB * HEAD_DIM
    n_hg = HEADS // SB_HB
    nq = SEQ // SB_BQ
    all_spec = pl.BlockSpec((None, SEQ, hw), lambda b, h, i, bias: (layer, b, h))
    all_shape = jax.ShapeDtypeStruct((DEPTH, N_PROMPT, WIDTH), F32)
    grid_spec = pltpu.PrefetchScalarGridSpec(
        num_scalar_prefetch=1,
        grid=(BATCH, n_hg, nq),
        in_specs=[
            pl.BlockSpec((SB_BQ, hw), lambda b, h, i, bias: (b * nq + i, C_Q // hw + h)),
            pl.BlockSpec((SEQ, hw), lambda b, h, i, bias: (b, C_K // hw + h)),
            pl.BlockSpec((SEQ, hw), lambda b, h, i, bias: (b, C_V // hw + h)),
            pl.BlockSpec(memory_space=pl.ANY),
            pl.BlockSpec(memory_space=pl.ANY),
        ],
        out_specs=[pl.BlockSpec((SB_BQ, hw), lambda b, h, i, bias: (b * nq + i, h)), all_spec, all_spec],
        scratch_shapes=[pltpu.VMEM((SB_HB, SB_BQ, HEAD_DIM), BF16),
                        pltpu.VMEM((SB_HB, SB_BQ, HEAD_DIM), F32),
                        pltpu.VMEM((SB_HB, SB_BQ, SB_BK), F32)],
    )
    return pl.pallas_call(
        functools.partial(_sb_prompt_body, layer=layer),
        grid_spec=grid_spec,
        out_shape=[jax.ShapeDtypeStruct((N_PROMPT, WIDTH), BF16), all_shape, all_shape],
        input_output_aliases={4: 1, 5: 2},
        compiler_params=_params("parallel", "parallel", "arbitrary"),
        name="sb_prompt",
    )(bias_flat, proj, proj, proj, k_all, v_all)


DEC_PAGES = 16
DEC_STEPS = N_PAGES // DEC_PAGES
DEC_LANES = PAGE * HEADS


def _lane_shift_up(x, s):
    n = x.shape[1]
    lane = lax.broadcasted_iota(jnp.int32, x.shape, 1)
    return jnp.where(lane < n - s, pltpu.roll(x, n - s, 1), 0.0)


def _sb_decode_body(pt_ref, q_ref, bias_ref, *refs):
    k_refs = refs[:DEC_PAGES]
    v_refs = refs[DEC_PAGES:2 * DEC_PAGES]
    o_ref, acc_scr, run_scr = refs[2 * DEC_PAGES:]
    g = pl.program_id(1)

    @pl.when(g == 0)
    def _():
        acc_scr[...] = jnp.zeros_like(acc_scr)
        run_scr[...] = jnp.zeros_like(run_scr)

    sub = lax.broadcasted_iota(jnp.int32, (HEADS, DEC_LANES), 0)
    lane = lax.broadcasted_iota(jnp.int32, (HEADS, DEC_LANES), 1)
    own_head = (lane & (HEADS - 1)) == sub
    q = q_ref[...].astype(BF16)
    rows = []
    for i in range(DEC_PAGES):
        k2 = k_refs[i][...].reshape(DEC_LANES, HEAD_DIM).astype(BF16)
        zt = _dot_nt(q, k2)
        rows.append(jnp.sum(jnp.where(own_head, zt, 0.0), axis=0, keepdims=True))
    z = jnp.concatenate(rows, axis=0) * SB_SCALE + bias_ref[...]
    log_hit, log_fail = _stick_terms(z)

    suffix = log_fail
    for s in (HEADS, 4 * HEADS):
        suffix = (suffix + _lane_shift_up(suffix, s)) + (_lane_shift_up(suffix, 2 * s) + _lane_shift_up(suffix, 3 * s))
    s = 16 * HEADS
    while s < DEC_LANES:
        suffix = suffix + jnp.concatenate([suffix[:, s:], jnp.zeros((DEC_PAGES, s), F32)], axis=1)
        s *= 2
    tile_lane = lax.broadcasted_iota(jnp.int32, (DEC_PAGES, 16 * HEADS), 1)
    tot = jnp.where(tile_lane < HEADS, suffix[:, :16 * HEADS], 0.0)
    for s in (HEADS, 4 * HEADS):
        tot = (tot + pltpu.roll(tot, s, 1)) + (pltpu.roll(tot, 2 * s, 1) + pltpu.roll(tot, 3 * s, 1))
    tot = jnp.concatenate([tot] * (DEC_LANES // (16 * HEADS)), axis=1)
    page_r = lax.broadcasted_iota(jnp.int32, (DEC_PAGES, DEC_PAGES), 0)
    page_c = lax.broadcasted_iota(jnp.int32, (DEC_PAGES, DEC_PAGES), 1)
    later = jnp.where(page_c > page_r, 1.0, 0.0)
    after = _dot(later, tot, precision=HIGHEST)
    run = run_scr[...]
    log_between = (suffix - log_fail) + after + run
    run_scr[...] = run + jnp.sum(tot, axis=0, keepdims=True)
    w = jnp.exp(log_hit + log_between)

    acc = acc_scr[...]
    for i in range(DEC_PAGES):
        wm = jnp.where(own_head, jnp.broadcast_to(w[i:i + 1, :], (HEADS, DEC_LANES)), 0.0).astype(BF16)
        v2 = v_refs[i][...].reshape(DEC_LANES, HEAD_DIM).astype(BF16)
        acc = acc + _dot(wm, v2)
    acc_scr[...] = acc

    @pl.when(g == DEC_STEPS - 1)
    def _():
        o_ref[...] = acc


def _sb_decode(q, cache_k, cache_v, page_table_flat, bias_lanes, layer):
    def page_spec(i):
        def index(b, g, pt):
            return (layer, pt[b * N_PAGES + (DEC_STEPS - 1 - g) * DEC_PAGES + i], 0, 0, 0)
        return pl.BlockSpec((None, None, PAGE, HEADS, HEAD_DIM), index)

    grid_spec = pltpu.PrefetchScalarGridSpec(
        num_scalar_prefetch=1,
        grid=(DEC_BATCH, DEC_STEPS),
        in_specs=[pl.BlockSpec((None, HEADS, HEAD_DIM), lambda b, g, pt: (b, 0, 0)),
                  pl.BlockSpec((None, 1, DEC_LANES), lambda b, g, pt: (layer, 0, 0))]
        + [page_spec(i) for i in range(DEC_PAGES)] * 2,
        out_specs=pl.BlockSpec((None, HEADS, HEAD_DIM), lambda b, g, pt: (b, 0, 0)),
        scratch_shapes=[pltpu.VMEM((HEADS, HEAD_DIM), F32), pltpu.VMEM((1, DEC_LANES), F32)],
    )
    return pl.pallas_call(
        _sb_decode_body,
        grid_spec=grid_spec,
        out_shape=jax.ShapeDtypeStruct((DEC_BATCH, HEADS, HEAD_DIM), F32),
        compiler_params=_params("parallel", "arbitrary"),
        name="sb_decode",
    )(page_table_flat, q, bias_lanes, *([cache_k] * DEC_PAGES), *([cache_v] * DEC_PAGES))


S5_GB = 8
S5_CH = S5_GB * SSM_GROUP
S5_ST = S5_GB * SSM_STATE
S5_NGB = WIDTH // S5_CH
S5_LANES = 8
S5_SEG = SEQ // S5_LANES


def _s5_discretize(par):
    ar, ai, ldt = par[0:1], par[1:2], par[2:3]
    dt = jnp.exp(ldt)
    mag = jnp.exp(dt * ar)
    abr = mag * jnp.cos(dt * ai)
    abi = mag * jnp.sin(dt * ai)
    den = ar * ar + ai * ai
    cr = ((abr - 1.0) * ar + abi * ai) / den
    ci = (abi * ar - (abr - 1.0) * ai) / den
    return abr, abi, cr, ci


def _gelu(y):
    return 0.5 * y * (1.0 + jnp.tanh(math.sqrt(2.0 / math.pi) * (y + 0.044715 * (y * y * y))))


def _s5_prompt_body(u_ref, par_ref, bre_ref, bim_ref, cre_ref, cim_ref, d_ref, h0r_ref, h0i_ref,
                    o_ref, hr_ref, hi_ref, xr_scr, xi_scr):
    abr, abi, cr, ci = _s5_discretize(par_ref[...])
    bre, bim = bre_ref[...], bim_ref[...]
    bbr = (cr * bre - ci * bim).astype(BF16)
    bbi = (cr * bim + ci * bre).astype(BF16)
    u = u_ref[...]
    ub = u.astype(BF16)
    xr_scr[...] = _dot(ub, bbr)
    xi_scr[...] = _dot(ub, bbi)

    shape = (S5_LANES, S5_ST)
    ar8 = jnp.broadcast_to(abr, shape)
    ai8 = jnp.broadcast_to(abi, shape)

    def local_step(t, carry):
        xr, xi = carry
        rows = pl.ds(pl.multiple_of(t * S5_LANES, S5_LANES), S5_LANES)
        nr = ar8 * xr - ai8 * xi + xr_scr[rows, :]
        ni = ar8 * xi + ai8 * xr + xi_scr[rows, :]
        xr_scr[rows, :] = nr
        xi_scr[rows, :] = ni
        return nr, ni

    zero = jnp.zeros(shape, F32)
    fr, fi = lax.fori_loop(0, S5_SEG, local_step, (zero, zero), unroll=8)

    pr, pi = abr, abi
    for _ in range(int(math.log2(S5_SEG))):
        pr, pi = pr * pr - pi * pi, 2.0 * pr * pi
    hr, hi = h0r_ref[...], h0i_ref[...]
    in_r, in_i = [], []
    for j in range(S5_LANES):
        in_r.append(hr)
        in_i.append(hi)
        hr, hi = fr[j:j + 1] + pr * hr - pi * hi, fi[j:j + 1] + pr * hi + pi * hr
    hr_ref[...] = hr
    hi_ref[...] = hi
    cin_r = jnp.concatenate(in_r, axis=0)
    cin_i = jnp.concatenate(in_i, axis=0)

    def fix_step(t, carry):
        wr, wi = carry
        rows = pl.ds(pl.multiple_of(t * S5_LANES, S5_LANES), S5_LANES)
        xr_scr[rows, :] = xr_scr[rows, :] + wr * cin_r - wi * cin_i
        xi_scr[rows, :] = xi_scr[rows, :] + wr * cin_i + wi * cin_r
        return wr * ar8 - wi * ai8, wr * ai8 + wi * ar8

    lax.fori_loop(0, S5_SEG, fix_step, (ar8, ai8), unroll=8)

    y = (_dot(xr_scr[...].astype(BF16), cre_ref[...].astype(BF16))
         - _dot(xi_scr[...].astype(BF16), cim_ref[...].astype(BF16)) + d_ref[...] * u)
    o_ref[...] = _gelu(y)


def _s5_param_specs(layer):
    lead = (lambda *a: a[-1])

    def spec(shape):
        return pl.BlockSpec((None, None) + shape, lambda *a: (layer, lead(*a), 0, 0))

    return [spec((8, S5_ST)), spec((S5_CH, S5_ST)), spec((S5_CH, S5_ST)),
            spec((S5_ST, S5_CH)), spec((S5_ST, S5_CH)), spec((1, S5_CH))]


def _s5_prompt(u_perm, s5p, layer):
    zeros = jnp.zeros((BATCH, S5_NGB, 1, S5_ST), F32)
    state_spec = pl.BlockSpec((None, None, 1, S5_ST), lambda b, g: (b, g, 0, 0))
    out, hr, hi = pl.pallas_call(
        _s5_prompt_body,
        grid=(BATCH, S5_NGB),
        in_specs=[pl.BlockSpec((SEQ, S5_CH), lambda b, g: (b, g))] + _s5_param_specs(layer)
        + [state_spec, state_spec],
        out_specs=[pl.BlockSpec((SEQ, S5_CH), lambda b, g: (b, g)), state_spec, state_spec],
        out_shape=[jax.ShapeDtypeStruct((N_PROMPT, WIDTH), F32),
                   jax.ShapeDtypeStruct((BATCH, S5_NGB, 1, S5_ST), F32),
                   jax.ShapeDtypeStruct((BATCH, S5_NGB, 1, S5_ST), F32)],
        scratch_shapes=[pltpu.VMEM((SEQ, S5_ST), F32), pltpu.VMEM((SEQ, S5_ST), F32)],
        compiler_params=_params("parallel", "parallel"),
        name="s5_prompt",
    )(u_perm, *s5p, zeros, zeros)
    return out, hr, hi


def _s5_sample_body(u_ref, par_ref, bre_ref, bim_ref, cre_ref, cim_ref, d_ref, h0r_ref, h0i_ref,
                    o_ref, hr_ref, hi_ref):
    abr, abi, cr, ci = _s5_discretize(par_ref[...])
    bre, bim = bre_ref[...], bim_ref[...]
    bbr = (cr * bre - ci * bim).astype(BF16)
    bbi = (cr * bim + ci * bre).astype(BF16)
    u = u_ref[...]
    ub = u.astype(BF16)
    h0r, h0i = h0r_ref[...], h0i_ref[...]
    xr = _dot(ub, bbr) + abr * h0r - abi * h0i
    xi = _dot(ub, bbi) + abr * h0i + abi * h0r
    hr_ref[...] = xr
    hi_ref[...] = xi
    y = (_dot(xr.astype(BF16), cre_ref[...].astype(BF16))
         - _dot(xi.astype(BF16), cim_ref[...].astype(BF16)) + d_ref[...] * u)
    o_ref[...] = _gelu(y)


def _s5_sample(proj, s5p, h0r, h0i, layer):
    state_spec = pl.BlockSpec((DEC_BATCH, S5_ST), lambda g: (0, g))
    return pl.pallas_call(
        _s5_sample_body,
        grid=(S5_NGB,),
        in_specs=[pl.BlockSpec((DEC_BATCH, S5_CH), lambda g: (SAMPLE_ROW_BLOCK, C_U // S5_CH + g))]
        + _s5_param_specs(layer) + [state_spec, state_spec],
        out_specs=[pl.BlockSpec((DEC_BATCH, S5_CH), lambda g: (0, g)), state_spec, state_spec],
        out_shape=[jax.ShapeDtypeStruct((DEC_BATCH, WIDTH), F32),
                   jax.ShapeDtypeStruct((DEC_BATCH, S5_NGB * S5_ST), F32),
                   jax.ShapeDtypeStruct((DEC_BATCH, S5_NGB * S5_ST), F32)],
        compiler_params=_params("parallel"),
        name="s5_sample",
    )(proj, *s5p, h0r, h0i)


GDN_TS = 512
GDN_SCALE = HEAD_DIM ** -0.5


def _l2norm_heads(xc, o_ref):
    for h in range(HEADS):
        cs = slice(h * HEAD_DIM, (h + 1) * HEAD_DIM)
        seg = xc[:, cs]
        o_ref[:, cs] = seg * lax.rsqrt(jnp.sum(seg * seg, axis=-1, keepdims=True) + EPS)


def _conv_prompt_body(x_ref, prev_ref, w_ref, o_ref, ext_scr):
    i = pl.program_id(1)
    part = pl.program_id(2)
    x = x_ref[...]
    ext_scr[0:8, :] = jnp.where(i == 0, 0.0, prev_ref[...])
    ext_scr[8:8 + GDN_TS, :] = x
    w = w_ref[...]
    acc = x * w[3:4]
    for tap in range(CONV_TAPS - 1):
        back = CONV_TAPS - 1 - tap
        acc = acc + ext_scr[8 - back:8 - back + GDN_TS, :] * w[tap:tap + 1]
    xc = acc * _sigmoid(acc)

    @pl.when(part < 2)
    def _():
        _l2norm_heads(xc, o_ref)

    @pl.when(part == 2)
    def _():
        o_ref[...] = xc


def _conv_prompt(proj, conv_w, layer):
    n_t = SEQ // GDN_TS
    col0 = C_QKV // WIDTH

    def prev_index(b, i, p):
        return (jnp.maximum(b * (SEQ // 8) + i * (GDN_TS // 8) - 1, 0), col0 + p)

    return pl.pallas_call(
        _conv_prompt_body,
        grid=(BATCH, n_t, 3),
        in_specs=[
            pl.BlockSpec((GDN_TS, WIDTH), lambda b, i, p: (b * n_t + i, col0 + p)),
            pl.BlockSpec((8, WIDTH), prev_index),
            pl.BlockSpec((None, 8, WIDTH), lambda b, i, p: (layer, 0, p)),
        ],
        out_specs=pl.BlockSpec((GDN_TS, WIDTH), lambda b, i, p: (b * n_t + i, p)),
        out_shape=jax.ShapeDtypeStruct((N_PROMPT, CONV_WIDTH), F32),
        scratch_shapes=[pltpu.VMEM((GDN_TS + 8, WIDTH), F32)],
        compiler_params=_params("parallel", "parallel", "parallel"),
        name="gdn_conv_prompt",
    )(proj, proj, conv_w)


def _conv_sample_body(x_ref, hist_ref, w_ref, o_ref):
    part = pl.program_id(0)
    w = w_ref[...]
    acc = x_ref[...] * w[3:4]
    for tap in range(CONV_TAPS - 1):
        acc = acc + hist_ref[tap] * w[tap:tap + 1]
    xc = acc * _sigmoid(acc)

    @pl.when(part < 2)
    def _():
        _l2norm_heads(xc, o_ref)

    @pl.when(part == 2)
    def _():
        o_ref[...] = xc


def _conv_sample(proj, hist, conv_w, layer):
    col0 = C_QKV // WIDTH
    return pl.pallas_call(
        _conv_sample_body,
        grid=(3,),
        in_specs=[
            pl.BlockSpec((DEC_BATCH, WIDTH), lambda p: (SAMPLE_ROW_BLOCK, col0 + p)),
            pl.BlockSpec((None, CONV_TAPS - 1, DEC_BATCH, WIDTH), lambda p: (layer, 0, 0, p)),
            pl.BlockSpec((None, 8, WIDTH), lambda p: (layer, 0, p)),
        ],
        out_specs=pl.BlockSpec((DEC_BATCH, WIDTH), lambda p: (0, p)),
        out_shape=jax.ShapeDtypeStruct((DEC_BATCH, CONV_WIDTH), F32),
        compiler_params=_params("parallel"),
        name="gdn_conv_sample",
    )(proj, hist, conv_w)


GDN_CHUNK = 128
GDN_ROW_SLOTS = 16


def _gdn_gates_body(ba_ref, par_ref, beta_ref, gc_ref, gcrow_ref, *, rows, chunk):
    ba = ba_ref[...]
    par = par_ref[...]
    beta = _sigmoid(ba)
    g = -jnp.exp(par[0:1]) * _softplus(ba + par[1:2])
    if chunk > 1:
        r = lax.broadcasted_iota(jnp.int32, (rows, rows), 0)
        c = lax.broadcasted_iota(jnp.int32, (rows, rows), 1)
        tri = jnp.where((r >= c) & ((r & -chunk) == (c & -chunk)), 1.0, 0.0)
        g = _dot(tri, g, precision=HIGHEST)
        g_t = g.T
        for cc in range(rows // chunk):
            gcrow_ref[cc * GDN_ROW_SLOTS:(cc + 1) * GDN_ROW_SLOTS, :] = (
                g_t[0:GDN_ROW_SLOTS, cc * chunk:(cc + 1) * chunk])
    for h in range(HEADS):
        cs = slice(h * HEAD_DIM, (h + 1) * HEAD_DIM)
        beta_ref[:, cs] = jnp.broadcast_to(beta[:, h:h + 1], (rows, HEAD_DIM))
        gc_ref[:, cs] = jnp.broadcast_to(g[:, HEADS + h:HEADS + h + 1], (rows, HEAD_DIM))


def _gdn_gates(proj, gate_par, layer, rows, n_blocks, first_block, chunk):
    out = jax.ShapeDtypeStruct((rows * n_blocks, WIDTH), F32)
    out_specs = [pl.BlockSpec((rows, WIDTH), lambda i: (i, 0))] * 2
    out_shape = [out, out]
    if chunk > 1:
        slots = rows // chunk * GDN_ROW_SLOTS
        out_specs.append(pl.BlockSpec((slots, chunk), lambda i: (i, 0)))
        out_shape.append(jax.ShapeDtypeStruct((slots * n_blocks, chunk), F32))
        body = functools.partial(_gdn_gates_body, rows=rows, chunk=chunk)
    else:
        body = functools.partial(_gdn_gates_body, gcrow_ref=None, rows=rows, chunk=chunk)
    return pl.pallas_call(
        body,
        grid=(n_blocks,),
        in_specs=[pl.BlockSpec((rows, 128), lambda i: (first_block + i, C_BA // 128)),
                  pl.BlockSpec((None, 8, 128), lambda i: (layer, 0, 0))],
        out_specs=out_specs,
        out_shape=out_shape,
        compiler_params=_params("parallel"),
        name="gdn_gates",
    )(proj, gate_par)


GDN_BASE = 8


def _gdn_chunk_body(q_ref, k_ref, v_ref, beta_ref, gc_ref, gcrow_ref, z_ref, s0_ref, gn_ref, o_ref, sfin_ref,
                    s_scr, mask_scr, u_scr, w_scr, qk_scr, qd_scr, kd_scr, *, n_tiles, tile_rows):
    ti = pl.program_id(1)
    n_chunks = tile_rows // GDN_CHUNK
    lanes = [slice(h * HEAD_DIM, (h + 1) * HEAD_DIM) for h in range(HEADS)]

    @pl.when(ti == 0)
    def _():
        s_scr[...] = s0_ref[...]

    row = lax.broadcasted_iota(jnp.int32, (GDN_CHUNK, GDN_CHUNK), 0)
    col = lax.broadcasted_iota(jnp.int32, (GDN_CHUNK, GDN_CHUNK), 1)

    def same_block(size):
        return (row & -size) == (col & -size)

    strict = row > col
    eye = jnp.where(row == col, 1.0, 0.0)
    mask_scr[0] = jnp.where(row >= col, 1.0, 0.0)
    mask_scr[1] = jnp.where(strict & same_block(GDN_BASE), 1.0, 0.0)
    merge_sizes = []
    size = GDN_BASE
    while size < GDN_CHUNK:
        mask_scr[2 + len(merge_sizes)] = jnp.where(strict & same_block(2 * size) & ~same_block(size), 1.0, 0.0)
        merge_sizes.append(size)
        size *= 2
    gain = gn_ref[...]

    def prepare(c, carry):
        rows = pl.ds(pl.multiple_of(c * GDN_CHUNK, GDN_CHUNK), GDN_CHUNK)
        gc_by_head = gcrow_ref[pl.ds(pl.multiple_of(c * GDN_ROW_SLOTS + HEADS, HEADS), HEADS), :]
        heads = range(HEADS)
        m = []
        for h in heads:
            cs = lanes[h]
            k = k_ref[rows, cs]
            gc = gc_ref[rows, cs]
            gc_lanes = gc_by_head[h:h + 1, :]
            decay = mask_scr[0] * jnp.exp(jnp.minimum(gc - gc_lanes, 0.0))
            k16 = k.astype(BF16)
            m.append(_dot_nt((k * beta_ref[rows, cs]).astype(BF16), k16) * decay)
            qk_scr[rows, cs] = (_dot_nt((q_ref[rows, cs] * GDN_SCALE).astype(BF16), k16) * decay).astype(BF16)
        base = [m[h] * mask_scr[1] for h in heads]
        b16 = [base[h].astype(BF16) for h in heads]
        power = [_dot(b16[h], b16[h]).astype(BF16) for h in heads]
        inv = [eye - base[h] for h in heads]
        inv = [inv[h] + _dot(inv[h].astype(BF16), power[h]) for h in heads]
        for _ in range(int(math.log2(GDN_BASE)) - 2):
            power = [_dot(power[h], power[h]).astype(BF16) for h in heads]
            inv = [inv[h] + _dot(inv[h].astype(BF16), power[h]) for h in heads]
        for n in range(len(merge_sizes)):
            i16 = [inv[h].astype(BF16) for h in heads]
            half = [_dot(i16[h], (m[h] * mask_scr[2 + n]).astype(BF16)).astype(BF16) for h in heads]
            inv = [inv[h] - _dot(half[h], i16[h]) for h in heads]
        for h in heads:
            cs = lanes[h]
            k = k_ref[rows, cs]
            beta = beta_ref[rows, cs]
            gc = gc_ref[rows, cs]
            egc = jnp.exp(gc)
            rhs = jnp.concatenate([v_ref[rows, cs] * beta, k * beta * egc], axis=-1)
            sol = rhs + _dot((inv[h] - eye).astype(BF16), rhs.astype(BF16))
            u_scr[rows, cs] = sol[:, :HEAD_DIM]
            w_scr[rows, cs] = sol[:, HEAD_DIM:].astype(BF16)
            gc_last = jnp.broadcast_to(gc[GDN_CHUNK - 1:GDN_CHUNK, :], (GDN_CHUNK, HEAD_DIM))
            qd_scr[rows, cs] = (q_ref[rows, cs] * GDN_SCALE * egc).astype(BF16)
            kd_scr[rows, cs] = (k * jnp.exp(gc_last - gc)).astype(BF16)
        return carry

    lax.fori_loop(0, n_chunks, prepare, 0)

    def advance(c, carry):
        rows = pl.ds(pl.multiple_of(c * GDN_CHUNK, GDN_CHUNK), GDN_CHUNK)
        tail = pl.ds(pl.multiple_of(c * GDN_CHUNK + GDN_CHUNK - 8, 8), 8)
        heads = range(HEADS)
        state = [s_scr[h] for h in heads]
        s16 = [state[h].astype(BF16) for h in heads]
        vn16 = [(u_scr[rows, lanes[h]] - _dot(w_scr[rows, lanes[h]], s16[h])).astype(BF16) for h in heads]
        for h in heads:
            s_scr[h] = state[h] * jnp.exp(gc_ref[tail, lanes[h]][7:8, :]) + lax.dot_general(
                kd_scr[rows, lanes[h]], vn16[h], (((0,), (0,)), ((), ())), preferred_element_type=F32)
        for h in heads:
            cs = lanes[h]
            o = _dot(qd_scr[rows, cs], s16[h]) + _dot(qk_scr[rows, cs], vn16[h])
            zg = z_ref[rows, cs]
            o_ref[rows, cs] = (_rms(o, gain) * (zg * _sigmoid(zg))).astype(BF16)
        return carry

    lax.fori_loop(0, n_chunks, advance, 0)

    @pl.when(ti == n_tiles - 1)
    def _():
        sfin_ref[...] = s_scr[...]


def _gdn_chunks(qkv, beta, gc, gcrow, z_src, z_col_block, s0, out_norm, layer, n_seq, seq_rows):
    tile_rows = min(seq_rows, GDN_TS)
    n_tiles = seq_rows // tile_rows
    n_masks = 2 + int(math.log2(GDN_CHUNK // GDN_BASE))

    def act(col_block):
        return pl.BlockSpec((tile_rows, WIDTH), lambda b, t: (b * n_tiles + t, col_block))

    slots = tile_rows // GDN_CHUNK * GDN_ROW_SLOTS
    state_spec = pl.BlockSpec((None, HEADS, HEAD_DIM, HEAD_DIM), lambda b, t: (b, 0, 0, 0))
    return pl.pallas_call(
        functools.partial(_gdn_chunk_body, n_tiles=n_tiles, tile_rows=tile_rows),
        grid=(n_seq, n_tiles),
        in_specs=[act(0), act(1), act(2), act(0), act(0),
                  pl.BlockSpec((slots, GDN_CHUNK), lambda b, t: (b * n_tiles + t, 0)),
                  act(z_col_block), state_spec,
                  pl.BlockSpec((None, 1, HEAD_DIM), lambda b, t: (layer, 0, 0))],
        out_specs=[act(0), state_spec],
        out_shape=[jax.ShapeDtypeStruct((n_seq * seq_rows, WIDTH), BF16),
                   jax.ShapeDtypeStruct((n_seq, HEADS, HEAD_DIM, HEAD_DIM), F32)],
        scratch_shapes=[pltpu.VMEM((HEADS, HEAD_DIM, HEAD_DIM), F32),
                        pltpu.VMEM((n_masks, GDN_CHUNK, GDN_CHUNK), F32),
                        pltpu.VMEM((tile_rows, WIDTH), F32),
                        pltpu.VMEM((tile_rows, WIDTH), BF16),
                        pltpu.VMEM((tile_rows, WIDTH), BF16),
                        pltpu.VMEM((tile_rows, WIDTH), BF16),
                        pltpu.VMEM((tile_rows, WIDTH), BF16)],
        compiler_params=_params("parallel", "arbitrary"),
        name="gdn_chunks",
    )(qkv, qkv, qkv, beta, gc, gcrow, z_src, s0, out_norm)


def _pack_body(a_ref, b_ref, o_ref):
    n = pl.program_id(1)
    first_gate = C_GATE // PACK_TN
    last = PROJ_COLS // PACK_TN - 1
    shift = 2 * HEADS
    a = a_ref[...]
    shifted = jnp.concatenate([a[:, shift:], b_ref[:, :shift]], axis=1)
    lane = lax.broadcasted_iota(jnp.int32, a.shape, 1)
    logits = jnp.where(lane < shift, a, 0.0)
    o_ref[...] = jnp.where(n < first_gate, a, jnp.where(n == last, logits, shifted)).astype(BF16)


def _pack_w_in(w_in):
    first_gate = C_GATE // PACK_TN
    last = PROJ_COLS // PACK_TN - 1

    def a_index(d, n):
        return (d, 0, jnp.where(n == last, first_gate, n))

    def b_index(d, n):
        return (d, 0, jnp.where((n >= first_gate) & (n < last), n + 1, 0))

    return pl.pallas_call(
        _pack_body,
        grid=(DEPTH, PROJ_COLS // PACK_TN),
        in_specs=[pl.BlockSpec((None, D_MODEL, PACK_TN), a_index),
                  pl.BlockSpec((None, D_MODEL, PACK_TN), b_index)],
        out_specs=pl.BlockSpec((None, D_MODEL, PACK_TN), lambda d, n: (d, 0, n)),
        out_shape=jax.ShapeDtypeStruct((DEPTH, D_MODEL, PROJ_COLS), BF16),
        compiler_params=_params("parallel", "arbitrary"),
        name="pack_w_in",
    )(w_in, w_in)


def _s5_params(a_re, a_im, log_dt, b_re, b_im, c_re, c_im, d):
    eye = jnp.eye(S5_GB, dtype=F32)

    def state_rows(x):
        return x.reshape(DEPTH, S5_NGB, 1, S5_ST)

    par = jnp.concatenate(
        [state_rows(a_re), state_rows(a_im),
         state_rows(jnp.repeat(log_dt, SSM_STATE, axis=-1)),
         jnp.zeros((DEPTH, S5_NGB, 5, S5_ST), F32)], axis=2)

    def b_blockdiag(b):
        b = b.reshape(DEPTH, S5_NGB, S5_GB, SSM_STATE, SSM_GROUP)
        return jnp.einsum('dbgpc,gh->dbgchp', b, eye).reshape(DEPTH, S5_NGB, S5_CH, S5_ST)

    def c_blockdiag(c):
        c = c.reshape(DEPTH, S5_NGB, S5_GB, SSM_GROUP, SSM_STATE)
        return jnp.einsum('dbgcp,gh->dbgphc', c, eye).reshape(DEPTH, S5_NGB, S5_ST, S5_CH)

    return (par, b_blockdiag(b_re), b_blockdiag(b_im), c_blockdiag(c_re), c_blockdiag(c_im),
            d.reshape(DEPTH, S5_NGB, 1, S5_CH))


def _segment_major(x):
    w = x.shape[-1]
    return x.reshape(BATCH, S5_LANES, S5_SEG, w).transpose(0, 2, 1, 3).reshape(N_PROMPT, w)


def _token_major(x):
    w = x.shape[-1]
    return x.reshape(BATCH, S5_SEG, S5_LANES, w).transpose(0, 2, 1, 3).reshape(N_PROMPT, w)


def _pad_rows(x):
    return jnp.concatenate([x, jnp.zeros((ROWS - x.shape[0], x.shape[1]), x.dtype)], axis=0)


def _one_row_chunks(x):
    return jnp.pad(x[:, None, :], ((0, 0), (0, GDN_CHUNK - 1), (0, 0))).reshape(DEC_BATCH * GDN_CHUNK, x.shape[-1])


def kernel(x_prompt, x_sample, cache_k, cache_v, page_table, state_ssm, state_conv, state_gdn, ffn1_norm, ffn1_w_gate, ffn1_w_up, ffn1_w_down, mix_norm, w_in, sb_bias, ssm_a_re, ssm_a_im, ssm_log_dt, ssm_b_re, ssm_b_im, ssm_c_re, ssm_c_im, ssm_d, ssm_w_glu, gdn_conv_w, gdn_a_log, gdn_dt_bias, gdn_out_norm, w_branch, w_out, ffn2_norm, ffn2_w_gate, ffn2_w_up, ffn2_w_down, final_norm):
    bf = lambda w: w.astype(BF16)
    gain3 = lambda g: g.reshape(DEPTH, 1, D_MODEL)
    w_in_p = _pack_w_in(w_in)
    ffn1 = (gain3(ffn1_norm), bf(ffn1_w_gate), bf(ffn1_w_up), bf(ffn1_w_down))
    ffn2 = (gain3(ffn2_norm), bf(ffn2_w_gate), bf(ffn2_w_up), bf(ffn2_w_down))
    mix_gain = gain3(mix_norm)
    w_glu, w_br, w_o = bf(ssm_w_glu), bf(w_branch).reshape(DEPTH, 3, WIDTH, D_MODEL), bf(w_out)
    s5p = _s5_params(ssm_a_re, ssm_a_im, ssm_log_dt, ssm_b_re, ssm_b_im, ssm_c_re, ssm_c_im, ssm_d)
    bias_flat = sb_bias.reshape(DEPTH * HEADS)
    bias_lanes = jnp.tile(sb_bias, (1, PAGE)).reshape(DEPTH, 1, DEC_LANES)
    page_flat = page_table.reshape(DEC_BATCH * N_PAGES)
    conv_w8 = jnp.pad(gdn_conv_w, ((0, 0), (0, 8 - CONV_TAPS), (0, 0)))
    conv_hist = state_conv.transpose(0, 2, 1, 3)
    lane_pad = jnp.zeros((DEPTH, 128 - 2 * HEADS), F32)
    gate_par = jnp.stack([jnp.concatenate([jnp.zeros((DEPTH, HEADS), F32), gdn_a_log, lane_pad], axis=1),
                          jnp.concatenate([jnp.zeros((DEPTH, HEADS), F32), gdn_dt_bias, lane_pad], axis=1)]
                         + [jnp.zeros((DEPTH, 128), F32)] * 6, axis=1)
    out_norm = gdn_out_norm.reshape(DEPTH, 1, HEAD_DIM)
    ssm_h0r = state_ssm[..., 0].reshape(DEPTH, DEC_BATCH, WIDTH * SSM_STATE // SSM_GROUP)
    ssm_h0i = state_ssm[..., 1].reshape(DEPTH, DEC_BATCH, WIDTH * SSM_STATE // SSM_GROUP)

    x = _pad_rows(jnp.concatenate([x_prompt.reshape(N_PROMPT, D_MODEL),
                                   x_sample.reshape(DEC_BATCH, D_MODEL)], axis=0))
    sample = slice(N_PROMPT, N_PROMPT + DEC_BATCH)
    ks, vs, sp, ss, cp, cs_, gp, gs = ([] for _ in range(8))
    k_all = jnp.zeros((DEPTH, N_PROMPT, WIDTH), F32)
    v_all = jnp.zeros((DEPTH, N_PROMPT, WIDTH), F32)
    for l in range(DEPTH):
        x = _ffn(x, *ffn1, l)
        proj = _in_proj(x, mix_gain, w_in_p, l)

        oa_p, k_all, v_all = _sb_prompt(proj, bias_flat, l, k_all, v_all)
        q_s = proj[sample, C_Q:C_Q + WIDTH].reshape(DEC_BATCH, HEADS, HEAD_DIM)
        oa_s = _sb_decode(q_s, cache_k, cache_v, page_flat, bias_lanes, l)
        oa = _pad_rows(jnp.concatenate([oa_p, oa_s.reshape(DEC_BATCH, WIDTH).astype(BF16)], axis=0))

        gb_p, hr_p, hi_p = _s5_prompt(_segment_major(proj[:N_PROMPT, C_U:C_U + WIDTH]), s5p, l)
        gb_s, hr_s, hi_s = _s5_sample(proj, s5p, ssm_h0r[l], ssm_h0i[l], l)
        ob = _glu(_pad_rows(jnp.concatenate([_token_major(gb_p), gb_s], axis=0)), w_glu, l)

        qkv_p = _conv_prompt(proj, conv_w8, l)
        beta_p, gc_p, gcrow_p = _gdn_gates(proj, gate_par, l, GDN_TS, N_PROMPT // GDN_TS, 0, GDN_CHUNK)
        oc_p, gdn_p = _gdn_chunks(qkv_p, beta_p, gc_p, gcrow_p, proj, C_Z // WIDTH,
                                  jnp.zeros((BATCH, HEADS, HEAD_DIM, HEAD_DIM), F32), out_norm, l, BATCH, SEQ)
        qkv_s = _conv_sample(proj, conv_hist, conv_w8, l)
        beta_s, g_s = _gdn_gates(proj, gate_par, l, DEC_BATCH, 1, SAMPLE_ROW_BLOCK, 1)
        gc_s = jnp.broadcast_to(g_s[:, None, :], (DEC_BATCH, GDN_CHUNK, WIDTH)).reshape(DEC_BATCH * GDN_CHUNK, WIDTH)
        g_heads = jnp.broadcast_to(g_s[:, ::HEAD_DIM, None], (DEC_BATCH, HEADS, GDN_CHUNK))
        gcrow_s = jnp.concatenate([jnp.zeros_like(g_heads), g_heads], axis=1).reshape(
            DEC_BATCH * GDN_ROW_SLOTS, GDN_CHUNK)
        oc_s, gdn_s = _gdn_chunks(_one_row_chunks(qkv_s), _one_row_chunks(beta_s), gc_s, gcrow_s,
                                  _one_row_chunks(proj[sample, C_Z:C_Z + WIDTH]), 0,
                                  state_gdn[l], out_norm, l, DEC_BATCH, GDN_CHUNK)
        oc = _pad_rows(jnp.concatenate([oc_p, oc_s[::GDN_CHUNK]], axis=0))

        x = _merge(x, oa, ob, oc, proj, w_br, w_o, l)
        x = _ffn(x, *ffn2, l)

        ks.append(proj[sample, C_K:C_K + WIDTH].reshape(DEC_BATCH, 1, HEADS, HEAD_DIM))
        vs.append(proj[sample, C_V:C_V + WIDTH].reshape(DEC_BATCH, 1, HEADS, HEAD_DIM))
        state = lambda r, i, n: jnp.stack([r.reshape(n, WIDTH // SSM_GROUP, SSM_STATE),
                                           i.reshape(n, WIDTH // SSM_GROUP, SSM_STATE)], axis=-1)
        sp.append(state(hr_p, hi_p, BATCH))
        ss.append(state(hr_s, hi_s, DEC_BATCH))
        qkv_cols = slice(C_QKV, C_QKV + CONV_WIDTH)
        cp.append(jnp.stack([proj[(b + 1) * SEQ - (CONV_TAPS - 1):(b + 1) * SEQ, qkv_cols] for b in range(BATCH)]))
        cs_.append(jnp.concatenate([state_conv[l][:, 1:], proj[sample, qkv_cols][:, None, :]], axis=1))
        gp.append(gdn_p)
        gs.append(gdn_s)

    gain = final_norm.reshape(1, D_MODEL)
    y_p = _final_norm(x, gain, GDN_TS, N_PROMPT // GDN_TS, 0)
    y_s = _final_norm(x, gain, DEC_BATCH, 1, SAMPLE_ROW_BLOCK)
    kv_shape = (DEPTH, BATCH, SEQ, HEADS, HEAD_DIM)
    return (y_p.reshape(BATCH, SEQ, D_MODEL), y_s.reshape(DEC_BATCH, 1, D_MODEL),
            k_all.reshape(kv_shape), v_all.reshape(kv_shape), jnp.stack(ks), jnp.stack(vs),
            jnp.stack(sp), jnp.stack(ss), jnp.stack(cp), jnp.stack(cs_),
            jnp.stack(gp), jnp.stack(gs))
```

```python
import functools
import math

import jax
import jax.numpy as jnp
from jax import lax
from jax.experimental import pallas as pl
from jax.experimental.pallas import tpu as pltpu

F32 = jnp.float32
BF16 = jnp.bfloat16
HIGHEST = lax.Precision.HIGHEST

D_MODEL = 2048
BATCH = 4
SEQ = 2048
DEPTH = 2
DEC_BATCH = 32
PAGE = 128
N_PAGES = 64
HEADS = 8
HEAD_DIM = 128
WIDTH = HEADS * HEAD_DIM
SSM_GROUP = 16
SSM_STATE = 64
CONV_TAPS = 4
CONV_WIDTH = 3 * WIDTH
FFN_DIM = 5632
EPS = 1e-6

N_PROMPT = BATCH * SEQ
TM = 528
ROWS = 16 * TM
SAMPLE_ROW_BLOCK = N_PROMPT // DEC_BATCH

C_Q, C_K, C_V, C_U, C_QKV, C_Z, C_GATE, C_BA = 0, 1024, 2048, 3072, 4096, 7168, 8192, 14336
PROJ_COLS = 14592
TN_PROJ = 768
PACK_TN = 256
TF = 512
TM_BIG = 2 * TM
TM_MERGE = TM // 3

VMEM_LIMIT = 56 * 1024 * 1024


def _params(*sem):
    return pltpu.CompilerParams(dimension_semantics=sem, vmem_limit_bytes=VMEM_LIMIT)


def _rms(x, gain):
    ms = jnp.mean(x * x, axis=-1, keepdims=True)
    return x * lax.rsqrt(ms + EPS) * gain


def _softplus(x):
    return jnp.maximum(x, 0.0) + jnp.log(1.0 + jnp.exp(-jnp.abs(x)))


def _sigmoid(x):
    return 1.0 / (1.0 + jnp.exp(-x))


def _dot(a, b, **kw):
    return jnp.dot(a, b, preferred_element_type=F32, **kw)


def _dot_nt(a, b, **kw):
    return lax.dot_general(a, b, (((1,), (1,)), ((), ())), preferred_element_type=F32, **kw)


def _ffn_body(x_ref, g_ref, wg_ref, wu_ref, wd_ref, o_ref, h_scr):
    @pl.when(pl.program_id(1) == 0)
    def _():
        x = x_ref[...]
        h_scr[...] = _rms(x, g_ref[...]).astype(BF16)
        o_ref[...] = x

    h = h_scr[...]
    a = _dot(h, wg_ref[...])
    b = _dot(h, wu_ref[...])
    act = (a * _sigmoid(a)) * b * 0.5
    o_ref[...] += _dot(act.astype(BF16), wd_ref[...])


def _ffn(x, gain, wg, wu, wd, layer):
    return pl.pallas_call(
        _ffn_body,
        grid=(ROWS // TM, FFN_DIM // TF),
        in_specs=[
            pl.BlockSpec((TM, D_MODEL), lambda i, f: (i, 0)),
            pl.BlockSpec((None, 1, D_MODEL), lambda i, f: (layer, 0, 0)),
            pl.BlockSpec((None, D_MODEL, TF), lambda i, f: (layer, 0, f)),
            pl.BlockSpec((None, D_MODEL, TF), lambda i, f: (layer, 0, f)),
            pl.BlockSpec((None, TF, D_MODEL), lambda i, f: (layer, f, 0)),
        ],
        out_specs=pl.BlockSpec((TM, D_MODEL), lambda i, f: (i, 0)),
        out_shape=jax.ShapeDtypeStruct((ROWS, D_MODEL), F32),
        scratch_shapes=[pltpu.VMEM((TM, D_MODEL), BF16)],
        compiler_params=_params("parallel", "arbitrary"),
        name="ffn",
    )(x, gain, wg, wu, wd)


def _proj_body(x_ref, g_ref, w_ref, o_ref, h_scr):
    @pl.when(pl.program_id(1) == 0)
    def _():
        h_scr[...] = _rms(x_ref[...], g_ref[...]).astype(BF16)

    o_ref[...] = _dot(h_scr[...], w_ref[...])


def _in_proj(x, gain, w, layer):
    return pl.pallas_call(
        _proj_body,
        grid=(ROWS // TM_BIG, PROJ_COLS // TN_PROJ),
        in_specs=[
            pl.BlockSpec((TM_BIG, D_MODEL), lambda i, n: (i, 0), pipeline_mode=pl.Buffered(1)),
            pl.BlockSpec((None, 1, D_MODEL), lambda i, n: (layer, 0, 0)),
            pl.BlockSpec((None, D_MODEL, TN_PROJ), lambda i, n: (layer, 0, n)),
        ],
        out_specs=pl.BlockSpec((TM_BIG, TN_PROJ), lambda i, n: (i, n)),
        out_shape=jax.ShapeDtypeStruct((ROWS, PROJ_COLS), F32),
        scratch_shapes=[pltpu.VMEM((TM_BIG, D_MODEL), BF16)],
        compiler_params=_params("parallel", "arbitrary"),
        name="in_proj",
    )(x, gain, w)


def _glu_body(g_ref, w_ref, o_ref):
    g = g_ref[...]
    o_ref[...] = (g * _sigmoid(_dot(g.astype(BF16), w_ref[...]))).astype(BF16)


def _glu(g, w, layer):
    return pl.pallas_call(
        _glu_body,
        grid=(ROWS // TM,),
        in_specs=[
            pl.BlockSpec((TM, WIDTH), lambda i: (i, 0)),
            pl.BlockSpec((None, WIDTH, WIDTH), lambda i: (layer, 0, 0)),
        ],
        out_specs=pl.BlockSpec((TM, WIDTH), lambda i: (i, 0)),
        out_shape=jax.ShapeDtypeStruct((ROWS, WIDTH), BF16),
        compiler_params=_params("parallel"),
        name="glu",
    )(g, w)


def _merge_body(x_ref, oa_ref, ob_ref, oc_ref, ga_ref, gb_ref, gc_ref, wa_ref, wb_ref, wc_ref, wo_ref, o_ref):
    m = (_sigmoid(ga_ref[...]) * _dot(oa_ref[...], wa_ref[...])
         + _sigmoid(gb_ref[...]) * _dot(ob_ref[...], wb_ref[...])
         + _sigmoid(gc_ref[...]) * _dot(oc_ref[...], wc_ref[...]))
    o_ref[...] = x_ref[...] + _dot(m.astype(BF16), wo_ref[...])


def _merge(x, oa, ob, oc, proj, w_branch, w_out, layer):
    gate_blk = C_GATE // D_MODEL
    o_spec = pl.BlockSpec((TM_MERGE, WIDTH), lambda i: (i, 0))

    def gate_spec(which):
        return pl.BlockSpec((TM_MERGE, D_MODEL), lambda i: (i, gate_blk + which))

    def w_spec(which):
        return pl.BlockSpec((None, None, WIDTH, D_MODEL), lambda i: (layer, which, 0, 0),
                            pipeline_mode=pl.Buffered(1))

    return pl.pallas_call(
        _merge_body,
        grid=(ROWS // TM_MERGE,),
        in_specs=[
            pl.BlockSpec((TM_MERGE, D_MODEL), lambda i: (i, 0)),
            o_spec, o_spec, o_spec,
            gate_spec(0), gate_spec(1), gate_spec(2),
            w_spec(0), w_spec(1), w_spec(2),
            pl.BlockSpec((None, D_MODEL, D_MODEL), lambda i: (layer, 0, 0), pipeline_mode=pl.Buffered(1)),
        ],
        out_specs=pl.BlockSpec((TM_MERGE, D_MODEL), lambda i: (i, 0)),
        out_shape=jax.ShapeDtypeStruct((ROWS, D_MODEL), F32),
        compiler_params=_params("parallel"),
        name="merge",
    )(x, oa, ob, oc, proj, proj, proj, w_branch, w_branch, w_branch, w_out)


def _final_norm_body(x_ref, g_ref, o_ref):
    o_ref[...] = _rms(x_ref[...], g_ref[...])


def _final_norm(x, gain, rows, n_blocks, first_block):
    return pl.pallas_call(
        _final_norm_body,
        grid=(n_blocks,),
        in_specs=[pl.BlockSpec((rows, D_MODEL), lambda i: (first_block + i, 0)),
                  pl.BlockSpec((1, D_MODEL), lambda i: (0, 0))],
        out_specs=pl.BlockSpec((rows, D_MODEL), lambda i: (i, 0)),
        out_shape=jax.ShapeDtypeStruct((rows * n_blocks, D_MODEL), F32),
        compiler_params=_params("parallel"),
        name="final_norm",
    )(x, gain)


SB_BQ = 256
SB_BK = 256
SB_HB = 4
SB_SCALE = HEAD_DIM ** -0.5


def _stick_terms(z):
    t = jnp.log(1.0 + jnp.exp(-jnp.abs(z)))
    return jnp.minimum(z, 0.0) - t, jnp.minimum(-z, 0.0) - t


LOG2E = 1.0 / math.log(2.0)


def _sb_prompt_body(bias_ref, q_ref, k_ref, v_ref, *rest, layer):
    o_ref, k_all_ref, v_all_ref, q_scr, acc_scr, run_scr = rest[2:]
    hg = pl.program_id(1)
    qi = pl.program_id(2)

    @pl.when(qi == 0)
    def _():
        k_all_ref[...] = k_ref[...]
        v_all_ref[...] = v_ref[...]

    heads = range(SB_HB)
    lanes = [slice(hh * HEAD_DIM, (hh + 1) * HEAD_DIM) for hh in heads]
    row = lax.broadcasted_iota(jnp.int32, (SB_BQ, SB_BK), 0)
    col = lax.broadcasted_iota(jnp.int32, (SB_BQ, SB_BK), 1)
    q_pos = qi * SB_BQ + row
    r2 = lax.broadcasted_iota(jnp.int32, (SB_BK, SB_BK), 0)
    c2 = lax.broadcasted_iota(jnp.int32, (SB_BK, SB_BK), 1)
    later_key = jnp.where(r2 > c2, 1.0, 0.0).astype(BF16)
    bias2 = [bias_ref[layer * HEADS + hg * SB_HB + hh] * LOG2E for hh in heads]

    for hh in heads:
        q_scr[hh] = (q_ref[:, lanes[hh]] * (SB_SCALE * LOG2E)).astype(BF16)
    acc_scr[...] = jnp.zeros_like(acc_scr)
    run_scr[...] = jnp.zeros_like(run_scr)

    def visit(j, masked):
        k_rows = pl.ds(pl.multiple_of(j * SB_BK, SB_BK), SB_BK)
        mask = (j * SB_BK + col) < q_pos
        z = [_dot_nt(q_scr[hh], k_ref[k_rows, lanes[hh]].astype(BF16)) + bias2[hh] for hh in heads]
        t = [jnp.log(1.0 + jnp.exp2(-jnp.abs(z[hh]))) * LOG2E for hh in heads]
        log_hit = [jnp.minimum(z[hh], 0.0) - t[hh] for hh in heads]
        log_fail = [jnp.minimum(-z[hh], 0.0) - t[hh] for hh in heads]
        if masked:
            log_fail = [jnp.where(mask, log_fail[hh], 0.0) for hh in heads]
        suffix = [_dot(log_fail[hh].astype(BF16), later_key) for hh in heads]
        w = []
        for hh in heads:
            run = run_scr[hh]
            w_h = jnp.exp2(log_hit[hh] + suffix[hh] + run)
            block_total = suffix[hh][:, 0:1] + log_fail[hh][:, 0:1]
            run_scr[hh] = run + jnp.broadcast_to(block_total, (SB_BQ, SB_BK))
            w.append(jnp.where(mask, w_h, 0.0) if masked else w_h)
        for hh in heads:
            acc_scr[hh] += _dot(w[hh].astype(BF16), v_ref[k_rows, lanes[hh]].astype(BF16))

    visit(qi, True)

    def below_diagonal(it, carry):
        visit(qi - 1 - it, False)
        return carry

    lax.fori_loop(0, qi, below_diagonal, 0)
    for hh in heads:
        o_ref[:, lanes[hh]] = acc_scr[hh].astype(BF16)


def _sb_prompt(proj, bias_flat, layer, k_all, v_all):
    hw = SB_HB * HEAD_DIM
    n_hg = HEADS // SB_HB
    nq = SEQ // SB_BQ
    all_spec = pl.BlockSpec((None, SEQ, hw), lambda b, h, i, bias: (layer, b, h))
    all_shape = jax.ShapeDtypeStruct((DEPTH, N_PROMPT, WIDTH), F32)
    grid_spec = pltpu.PrefetchScalarGridSpec(
        num_scalar_prefetch=1,
        grid=(BATCH, n_hg, nq),
        in_specs=[
            pl.BlockSpec((SB_BQ, hw), lambda b, h, i, bias: (b * nq + i, C_Q // hw + h)),
            pl.BlockSpec((SEQ, hw), lambda b, h, i, bias: (b, C_K // hw + h)),
            pl.BlockSpec((SEQ, hw), lambda b, h, i, bias: (b, C_V // hw + h)),
            pl.BlockSpec(memory_space=pl.ANY),
            pl.BlockSpec(memory_space=pl.ANY),
        ],
        out_specs=[pl.BlockSpec((SB_BQ, hw), lambda b, h, i, bias: (b * nq + i, h)), all_spec, all_spec],
        scratch_shapes=[pltpu.VMEM((SB_HB, SB_BQ, HEAD_DIM), BF16),
                        pltpu.VMEM((SB_HB, SB_BQ, HEAD_DIM), F32),
                        pltpu.VMEM((SB_HB, SB_BQ, SB_BK), F32)],
    )
    return pl.pallas_call(
        functools.partial(_sb_prompt_body, layer=layer),
        grid_spec=grid_spec,
        out_shape=[jax.ShapeDtypeStruct((N_PROMPT, WIDTH), BF16), all_shape, all_shape],
        input_output_aliases={4: 1, 5: 2},
        compiler_params=_params("parallel", "parallel", "arbitrary"),
        name="sb_prompt",
    )(bias_flat, proj, proj, proj, k_all, v_all)


DEC_PAGES = 16
DEC_STEPS = N_PAGES // DEC_PAGES
DEC_LANES = PAGE * HEADS


def _lane_shift_up(x, s):
    n = x.shape[1]
    lane = lax.broadcasted_iota(jnp.int32, x.shape, 1)
    return jnp.where(lane < n - s, pltpu.roll(x, n - s, 1), 0.0)


def _sb_decode_body(pt_ref, q_ref, bias_ref, *refs):
    k_refs = refs[:DEC_PAGES]
    v_refs = refs[DEC_PAGES:2 * DEC_PAGES]
    o_ref, acc_scr, run_scr = refs[2 * DEC_PAGES:]
    g = pl.program_id(1)

    @pl.when(g == 0)
    def _():
        acc_scr[...] = jnp.zeros_like(acc_scr)
        run_scr[...] = jnp.zeros_like(run_scr)

    sub = lax.broadcasted_iota(jnp.int32, (HEADS, DEC_LANES), 0)
    lane = lax.broadcasted_iota(jnp.int32, (HEADS, DEC_LANES), 1)
    own_head = (lane & (HEADS - 1)) == sub
    q = q_ref[...].astype(BF16)
    rows = []
    for i in range(DEC_PAGES):
        k2 = k_refs[i][...].reshape(DEC_LANES, HEAD_DIM).astype(BF16)
        zt = _dot_nt(q, k2)
        rows.append(jnp.sum(jnp.where(own_head, zt, 0.0), axis=0, keepdims=True))
    z = jnp.concatenate(rows, axis=0) * SB_SCALE + bias_ref[...]
    log_hit, log_fail = _stick_terms(z)

    suffix = log_fail
    for s in (HEADS, 4 * HEADS):
        suffix = (suffix + _lane_shift_up(suffix, s)) + (_lane_shift_up(suffix, 2 * s) + _lane_shift_up(suffix, 3 * s))
    s = 16 * HEADS
    while s < DEC_LANES:
        suffix = suffix + jnp.concatenate([suffix[:, s:], jnp.zeros((DEC_PAGES, s), F32)], axis=1)
        s *= 2
    tile_lane = lax.broadcasted_iota(jnp.int32, (DEC_PAGES, 16 * HEADS), 1)
    tot = jnp.where(tile_lane < HEADS, suffix[:, :16 * HEADS], 0.0)
    for s in (HEADS, 4 * HEADS):
        tot = (tot + pltpu.roll(tot, s, 1)) + (pltpu.roll(tot, 2 * s, 1) + pltpu.roll(tot, 3 * s, 1))
    tot = jnp.concatenate([tot] * (DEC_LANES // (16 * HEADS)), axis=1)
    page_r = lax.broadcasted_iota(jnp.int32, (DEC_PAGES, DEC_PAGES), 0)
    page_c = lax.broadcasted_iota(jnp.int32, (DEC_PAGES, DEC_PAGES), 1)
    later = jnp.where(page_c > page_r, 1.0, 0.0)
    after = _dot(later, tot, precision=HIGHEST)
    run = run_scr[...]
    log_between = (suffix - log_fail) + after + run
    run_scr[...] = run + jnp.sum(tot, axis=0, keepdims=True)
    w = jnp.exp(log_hit + log_between)

    acc = acc_scr[...]
    for i in range(DEC_PAGES):
        wm = jnp.where(own_head, jnp.broadcast_to(w[i:i + 1, :], (HEADS, DEC_LANES)), 0.0).astype(BF16)
        v2 = v_refs[i][...].reshape(DEC_LANES, HEAD_DIM).astype(BF16)
        acc = acc + _dot(wm, v2)
    acc_scr[...] = acc

    @pl.when(g == DEC_STEPS - 1)
    def _():
        o_ref[...] = acc


def _sb_decode(q, cache_k, cache_v, page_table_flat, bias_lanes, layer):
    def page_spec(i):
        def index(b, g, pt):
            return (layer, pt[b * N_PAGES + (DEC_STEPS - 1 - g) * DEC_PAGES + i], 0, 0, 0)
        return pl.BlockSpec((None, None, PAGE, HEADS, HEAD_DIM), index)

    grid_spec = pltpu.PrefetchScalarGridSpec(
        num_scalar_prefetch=1,
        grid=(DEC_BATCH, DEC_STEPS),
        in_specs=[pl.BlockSpec((None, HEADS, HEAD_DIM), lambda b, g, pt: (b, 0, 0)),
                  pl.BlockSpec((None, 1, DEC_LANES), lambda b, g, pt: (layer, 0, 0))]
        + [page_spec(i) for i in range(DEC_PAGES)] * 2,
        out_specs=pl.BlockSpec((None, HEADS, HEAD_DIM), lambda b, g, pt: (b, 0, 0)),
        scratch_shapes=[pltpu.VMEM((HEADS, HEAD_DIM), F32), pltpu.VMEM((1, DEC_LANES), F32)],
    )
    return pl.pallas_call(
        _sb_decode_body,
        grid_spec=grid_spec,
        out_shape=jax.ShapeDtypeStruct((DEC_BATCH, HEADS, HEAD_DIM), F32),
        compiler_params=_params("parallel", "arbitrary"),
        name="sb_decode",
    )(page_table_flat, q, bias_lanes, *([cache_k] * DEC_PAGES), *([cache_v] * DEC_PAGES))


S5_GB = 8
S5_CH = S5_GB * SSM_GROUP
S5_ST = S5_GB * SSM_STATE
S5_NGB = WIDTH // S5_CH
S5_LANES = 8
S5_SEG = SEQ // S5_LANES


def _s5_discretize(par):
    ar, ai, ldt = par[0:1], par[1:2], par[2:3]
    dt = jnp.exp(ldt)
    mag = jnp.exp(dt * ar)
    abr = mag * jnp.cos(dt * ai)
    abi = mag * jnp.sin(dt * ai)
    den = ar * ar + ai * ai
    cr = ((abr - 1.0) * ar + abi * ai) / den
    ci = (abi * ar - (abr - 1.0) * ai) / den
    return abr, abi, cr, ci


def _gelu(y):
    return 0.5 * y * (1.0 + jnp.tanh(math.sqrt(2.0 / math.pi) * (y + 0.044715 * (y * y * y))))


def _s5_prompt_body(u_ref, par_ref, bre_ref, bim_ref, cre_ref, cim_ref, d_ref, h0r_ref, h0i_ref,
                    o_ref, hr_ref, hi_ref, xr_scr, xi_scr):
    abr, abi, cr, ci = _s5_discretize(par_ref[...])
    bre, bim = bre_ref[...], bim_ref[...]
    bbr = (cr * bre - ci * bim).astype(BF16)
    bbi = (cr * bim + ci * bre).astype(BF16)
    u = u_ref[...]
    ub = u.astype(BF16)
    xr_scr[...] = _dot(ub, bbr)
    xi_scr[...] = _dot(ub, bbi)

    shape = (S5_LANES, S5_ST)
    ar8 = jnp.broadcast_to(abr, shape)
    ai8 = jnp.broadcast_to(abi, shape)

    def local_step(t, carry):
        xr, xi = carry
        rows = pl.ds(pl.multiple_of(t * S5_LANES, S5_LANES), S5_LANES)
        nr = ar8 * xr - ai8 * xi + xr_scr[rows, :]
        ni = ar8 * xi + ai8 * xr + xi_scr[rows, :]
        xr_scr[rows, :] = nr
        xi_scr[rows, :] = ni
        return nr, ni

    zero = jnp.zeros(shape, F32)
    fr, fi = lax.fori_loop(0, S5_SEG, local_step, (zero, zero), unroll=8)

    pr, pi = abr, abi
    for _ in range(int(math.log2(S5_SEG))):
        pr, pi = pr * pr - pi * pi, 2.0 * pr * pi
    hr, hi = h0r_ref[...], h0i_ref[...]
    in_r, in_i = [], []
    for j in range(S5_LANES):
        in_r.append(hr)
        in_i.append(hi)
        hr, hi = fr[j:j + 1] + pr * hr - pi * hi, fi[j:j + 1] + pr * hi + pi * hr
    hr_ref[...] = hr
    hi_ref[...] = hi
    cin_r = jnp.concatenate(in_r, axis=0)
    cin_i = jnp.concatenate(in_i, axis=0)

    def fix_step(t, carry):
        wr, wi = carry
        rows = pl.ds(pl.multiple_of(t * S5_LANES, S5_LANES), S5_LANES)
        xr_scr[rows, :] = xr_scr[rows, :] + wr * cin_r - wi * cin_i
        xi_scr[rows, :] = xi_scr[rows, :] + wr * cin_i + wi * cin_r
        return wr * ar8 - wi * ai8, wr * ai8 + wi * ar8

    lax.fori_loop(0, S5_SEG, fix_step, (ar8, ai8), unroll=8)

    y = (_dot(xr_scr[...].astype(BF16), cre_ref[...].astype(BF16))
         - _dot(xi_scr[...].astype(BF16), cim_ref[...].astype(BF16)) + d_ref[...] * u)
    o_ref[...] = _gelu(y)


def _s5_param_specs(layer):
    lead = (lambda *a: a[-1])

    def spec(shape):
        return pl.BlockSpec((None, None) + shape, lambda *a: (layer, lead(*a), 0, 0))

    return [spec((8, S5_ST)), spec((S5_CH, S5_ST)), spec((S5_CH, S5_ST)),
            spec((S5_ST, S5_CH)), spec((S5_ST, S5_CH)), spec((1, S5_CH))]


def _s5_prompt(u_perm, s5p, layer):
    zeros = jnp.zeros((BATCH, S5_NGB, 1, S5_ST), F32)
    state_spec = pl.BlockSpec((None, None, 1, S5_ST), lambda b, g: (b, g, 0, 0))
    out, hr, hi = pl.pallas_call(
        _s5_prompt_body,
        grid=(BATCH, S5_NGB),
        in_specs=[pl.BlockSpec((SEQ, S5_CH), lambda b, g: (b, g))] + _s5_param_specs(layer)
        + [state_spec, state_spec],
        out_specs=[pl.BlockSpec((SEQ, S5_CH), lambda b, g: (b, g)), state_spec, state_spec],
        out_shape=[jax.ShapeDtypeStruct((N_PROMPT, WIDTH), F32),
                   jax.ShapeDtypeStruct((BATCH, S5_NGB, 1, S5_ST), F32),
                   jax.ShapeDtypeStruct((BATCH, S5_NGB, 1, S5_ST), F32)],
        scratch_shapes=[pltpu.VMEM((SEQ, S5_ST), F32), pltpu.VMEM((SEQ, S5_ST), F32)],
        compiler_params=_params("parallel", "parallel"),
        name="s5_prompt",
    )(u_perm, *s5p, zeros, zeros)
    return out, hr, hi


def _s5_sample_body(u_ref, par_ref, bre_ref, bim_ref, cre_ref, cim_ref, d_ref, h0r_ref, h0i_ref,
                    o_ref, hr_ref, hi_ref):
    abr, abi, cr, ci = _s5_discretize(par_ref[...])
    bre, bim = bre_ref[...], bim_ref[...]
    bbr = (cr * bre - ci * bim).astype(BF16)
    bbi = (cr * bim + ci * bre).astype(BF16)
    u = u_ref[...]
    ub = u.astype(BF16)
    h0r, h0i = h0r_ref[...], h0i_ref[...]
    xr = _dot(ub, bbr) + abr * h0r - abi * h0i
    xi = _dot(ub, bbi) + abr * h0i + abi * h0r
    hr_ref[...] = xr
    hi_ref[...] = xi
    y = (_dot(xr.astype(BF16), cre_ref[...].astype(BF16))
         - _dot(xi.astype(BF16), cim_ref[...].astype(BF16)) + d_ref[...] * u)
    o_ref[...] = _gelu(y)


def _s5_sample(proj, s5p, h0r, h0i, layer):
    state_spec = pl.BlockSpec((DEC_BATCH, S5_ST), lambda g: (0, g))
    return pl.pallas_call(
        _s5_sample_body,
        grid=(S5_NGB,),
        in_specs=[pl.BlockSpec((DEC_BATCH, S5_CH), lambda g: (SAMPLE_ROW_BLOCK, C_U // S5_CH + g))]
        + _s5_param_specs(layer) + [state_spec, state_spec],
        out_specs=[pl.BlockSpec((DEC_BATCH, S5_CH), lambda g: (0, g)), state_spec, state_spec],
        out_shape=[jax.ShapeDtypeStruct((DEC_BATCH, WIDTH), F32),
                   jax.ShapeDtypeStruct((DEC_BATCH, S5_NGB * S5_ST), F32),
                   jax.ShapeDtypeStruct((DEC_BATCH, S5_NGB * S5_ST), F32)],
        compiler_params=_params("parallel"),
        name="s5_sample",
    )(proj, *s5p, h0r, h0i)


GDN_TS = 512
GDN_SCALE = HEAD_DIM ** -0.5


def _l2norm_heads(xc, o_ref):
    for h in range(HEADS):
        cs = slice(h * HEAD_DIM, (h + 1) * HEAD_DIM)
        seg = xc[:, cs]
        o_ref[:, cs] = seg * lax.rsqrt(jnp.sum(seg * seg, axis=-1, keepdims=True) + EPS)


def _conv_prompt_body(x_ref, prev_ref, w_ref, o_ref, ext_scr):
    i = pl.program_id(1)
    part = pl.program_id(2)
    x = x_ref[...]
    ext_scr[0:8, :] = jnp.where(i == 0, 0.0, prev_ref[...])
    ext_scr[8:8 + GDN_TS, :] = x
    w = w_ref[...]
    acc = x * w[3:4]
    for tap in range(CONV_TAPS - 1):
        back = CONV_TAPS - 1 - tap
        acc = acc + ext_scr[8 - back:8 - back + GDN_TS, :] * w[tap:tap + 1]
    xc = acc * _sigmoid(acc)

    @pl.when(part < 2)
    def _():
        _l2norm_heads(xc, o_ref)

    @pl.when(part == 2)
    def _():
        o_ref[...] = xc


def _conv_prompt(proj, conv_w, layer):
    n_t = SEQ // GDN_TS
    col0 = C_QKV // WIDTH

    def prev_index(b, i, p):
        return (jnp.maximum(b * (SEQ // 8) + i * (GDN_TS // 8) - 1, 0), col0 + p)

    return pl.pallas_call(
        _conv_prompt_body,
        grid=(BATCH, n_t, 3),
        in_specs=[
            pl.BlockSpec((GDN_TS, WIDTH), lambda b, i, p: (b * n_t + i, col0 + p)),
            pl.BlockSpec((8, WIDTH), prev_index),
            pl.BlockSpec((None, 8, WIDTH), lambda b, i, p: (layer, 0, p)),
        ],
        out_specs=pl.BlockSpec((GDN_TS, WIDTH), lambda b, i, p: (b * n_t + i, p)),
        out_shape=jax.ShapeDtypeStruct((N_PROMPT, CONV_WIDTH), F32),
        scratch_shapes=[pltpu.VMEM((GDN_TS + 8, WIDTH), F32)],
        compiler_params=_params("parallel", "parallel", "parallel"),
        name="gdn_conv_prompt",
    )(proj, proj, conv_w)


def _conv_sample_body(x_ref, hist_ref, w_ref, o_ref):
    part = pl.program_id(0)
    w = w_ref[...]
    acc = x_ref[...] * w[3:4]
    for tap in range(CONV_TAPS - 1):
        acc = acc + hist_ref[tap] * w[tap:tap + 1]
    xc = acc * _sigmoid(acc)

    @pl.when(part < 2)
    def _():
        _l2norm_heads(xc, o_ref)

    @pl.when(part == 2)
    def _():
        o_ref[...] = xc


def _conv_sample(proj, hist, conv_w, layer):
    col0 = C_QKV // WIDTH
    return pl.pallas_call(
        _conv_sample_body,
        grid=(3,),
        in_specs=[
            pl.BlockSpec((DEC_BATCH, WIDTH), lambda p: (SAMPLE_ROW_BLOCK, col0 + p)),
            pl.BlockSpec((None, CONV_TAPS - 1, DEC_BATCH, WIDTH), lambda p: (layer, 0, 0, p)),
            pl.BlockSpec((None, 8, WIDTH), lambda p: (layer, 0, p)),
        ],
        out_specs=pl.BlockSpec((DEC_BATCH, WIDTH), lambda p: (0, p)),
        out_shape=jax.ShapeDtypeStruct((DEC_BATCH, CONV_WIDTH), F32),
        compiler_params=_params("parallel"),
        name="gdn_conv_sample",
    )(proj, hist, conv_w)


GDN_CHUNK = 128
GDN_ROW_SLOTS = 16


def _gdn_gates_body(ba_ref, par_ref, beta_ref, gc_ref, gcrow_ref, *, rows, chunk):
    ba = ba_ref[...]
    par = par_ref[...]
    beta = _sigmoid(ba)
    g = -jnp.exp(par[0:1]) * _softplus(ba + par[1:2])
    if chunk > 1:
        r = lax.broadcasted_iota(jnp.int32, (rows, rows), 0)
        c = lax.broadcasted_iota(jnp.int32, (rows, rows), 1)
        tri = jnp.where((r >= c) & ((r & -chunk) == (c & -chunk)), 1.0, 0.0)
        g = _dot(tri, g, precision=HIGHEST)
        g_t = g.T
        for cc in range(rows // chunk):
            gcrow_ref[cc * GDN_ROW_SLOTS:(cc + 1) * GDN_ROW_SLOTS, :] = (
                g_t[0:GDN_ROW_SLOTS, cc * chunk:(cc + 1) * chunk])
    for h in range(HEADS):
        cs = slice(h * HEAD_DIM, (h + 1) * HEAD_DIM)
        beta_ref[:, cs] = jnp.broadcast_to(beta[:, h:h + 1], (rows, HEAD_DIM))
        gc_ref[:, cs] = jnp.broadcast_to(g[:, HEADS + h:HEADS + h + 1], (rows, HEAD_DIM))


def _gdn_gates(proj, gate_par, layer, rows, n_blocks, first_block, chunk):
    out = jax.ShapeDtypeStruct((rows * n_blocks, WIDTH), F32)
    out_specs = [pl.BlockSpec((rows, WIDTH), lambda i: (i, 0))] * 2
    out_shape = [out, out]
    if chunk > 1:
        slots = rows // chunk * GDN_ROW_SLOTS
        out_specs.append(pl.BlockSpec((slots, chunk), lambda i: (i, 0)))
        out_shape.append(jax.ShapeDtypeStruct((slots * n_blocks, chunk), F32))
        body = functools.partial(_gdn_gates_body, rows=rows, chunk=chunk)
    else:
        body = functools.partial(_gdn_gates_body, gcrow_ref=None, rows=rows, chunk=chunk)
    return pl.pallas_call(
        body,
        grid=(n_blocks,),
        in_specs=[pl.BlockSpec((rows, 128), lambda i: (first_block + i, C_BA // 128)),
                  pl.BlockSpec((None, 8, 128), lambda i: (layer, 0, 0))],
        out_specs=out_specs,
        out_shape=out_shape,
        compiler_params=_params("parallel"),
        name="gdn_gates",
    )(proj, gate_par)


GDN_BASE = 8


def _gdn_chunk_body(q_ref, k_ref, v_ref, beta_ref, gc_ref, gcrow_ref, z_ref, s0_ref, gn_ref, o_ref, sfin_ref,
                    s_scr, mask_scr, u_scr, w_scr, qk_scr, qd_scr, kd_scr, *, n_tiles, tile_rows):
    ti = pl.program_id(1)
    n_chunks = tile_rows // GDN_CHUNK
    lanes = [slice(h * HEAD_DIM, (h + 1) * HEAD_DIM) for h in range(HEADS)]

    @pl.when(ti == 0)
    def _():
        s_scr[...] = s0_ref[...]

    row = lax.broadcasted_iota(jnp.int32, (GDN_CHUNK, GDN_CHUNK), 0)
    col = lax.broadcasted_iota(jnp.int32, (GDN_CHUNK, GDN_CHUNK), 1)

    def same_block(size):
        return (row & -size) == (col & -size)

    strict = row > col
    eye = jnp.where(row == col, 1.0, 0.0)
    mask_scr[0] = jnp.where(row >= col, 1.0, 0.0)
    mask_scr[1] = jnp.where(strict & same_block(GDN_BASE), 1.0, 0.0)
    merge_sizes = []
    size = GDN_BASE
    while size < GDN_CHUNK:
        mask_scr[2 + len(merge_sizes)] = jnp.where(strict & same_block(2 * size) & ~same_block(size), 1.0, 0.0)
        merge_sizes.append(size)
        size *= 2
    gain = gn_ref[...]

    def prepare(c, carry):
        rows = pl.ds(pl.multiple_of(c * GDN_CHUNK, GDN_CHUNK), GDN_CHUNK)
        gc_by_head = gcrow_ref[pl.ds(pl.multiple_of(c * GDN_ROW_SLOTS + HEADS, HEADS), HEADS), :]
        heads = range(HEADS)
        m = []
        for h in heads:
            cs = lanes[h]
            k = k_ref[rows, cs]
            gc = gc_ref[rows, cs]
            gc_lanes = gc_by_head[h:h + 1, :]
            decay = mask_scr[0] * jnp.exp(jnp.minimum(gc - gc_lanes, 0.0))
            k16 = k.astype(BF16)
            m.append(_dot_nt((k * beta_ref[rows, cs]).astype(BF16), k16) * decay)
            qk_scr[rows, cs] = (_dot_nt((q_ref[rows, cs] * GDN_SCALE).astype(BF16), k16) * decay).astype(BF16)
        base = [m[h] * mask_scr[1] for h in heads]
        b16 = [base[h].astype(BF16) for h in heads]
        power = [_dot(b16[h], b16[h]).astype(BF16) for h in heads]
        inv = [eye - base[h] for h in heads]
        inv = [inv[h] + _dot(inv[h].astype(BF16), power[h]) for h in heads]
        for _ in range(int(math.log2(GDN_BASE)) - 2):
            power = [_dot(power[h], power[h]).astype(BF16) for h in heads]
            inv = [inv[h] + _dot(inv[h].astype(BF16), power[h]) for h in heads]
        for n in range(len(merge_sizes)):
            i16 = [inv[h].astype(BF16) for h in heads]
            half = [_dot(i16[h], (m[h] * mask_scr[2 + n]).astype(BF16)).astype(BF16) for h in heads]
            inv = [inv[h] - _dot(half[h], i16[h]) for h in heads]
        for h in heads:
            cs = lanes[h]
            k = k_ref[rows, cs]
            beta = beta_ref[rows, cs]
            gc = gc_ref[rows, cs]
            egc = jnp.exp(gc)
            rhs = jnp.concatenate([v_ref[rows, cs] * beta, k * beta * egc], axis=-1)
            sol = rhs + _dot((inv[h] - eye).astype(BF16), rhs.astype(BF16))
            u_scr[rows, cs] = sol[:, :HEAD_DIM]
            w_scr[rows, cs] = sol[:, HEAD_DIM:].astype(BF16)
            gc_last = jnp.broadcast_to(gc[GDN_CHUNK - 1:GDN_CHUNK, :], (GDN_CHUNK, HEAD_DIM))
            qd_scr[rows, cs] = (q_ref[rows, cs] * GDN_SCALE * egc).astype(BF16)
            kd_scr[rows, cs] = (k * jnp.exp(gc_last - gc)).astype(BF16)
        return carry

    lax.fori_loop(0, n_chunks, prepare, 0)

    def advance(c, carry):
        rows = pl.ds(pl.multiple_of(c * GDN_CHUNK, GDN_CHUNK), GDN_CHUNK)
        tail = pl.ds(pl.multiple_of(c * GDN_CHUNK + GDN_CHUNK - 8, 8), 8)
        heads = range(HEADS)
        state = [s_scr[h] for h in heads]
        s16 = [state[h].astype(BF16) for h in heads]
        vn16 = [(u_scr[rows, lanes[h]] - _dot(w_scr[rows, lanes[h]], s16[h])).astype(BF16) for h in heads]
        for h in heads:
            s_scr[h] = state[h] * jnp.exp(gc_ref[tail, lanes[h]][7:8, :]) + lax.dot_general(
                kd_scr[rows, lanes[h]], vn16[h], (((0,), (0,)), ((), ())), preferred_element_type=F32)
        for h in heads:
            cs = lanes[h]
            o = _dot(qd_scr[rows, cs], s16[h]) + _dot(qk_scr[rows, cs], vn16[h])
            zg = z_ref[rows, cs]
            o_ref[rows, cs] = (_rms(o, gain) * (zg * _sigmoid(zg))).astype(BF16)
        return carry

    lax.fori_loop(0, n_chunks, advance, 0)

    @pl.when(ti == n_tiles - 1)
    def _():
        sfin_ref[...] = s_scr[...]


def _gdn_chunks(qkv, beta, gc, gcrow, z_src, z_col_block, s0, out_norm, layer, n_seq, seq_rows):
    tile_rows = min(seq_rows, GDN_TS)
    n_tiles = seq_rows // tile_rows
    n_masks = 2 + int(math.log2(GDN_CHUNK // GDN_BASE))

    def act(col_block):
        return pl.BlockSpec((tile_rows, WIDTH), lambda b, t: (b * n_tiles + t, col_block))

    slots = tile_rows // GDN_CHUNK * GDN_ROW_SLOTS
    state_spec = pl.BlockSpec((None, HEADS, HEAD_DIM, HEAD_DIM), lambda b, t: (b, 0, 0, 0))
    return pl.pallas_call(
        functools.partial(_gdn_chunk_body, n_tiles=n_tiles, tile_rows=tile_rows),
        grid=(n_seq, n_tiles),
        in_specs=[act(0), act(1), act(2), act(0), act(0),
                  pl.BlockSpec((slots, GDN_CHUNK), lambda b, t: (b * n_tiles + t, 0)),
                  act(z_col_block), state_spec,
                  pl.BlockSpec((None, 1, HEAD_DIM), lambda b, t: (layer, 0, 0))],
        out_specs=[act(0), state_spec],
        out_shape=[jax.ShapeDtypeStruct((n_seq * seq_rows, WIDTH), BF16),
                   jax.ShapeDtypeStruct((n_seq, HEADS, HEAD_DIM, HEAD_DIM), F32)],
        scratch_shapes=[pltpu.VMEM((HEADS, HEAD_DIM, HEAD_DIM), F32),
                        pltpu.VMEM((n_masks, GDN_CHUNK, GDN_CHUNK), F32),
                        pltpu.VMEM((tile_rows, WIDTH), F32),
                        pltpu.VMEM((tile_rows, WIDTH), BF16),
                        pltpu.VMEM((tile_rows, WIDTH), BF16),
                        pltpu.VMEM((tile_rows, WIDTH), BF16),
                        pltpu.VMEM((tile_rows, WIDTH), BF16)],
        compiler_params=_params("parallel", "arbitrary"),
        name="gdn_chunks",
    )(qkv, qkv, qkv, beta, gc, gcrow, z_src, s0, out_norm)


GDN_STEP_COLS = CONV_WIDTH + 3 * WIDTH


def _gdn_step_body(x_ref, s0_ref, gn_ref, o_ref, s_ref):
    heads = range(HEADS)
    col = lambda part, h: slice(part * WIDTH + h * HEAD_DIM, part * WIDTH + (h + 1) * HEAD_DIM)
    gain = gn_ref[...]
    k = [x_ref[:, col(1, h)] for h in heads]
    q = [x_ref[:, col(0, h)] * GDN_SCALE for h in heads]
    k16 = [k[h].astype(BF16) for h in heads]
    state = [s0_ref[h] for h in heads]
    s16 = [state[h].astype(BF16) for h in heads]
    k_s = [_dot(k16[h], s16[h]) for h in heads]
    q_s = [_dot(q[h].astype(BF16), s16[h]) for h in heads]
    decay = [jnp.exp(x_ref[:, col(4, h)]) for h in heads]
    v_new = [x_ref[:, col(3, h)] * (x_ref[:, col(2, h)] - decay[h] * k_s[h]) for h in heads]
    for h in heads:
        s_ref[h] = state[h] * decay[h][0:1, :] + lax.dot_general(
            k16[h], v_new[h].astype(BF16), (((0,), (0,)), ((), ())), preferred_element_type=F32)
    for h in heads:
        qk = jnp.sum(q[h].astype(BF16).astype(F32) * k16[h].astype(F32), axis=-1, keepdims=True)
        o = decay[h] * q_s[h] + qk * v_new[h].astype(BF16).astype(F32)
        zg = x_ref[:, col(5, h)]
        o_ref[:, h * HEAD_DIM:(h + 1) * HEAD_DIM] = (_rms(o, gain) * (zg * _sigmoid(zg))).astype(BF16)


def _gdn_step(x8, state, out_norm, layer):
    return pl.pallas_call(
        _gdn_step_body,
        grid=(DEC_BATCH,),
        in_specs=[pl.BlockSpec((None, 8, GDN_STEP_COLS), lambda b: (b, 0, 0)),
                  pl.BlockSpec((None, None, HEADS, HEAD_DIM, HEAD_DIM), lambda b: (layer, b, 0, 0, 0)),
                  pl.BlockSpec((None, 1, HEAD_DIM), lambda b: (layer, 0, 0))],
        out_specs=[pl.BlockSpec((None, 8, WIDTH), lambda b: (b, 0, 0)),
                   pl.BlockSpec((None, HEADS, HEAD_DIM, HEAD_DIM), lambda b: (b, 0, 0, 0))],
        out_shape=[jax.ShapeDtypeStruct((DEC_BATCH, 8, WIDTH), BF16),
                   jax.ShapeDtypeStruct((DEC_BATCH, HEADS, HEAD_DIM, HEAD_DIM), F32)],
        compiler_params=_params("parallel"),
        name="gdn_step",
    )(x8, state, out_norm)


def _pack_body(a_ref, b_ref, o_ref):
    n = pl.program_id(1)
    first_gate = C_GATE // PACK_TN
    last = PROJ_COLS // PACK_TN - 1
    shift = 2 * HEADS
    a = a_ref[...]
    shifted = jnp.concatenate([a[shift:, :], b_ref[:shift, :]], axis=0)
    row = lax.broadcasted_iota(jnp.int32, a.shape, 0)
    logits = jnp.where(row < shift, a, 0.0)
    tile = jnp.where(n < first_gate, a, jnp.where(n == last, logits, shifted))
    o_ref[...] = tile.T.astype(BF16)


def _pack_w_in(w_in):
    first_gate = C_GATE // PACK_TN
    last = PROJ_COLS // PACK_TN - 1
    w_t = jnp.swapaxes(w_in, 1, 2)

    def a_index(d, n):
        return (d, jnp.where(n == last, first_gate, n), 0)

    def b_index(d, n):
        return (d, jnp.where((n >= first_gate) & (n < last), n + 1, 0), 0)

    return pl.pallas_call(
        _pack_body,
        grid=(DEPTH, PROJ_COLS // PACK_TN),
        in_specs=[pl.BlockSpec((None, PACK_TN, D_MODEL), a_index),
                  pl.BlockSpec((None, PACK_TN, D_MODEL), b_index)],
        out_specs=pl.BlockSpec((None, D_MODEL, PACK_TN), lambda d, n: (d, 0, n)),
        out_shape=jax.ShapeDtypeStruct((DEPTH, D_MODEL, PROJ_COLS), BF16),
        compiler_params=_params("parallel", "arbitrary"),
        name="pack_w_in",
    )(w_t, w_t)


def _s5_params(a_re, a_im, log_dt, b_re, b_im, c_re, c_im, d):
    eye = jnp.eye(S5_GB, dtype=F32)

    def state_rows(x):
        return x.reshape(DEPTH, S5_NGB, 1, S5_ST)

    par = jnp.concatenate(
        [state_rows(a_re), state_rows(a_im),
         state_rows(jnp.repeat(log_dt, SSM_STATE, axis=-1)),
         jnp.zeros((DEPTH, S5_NGB, 5, S5_ST), F32)], axis=2)

    def b_blockdiag(b):
        b = b.reshape(DEPTH, S5_NGB, S5_GB, SSM_STATE, SSM_GROUP)
        return jnp.einsum('dbgpc,gh->dbgchp', b, eye).reshape(DEPTH, S5_NGB, S5_CH, S5_ST)

    def c_blockdiag(c):
        c = c.reshape(DEPTH, S5_NGB, S5_GB, SSM_GROUP, SSM_STATE)
        return jnp.einsum('dbgcp,gh->dbgphc', c, eye).reshape(DEPTH, S5_NGB, S5_ST, S5_CH)

    return (par, b_blockdiag(b_re), b_blockdiag(b_im), c_blockdiag(c_re), c_blockdiag(c_im),
            d.reshape(DEPTH, S5_NGB, 1, S5_CH))


def _segment_major(x):
    w = x.shape[-1]
    return x.reshape(BATCH, S5_LANES, S5_SEG, w).transpose(0, 2, 1, 3).reshape(N_PROMPT, w)


def _token_major(x):
    w = x.shape[-1]
    return x.reshape(BATCH, S5_SEG, S5_LANES, w).transpose(0, 2, 1, 3).reshape(N_PROMPT, w)


def _pad_rows(x):
    return jnp.concatenate([x, jnp.zeros((ROWS - x.shape[0], x.shape[1]), x.dtype)], axis=0)


def kernel(x_prompt, x_sample, cache_k, cache_v, page_table, state_ssm, state_conv, state_gdn, ffn1_norm, ffn1_w_gate, ffn1_w_up, ffn1_w_down, mix_norm, w_in, sb_bias, ssm_a_re, ssm_a_im, ssm_log_dt, ssm_b_re, ssm_b_im, ssm_c_re, ssm_c_im, ssm_d, ssm_w_glu, gdn_conv_w, gdn_a_log, gdn_dt_bias, gdn_out_norm, w_branch, w_out, ffn2_norm, ffn2_w_gate, ffn2_w_up, ffn2_w_down, final_norm):
    bf = lambda w: w.astype(BF16)
    gain3 = lambda g: g.reshape(DEPTH, 1, D_MODEL)
    w_in_p = _pack_w_in(w_in)
    ffn1 = (gain3(ffn1_norm), bf(ffn1_w_gate), bf(ffn1_w_up), bf(ffn1_w_down))
    ffn2 = (gain3(ffn2_norm), bf(ffn2_w_gate), bf(ffn2_w_up), bf(ffn2_w_down))
    mix_gain = gain3(mix_norm)
    w_glu, w_br, w_o = bf(ssm_w_glu), bf(w_branch).reshape(DEPTH, 3, WIDTH, D_MODEL), bf(w_out)
    s5p = _s5_params(ssm_a_re, ssm_a_im, ssm_log_dt, ssm_b_re, ssm_b_im, ssm_c_re, ssm_c_im, ssm_d)
    bias_flat = sb_bias.reshape(DEPTH * HEADS)
    bias_lanes = jnp.tile(sb_bias, (1, PAGE)).reshape(DEPTH, 1, DEC_LANES)
    page_flat = page_table.reshape(DEC_BATCH * N_PAGES)
    conv_w8 = jnp.pad(gdn_conv_w, ((0, 0), (0, 8 - CONV_TAPS), (0, 0)))
    conv_hist = state_conv.transpose(0, 2, 1, 3)
    lane_pad = jnp.zeros((DEPTH, 128 - 2 * HEADS), F32)
    gate_par = jnp.stack([jnp.concatenate([jnp.zeros((DEPTH, HEADS), F32), gdn_a_log, lane_pad], axis=1),
                          jnp.concatenate([jnp.zeros((DEPTH, HEADS), F32), gdn_dt_bias, lane_pad], axis=1)]
                         + [jnp.zeros((DEPTH, 128), F32)] * 6, axis=1)
    out_norm = gdn_out_norm.reshape(DEPTH, 1, HEAD_DIM)
    ssm_h0r = state_ssm[..., 0].reshape(DEPTH, DEC_BATCH, WIDTH * SSM_STATE // SSM_GROUP)
    ssm_h0i = state_ssm[..., 1].reshape(DEPTH, DEC_BATCH, WIDTH * SSM_STATE // SSM_GROUP)

    x = _pad_rows(jnp.concatenate([x_prompt.reshape(N_PROMPT, D_MODEL),
                                   x_sample.reshape(DEC_BATCH, D_MODEL)], axis=0))
    sample = slice(N_PROMPT, N_PROMPT + DEC_BATCH)
    ks, vs, sp, ss, cp, cs_, gp, gs = ([] for _ in range(8))
    k_all = jnp.zeros((DEPTH, N_PROMPT, WIDTH), F32)
    v_all = jnp.zeros((DEPTH, N_PROMPT, WIDTH), F32)
    for l in range(DEPTH):
        x = _ffn(x, *ffn1, l)
        proj = _in_proj(x, mix_gain, w_in_p, l)

        oa_p, k_all, v_all = _sb_prompt(proj, bias_flat, l, k_all, v_all)
        q_s = proj[sample, C_Q:C_Q + WIDTH].reshape(DEC_BATCH, HEADS, HEAD_DIM)
        oa_s = _sb_decode(q_s, cache_k, cache_v, page_flat, bias_lanes, l)
        oa = _pad_rows(jnp.concatenate([oa_p, oa_s.reshape(DEC_BATCH, WIDTH).astype(BF16)], axis=0))

        gb_p, hr_p, hi_p = _s5_prompt(_segment_major(proj[:N_PROMPT, C_U:C_U + WIDTH]), s5p, l)
        gb_s, hr_s, hi_s = _s5_sample(proj, s5p, ssm_h0r[l], ssm_h0i[l], l)
        ob = _glu(_pad_rows(jnp.concatenate([_token_major(gb_p), gb_s], axis=0)), w_glu, l)

        qkv_p = _conv_prompt(proj, conv_w8, l)
        beta_p, gc_p, gcrow_p = _gdn_gates(proj, gate_par, l, GDN_TS, N_PROMPT // GDN_TS, 0, GDN_CHUNK)
        oc_p, gdn_p = _gdn_chunks(qkv_p, beta_p, gc_p, gcrow_p, proj, C_Z // WIDTH,
                                  jnp.zeros((BATCH, HEADS, HEAD_DIM, HEAD_DIM), F32), out_norm, l, BATCH, SEQ)
        qkv_s = _conv_sample(proj, conv_hist, conv_w8, l)
        beta_s, g_s = _gdn_gates(proj, gate_par, l, DEC_BATCH, 1, SAMPLE_ROW_BLOCK, 1)
        step_in = jnp.concatenate([qkv_s, beta_s, g_s, proj[sample, C_Z:C_Z + WIDTH]], axis=1)
        oc_s, gdn_s = _gdn_step(jnp.pad(step_in[:, None, :], ((0, 0), (0, 7), (0, 0))), state_gdn, out_norm, l)
        oc = _pad_rows(jnp.concatenate([oc_p, oc_s[:, 0, :]], axis=0))

        x = _merge(x, oa, ob, oc, proj, w_br, w_o, l)
        x = _ffn(x, *ffn2, l)

        ks.append(proj[sample, C_K:C_K + WIDTH].reshape(DEC_BATCH, 1, HEADS, HEAD_DIM))
        vs.append(proj[sample, C_V:C_V + WIDTH].reshape(DEC_BATCH, 1, HEADS, HEAD_DIM))
        state = lambda r, i, n: jnp.stack([r.reshape(n, WIDTH // SSM_GROUP, SSM_STATE),
                                           i.reshape(n, WIDTH // SSM_GROUP, SSM_STATE)], axis=-1)
        sp.append(state(hr_p, hi_p, BATCH))
        ss.append(state(hr_s, hi_s, DEC_BATCH))
        qkv_cols = slice(C_QKV, C_QKV + CONV_WIDTH)
        cp.append(jnp.stack([proj[(b + 1) * SEQ - (CONV_TAPS - 1):(b + 1) * SEQ, qkv_cols] for b in range(BATCH)]))
        cs_.append(jnp.concatenate([state_conv[l][:, 1:], proj[sample, qkv_cols][:, None, :]], axis=1))
        gp.append(gdn_p)
        gs.append(gdn_s)

    gain = final_norm.reshape(1, D_MODEL)
    y_p = _final_norm(x, gain, GDN_TS, N_PROMPT // GDN_TS, 0)
    y_s = _final_norm(x, gain, DEC_BATCH, 1, SAMPLE_ROW_BLOCK)
    kv_shape = (DEPTH, BATCH, SEQ, HEADS, HEAD_DIM)
    return (y_p.reshape(BATCH, SEQ, D_MODEL), y_s.reshape(DEC_BATCH, 1, D_MODEL),
            k_all.reshape(kv_shape), v_all.reshape(kv_shape), jnp.stack(ks), jnp.stack(vs),
            jnp.stack(sp), jnp.stack(ss), jnp.stack(cp), jnp.stack(cs_),
            jnp.stack(gp), jnp.stack(gs))
```

```python
import functools
import math

import jax
import jax.numpy as jnp
from jax import lax
from jax.experimental import pallas as pl
from jax.experimental.pallas import tpu as pltpu

F32 = jnp.float32
BF16 = jnp.bfloat16
HIGHEST = lax.Precision.HIGHEST

D_MODEL = 2048
BATCH = 4
SEQ = 2048
DEPTH = 2
DEC_BATCH = 32
PAGE = 128
N_PAGES = 64
HEADS = 8
HEAD_DIM = 128
WIDTH = HEADS * HEAD_DIM
SSM_GROUP = 16
SSM_STATE = 64
CONV_TAPS = 4
CONV_WIDTH = 3 * WIDTH
FFN_DIM = 5632
EPS = 1e-6

N_PROMPT = BATCH * SEQ
TM = 528
ROWS = 16 * TM
SAMPLE_ROW_BLOCK = N_PROMPT // DEC_BATCH

C_Q, C_K, C_V, C_U, C_QKV, C_Z, C_GATE, C_BA = 0, 1024, 2048, 3072, 4096, 7168, 8192, 14336
PROJ_COLS = 14592
TN_PROJ = 768
PACK_TN = 512
PACK_TILES = -(-PROJ_COLS // PACK_TN)
TF = 512
TM_BIG = 2 * TM
TM_MERGE = TM // 3

VMEM_LIMIT = 56 * 1024 * 1024


def _params(*sem):
    return pltpu.CompilerParams(dimension_semantics=sem, vmem_limit_bytes=VMEM_LIMIT)


def _rms(x, gain):
    ms = jnp.mean(x * x, axis=-1, keepdims=True)
    return x * lax.rsqrt(ms + EPS) * gain


def _softplus(x):
    return jnp.maximum(x, 0.0) + jnp.log(1.0 + jnp.exp(-jnp.abs(x)))


def _sigmoid(x):
    return 1.0 / (1.0 + jnp.exp(-x))


def _dot(a, b, **kw):
    return jnp.dot(a, b, preferred_element_type=F32, **kw)


def _dot_nt(a, b, **kw):
    return lax.dot_general(a, b, (((1,), (1,)), ((), ())), preferred_element_type=F32, **kw)


def _ffn_body(x_ref, g_ref, wg_ref, wu_ref, wd_ref, o_ref, h_scr):
    @pl.when(pl.program_id(1) == 0)
    def _():
        x = x_ref[...]
        h_scr[...] = _rms(x, g_ref[...]).astype(BF16)
        o_ref[...] = x

    h = h_scr[...]
    a = _dot(h, wg_ref[...])
    b = _dot(h, wu_ref[...])
    act = (a * _sigmoid(a)) * b * 0.5
    o_ref[...] += _dot(act.astype(BF16), wd_ref[...])


def _ffn(x, gain, wg, wu, wd, layer):
    return pl.pallas_call(
        _ffn_body,
        grid=(ROWS // TM, FFN_DIM // TF),
        in_specs=[
            pl.BlockSpec((TM, D_MODEL), lambda i, f: (i, 0)),
            pl.BlockSpec((None, 1, D_MODEL), lambda i, f: (layer, 0, 0)),
            pl.BlockSpec((None, D_MODEL, TF), lambda i, f: (layer, 0, f)),
            pl.BlockSpec((None, D_MODEL, TF), lambda i, f: (layer, 0, f)),
            pl.BlockSpec((None, TF, D_MODEL), lambda i, f: (layer, f, 0)),
        ],
        out_specs=pl.BlockSpec((TM, D_MODEL), lambda i, f: (i, 0)),
        out_shape=jax.ShapeDtypeStruct((ROWS, D_MODEL), F32),
        scratch_shapes=[pltpu.VMEM((TM, D_MODEL), BF16)],
        compiler_params=_params("parallel", "arbitrary"),
        name="ffn",
    )(x, gain, wg, wu, wd)


def _proj_body(x_ref, g_ref, w_ref, o_ref, h_scr):
    @pl.when(pl.program_id(1) == 0)
    def _():
        h_scr[...] = _rms(x_ref[...], g_ref[...]).astype(BF16)

    o_ref[...] = _dot(h_scr[...], w_ref[...])


def _in_proj(x, gain, w, layer):
    return pl.pallas_call(
        _proj_body,
        grid=(ROWS // TM_BIG, PROJ_COLS // TN_PROJ),
        in_specs=[
            pl.BlockSpec((TM_BIG, D_MODEL), lambda i, n: (i, 0), pipeline_mode=pl.Buffered(1)),
            pl.BlockSpec((None, 1, D_MODEL), lambda i, n: (layer, 0, 0)),
            pl.BlockSpec((None, D_MODEL, TN_PROJ), lambda i, n: (layer, 0, n)),
        ],
        out_specs=pl.BlockSpec((TM_BIG, TN_PROJ), lambda i, n: (i, n)),
        out_shape=jax.ShapeDtypeStruct((ROWS, PROJ_COLS), F32),
        scratch_shapes=[pltpu.VMEM((TM_BIG, D_MODEL), BF16)],
        compiler_params=_params("parallel", "arbitrary"),
        name="in_proj",
    )(x, gain, w)


def _glu_body(g_ref, w_ref, o_ref):
    g = g_ref[...]
    o_ref[...] = (g * _sigmoid(_dot(g.astype(BF16), w_ref[...]))).astype(BF16)


def _glu(g, w, layer):
    return pl.pallas_call(
        _glu_body,
        grid=(ROWS // TM,),
        in_specs=[
            pl.BlockSpec((TM, WIDTH), lambda i: (i, 0)),
            pl.BlockSpec((None, WIDTH, WIDTH), lambda i: (layer, 0, 0)),
        ],
        out_specs=pl.BlockSpec((TM, WIDTH), lambda i: (i, 0)),
        out_shape=jax.ShapeDtypeStruct((ROWS, WIDTH), BF16),
        compiler_params=_params("parallel"),
        name="glu",
    )(g, w)


def _merge_body(x_ref, oa_ref, ob_ref, oc_ref, ga_ref, gb_ref, gc_ref, wa_ref, wb_ref, wc_ref, wo_ref, o_ref):
    m = (_sigmoid(ga_ref[...]) * _dot(oa_ref[...], wa_ref[...])
         + _sigmoid(gb_ref[...]) * _dot(ob_ref[...], wb_ref[...])
         + _sigmoid(gc_ref[...]) * _dot(oc_ref[...], wc_ref[...]))
    o_ref[...] = x_ref[...] + _dot(m.astype(BF16), wo_ref[...])


def _merge(x, oa, ob, oc, proj, w_branch, w_out, layer):
    gate_blk = C_GATE // D_MODEL
    o_spec = pl.BlockSpec((TM_MERGE, WIDTH), lambda i: (i, 0))

    def gate_spec(which):
        return pl.BlockSpec((TM_MERGE, D_MODEL), lambda i: (i, gate_blk + which))

    def w_spec(which):
        return pl.BlockSpec((None, None, WIDTH, D_MODEL), lambda i: (layer, which, 0, 0),
                            pipeline_mode=pl.Buffered(1))

    return pl.pallas_call(
        _merge_body,
        grid=(ROWS // TM_MERGE,),
        in_specs=[
            pl.BlockSpec((TM_MERGE, D_MODEL), lambda i: (i, 0)),
            o_spec, o_spec, o_spec,
            gate_spec(0), gate_spec(1), gate_spec(2),
            w_spec(0), w_spec(1), w_spec(2),
            pl.BlockSpec((None, D_MODEL, D_MODEL), lambda i: (layer, 0, 0), pipeline_mode=pl.Buffered(1)),
        ],
        out_specs=pl.BlockSpec((TM_MERGE, D_MODEL), lambda i: (i, 0)),
        out_shape=jax.ShapeDtypeStruct((ROWS, D_MODEL), F32),
        compiler_params=_params("parallel"),
        name="merge",
    )(x, oa, ob, oc, proj, proj, proj, w_branch, w_branch, w_branch, w_out)


def _final_norm_body(x_ref, g_ref, o_ref):
    o_ref[...] = _rms(x_ref[...], g_ref[...])


def _final_norm(x, gain, rows, n_blocks, first_block):
    return pl.pallas_call(
        _final_norm_body,
        grid=(n_blocks,),
        in_specs=[pl.BlockSpec((rows, D_MODEL), lambda i: (first_block + i, 0)),
                  pl.BlockSpec((1, D_MODEL), lambda i: (0, 0))],
        out_specs=pl.BlockSpec((rows, D_MODEL), lambda i: (i, 0)),
        out_shape=jax.ShapeDtypeStruct((rows * n_blocks, D_MODEL), F32),
        compiler_params=_params("parallel"),
        name="final_norm",
    )(x, gain)


SB_BQ = 256
SB_BK = 256
SB_HB = 4
SB_SCALE = HEAD_DIM ** -0.5


def _stick_terms(z):
    t = jnp.log(1.0 + jnp.exp(-jnp.abs(z)))
    return jnp.minimum(z, 0.0) - t, jnp.minimum(-z, 0.0) - t


LOG2E = 1.0 / math.log(2.0)


def _sb_prompt_body(bias_ref, q_ref, k_ref, v_ref, *rest, layer):
    o_ref, k_all_ref, v_all_ref, q_scr, acc_scr, run_scr = rest[2:]
    hg = pl.program_id(1)
    qi = pl.program_id(2)

    @pl.when(qi == 0)
    def _():
        k_all_ref[...] = k_ref[...]
        v_all_ref[...] = v_ref[...]

    heads = range(SB_HB)
    lanes = [slice(hh * HEAD_DIM, (hh + 1) * HEAD_DIM) for hh in heads]
    row = lax.broadcasted_iota(jnp.int32, (SB_BQ, SB_BK), 0)
    col = lax.broadcasted_iota(jnp.int32, (SB_BQ, SB_BK), 1)
    q_pos = qi * SB_BQ + row
    r2 = lax.broadcasted_iota(jnp.int32, (SB_BK, SB_BK), 0)
    c2 = lax.broadcasted_iota(jnp.int32, (SB_BK, SB_BK), 1)
    later_key = jnp.where(r2 > c2, 1.0, 0.0).astype(BF16)
    bias2 = [bias_ref[layer * HEADS + hg * SB_HB + hh] * LOG2E for hh in heads]

    for hh in heads:
        q_scr[hh] = (q_ref[:, lanes[hh]] * (SB_SCALE * LOG2E)).astype(BF16)
    acc_scr[...] = jnp.zeros_like(acc_scr)
    run_scr[...] = jnp.zeros_like(run_scr)

    def visit(j, masked):
        k_rows = pl.ds(pl.multiple_of(j * SB_BK, SB_BK), SB_BK)
        mask = (j * SB_BK + col) < q_pos
        z = [_dot_nt(q_scr[hh], k_ref[k_rows, lanes[hh]].astype(BF16)) + bias2[hh] for hh in heads]
        t = [jnp.log(1.0 + jnp.exp2(-jnp.abs(z[hh]))) * LOG2E for hh in heads]
        log_hit = [jnp.minimum(z[hh], 0.0) - t[hh] for hh in heads]
        log_fail = [jnp.minimum(-z[hh], 0.0) - t[hh] for hh in heads]
        if masked:
            log_fail = [jnp.where(mask, log_fail[hh], 0.0) for hh in heads]
        suffix = [_dot(log_fail[hh].astype(BF16), later_key) for hh in heads]
        w = []
        for hh in heads:
            run = run_scr[hh]
            w_h = jnp.exp2(log_hit[hh] + suffix[hh] + run)
            block_total = suffix[hh][:, 0:1] + log_fail[hh][:, 0:1]
            run_scr[hh] = run + jnp.broadcast_to(block_total, (SB_BQ, SB_BK))
            w.append(jnp.where(mask, w_h, 0.0) if masked else w_h)
        for hh in heads:
            acc_scr[hh] += _dot(w[hh].astype(BF16), v_ref[k_rows, lanes[hh]].astype(BF16))

    visit(qi, True)

    def below_diagonal(it, carry):
        visit(qi - 1 - it, False)
        return carry

    lax.fori_loop(0, qi, below_diagonal, 0)
    for hh in heads:
        o_ref[:, lanes[hh]] = acc_scr[hh].astype(BF16)


def _sb_prompt(proj, bias_flat, layer, k_all, v_all):
    hw = SB_HB * HEAD_DIM
    n_hg = HEADS // SB_HB
    nq = SEQ // SB_BQ
    all_spec = pl.BlockSpec((None, SEQ, hw), lambda b, h, i, bias: (layer, b, h))
    all_shape = jax.ShapeDtypeStruct((DEPTH, N_PROMPT, WIDTH), F32)
    grid_spec = pltpu.PrefetchScalarGridSpec(
        num_scalar_prefetch=1,
        grid=(BATCH, n_hg, nq),
        in_specs=[
            pl.BlockSpec((SB_BQ, hw), lambda b, h, i, bias: (b * nq + i, C_Q // hw + h)),
            pl.BlockSpec((SEQ, hw), lambda b, h, i, bias: (b, C_K // hw + h)),
            pl.BlockSpec((SEQ, hw), lambda b, h, i, bias: (b, C_V // hw + h)),
            pl.BlockSpec(memory_space=pl.ANY),
            pl.BlockSpec(memory_space=pl.ANY),
        ],
        out_specs=[pl.BlockSpec((SB_BQ, hw), lambda b, h, i, bias: (b * nq + i, h)), all_spec, all_spec],
        scratch_shapes=[pltpu.VMEM((SB_HB, SB_BQ, HEAD_DIM), BF16),
                        pltpu.VMEM((SB_HB, SB_BQ, HEAD_DIM), F32),
                        pltpu.VMEM((SB_HB, SB_BQ, SB_BK), F32)],
    )
    return pl.pallas_call(
        functools.partial(_sb_prompt_body, layer=layer),
        grid_spec=grid_spec,
        out_shape=[jax.ShapeDtypeStruct((N_PROMPT, WIDTH), BF16), all_shape, all_shape],
        input_output_aliases={4: 1, 5: 2},
        compiler_params=_params("parallel", "parallel", "arbitrary"),
        name="sb_prompt",
    )(bias_flat, proj, proj, proj, k_all, v_all)


DEC_PAGES = 16
DEC_STEPS = N_PAGES // DEC_PAGES
DEC_LANES = PAGE * HEADS


def _lane_shift_up(x, s):
    n = x.shape[1]
    lane = lax.broadcasted_iota(jnp.int32, x.shape, 1)
    return jnp.where(lane < n - s, pltpu.roll(x, n - s, 1), 0.0)


def _sb_decode_body(pt_ref, q_ref, bias_ref, *refs):
    k_refs = refs[:DEC_PAGES]
    v_refs = refs[DEC_PAGES:2 * DEC_PAGES]
    o_ref, acc_scr, run_scr = refs[2 * DEC_PAGES:]
    g = pl.program_id(1)

    @pl.when(g == 0)
    def _():
        acc_scr[...] = jnp.zeros_like(acc_scr)
        run_scr[...] = jnp.zeros_like(run_scr)

    sub = lax.broadcasted_iota(jnp.int32, (HEADS, DEC_LANES), 0)
    lane = lax.broadcasted_iota(jnp.int32, (HEADS, DEC_LANES), 1)
    own_head = (lane & (HEADS - 1)) == sub
    q = q_ref[...].astype(BF16)
    rows = []
    for i in range(DEC_PAGES):
        k2 = k_refs[i][...].reshape(DEC_LANES, HEAD_DIM).astype(BF16)
        zt = _dot_nt(q, k2)
        rows.append(jnp.sum(jnp.where(own_head, zt, 0.0), axis=0, keepdims=True))
    z = jnp.concatenate(rows, axis=0) * SB_SCALE + bias_ref[...]
    log_hit, log_fail = _stick_terms(z)

    suffix = log_fail
    for s in (HEADS, 4 * HEADS):
        suffix = (suffix + _lane_shift_up(suffix, s)) + (_lane_shift_up(suffix, 2 * s) + _lane_shift_up(suffix, 3 * s))
    s = 16 * HEADS
    while s < DEC_LANES:
        suffix = suffix + jnp.concatenate([suffix[:, s:], jnp.zeros((DEC_PAGES, s), F32)], axis=1)
        s *= 2
    tile_lane = lax.broadcasted_iota(jnp.int32, (DEC_PAGES, 16 * HEADS), 1)
    tot = jnp.where(tile_lane < HEADS, suffix[:, :16 * HEADS], 0.0)
    for s in (HEADS, 4 * HEADS):
        tot = (tot + pltpu.roll(tot, s, 1)) + (pltpu.roll(tot, 2 * s, 1) + pltpu.roll(tot, 3 * s, 1))
    tot = jnp.concatenate([tot] * (DEC_LANES // (16 * HEADS)), axis=1)
    page_r = lax.broadcasted_iota(jnp.int32, (DEC_PAGES, DEC_PAGES), 0)
    page_c = lax.broadcasted_iota(jnp.int32, (DEC_PAGES, DEC_PAGES), 1)
    later = jnp.where(page_c > page_r, 1.0, 0.0)
    after = _dot(later, tot, precision=HIGHEST)
    run = run_scr[...]
    log_between = (suffix - log_fail) + after + run
    run_scr[...] = run + jnp.sum(tot, axis=0, keepdims=True)
    w = jnp.exp(log_hit + log_between)

    acc = acc_scr[...]
    for i in range(DEC_PAGES):
        wm = jnp.where(own_head, jnp.broadcast_to(w[i:i + 1, :], (HEADS, DEC_LANES)), 0.0).astype(BF16)
        v2 = v_refs[i][...].reshape(DEC_LANES, HEAD_DIM).astype(BF16)
        acc = acc + _dot(wm, v2)
    acc_scr[...] = acc

    @pl.when(g == DEC_STEPS - 1)
    def _():
        o_ref[...] = acc


def _sb_decode(q, cache_k, cache_v, page_table_flat, bias_lanes, layer):
    def page_spec(i):
        def index(b, g, pt):
            return (layer, pt[b * N_PAGES + (DEC_STEPS - 1 - g) * DEC_PAGES + i], 0, 0, 0)
        return pl.BlockSpec((None, None, PAGE, HEADS, HEAD_DIM), index)

    grid_spec = pltpu.PrefetchScalarGridSpec(
        num_scalar_prefetch=1,
        grid=(DEC_BATCH, DEC_STEPS),
        in_specs=[pl.BlockSpec((None, HEADS, HEAD_DIM), lambda b, g, pt: (b, 0, 0)),
                  pl.BlockSpec((None, 1, DEC_LANES), lambda b, g, pt: (layer, 0, 0))]
        + [page_spec(i) for i in range(DEC_PAGES)] * 2,
        out_specs=pl.BlockSpec((None, HEADS, HEAD_DIM), lambda b, g, pt: (b, 0, 0)),
        scratch_shapes=[pltpu.VMEM((HEADS, HEAD_DIM), F32), pltpu.VMEM((1, DEC_LANES), F32)],
    )
    return pl.pallas_call(
        _sb_decode_body,
        grid_spec=grid_spec,
        out_shape=jax.ShapeDtypeStruct((DEC_BATCH, HEADS, HEAD_DIM), F32),
        compiler_params=_params("parallel", "arbitrary"),
        name="sb_decode",
    )(page_table_flat, q, bias_lanes, *([cache_k] * DEC_PAGES), *([cache_v] * DEC_PAGES))


S5_GB = 8
S5_CH = S5_GB * SSM_GROUP
S5_ST = S5_GB * SSM_STATE
S5_NGB = WIDTH // S5_CH
S5_LANES = 8
S5_SEG = SEQ // S5_LANES


def _s5_discretize(par):
    ar, ai, ldt = par[0:1], par[1:2], par[2:3]
    dt = jnp.exp(ldt)
    mag = jnp.exp(dt * ar)
    abr = mag * jnp.cos(dt * ai)
    abi = mag * jnp.sin(dt * ai)
    den = ar * ar + ai * ai
    cr = ((abr - 1.0) * ar + abi * ai) / den
    ci = (abi * ar - (abr - 1.0) * ai) / den
    return abr, abi, cr, ci


def _gelu(y):
    return 0.5 * y * (1.0 + jnp.tanh(math.sqrt(2.0 / math.pi) * (y + 0.044715 * (y * y * y))))


def _s5_prompt_body(u_ref, par_ref, bre_ref, bim_ref, cre_ref, cim_ref, d_ref, h0r_ref, h0i_ref,
                    o_ref, hr_ref, hi_ref, xr_scr, xi_scr):
    abr, abi, cr, ci = _s5_discretize(par_ref[...])
    bre, bim = bre_ref[...], bim_ref[...]
    bbr = (cr * bre - ci * bim).astype(BF16)
    bbi = (cr * bim + ci * bre).astype(BF16)
    u = u_ref[...]
    ub = u.astype(BF16)
    xr_scr[...] = _dot(ub, bbr)
    xi_scr[...] = _dot(ub, bbi)

    shape = (S5_LANES, S5_ST)
    ar8 = jnp.broadcast_to(abr, shape)
    ai8 = jnp.broadcast_to(abi, shape)

    def local_step(t, carry):
        xr, xi = carry
        rows = pl.ds(pl.multiple_of(t * S5_LANES, S5_LANES), S5_LANES)
        nr = ar8 * xr - ai8 * xi + xr_scr[rows, :]
        ni = ar8 * xi + ai8 * xr + xi_scr[rows, :]
        xr_scr[rows, :] = nr
        xi_scr[rows, :] = ni
        return nr, ni

    zero = jnp.zeros(shape, F32)
    fr, fi = lax.fori_loop(0, S5_SEG, local_step, (zero, zero), unroll=8)

    pr, pi = abr, abi
    for _ in range(int(math.log2(S5_SEG))):
        pr, pi = pr * pr - pi * pi, 2.0 * pr * pi
    hr, hi = h0r_ref[...], h0i_ref[...]
    in_r, in_i = [], []
    for j in range(S5_LANES):
        in_r.append(hr)
        in_i.append(hi)
        hr, hi = fr[j:j + 1] + pr * hr - pi * hi, fi[j:j + 1] + pr * hi + pi * hr
    hr_ref[...] = hr
    hi_ref[...] = hi
    cin_r = jnp.concatenate(in_r, axis=0)
    cin_i = jnp.concatenate(in_i, axis=0)

    def fix_step(t, carry):
        wr, wi = carry
        rows = pl.ds(pl.multiple_of(t * S5_LANES, S5_LANES), S5_LANES)
        xr_scr[rows, :] = xr_scr[rows, :] + wr * cin_r - wi * cin_i
        xi_scr[rows, :] = xi_scr[rows, :] + wr * cin_i + wi * cin_r
        return wr * ar8 - wi * ai8, wr * ai8 + wi * ar8

    lax.fori_loop(0, S5_SEG, fix_step, (ar8, ai8), unroll=8)

    y = (_dot(xr_scr[...].astype(BF16), cre_ref[...].astype(BF16))
         - _dot(xi_scr[...].astype(BF16), cim_ref[...].astype(BF16)) + d_ref[...] * u)
    o_ref[...] = _gelu(y)


def _s5_param_specs(layer):
    lead = (lambda *a: a[-1])

    def spec(shape):
        return pl.BlockSpec((None, None) + shape, lambda *a: (layer, lead(*a), 0, 0))

    return [spec((8, S5_ST)), spec((S5_CH, S5_ST)), spec((S5_CH, S5_ST)),
            spec((S5_ST, S5_CH)), spec((S5_ST, S5_CH)), spec((1, S5_CH))]


def _s5_prompt(u_perm, s5p, layer):
    zeros = jnp.zeros((BATCH, S5_NGB, 1, S5_ST), F32)
    state_spec = pl.BlockSpec((None, None, 1, S5_ST), lambda b, g: (b, g, 0, 0))
    out, hr, hi = pl.pallas_call(
        _s5_prompt_body,
        grid=(BATCH, S5_NGB),
        in_specs=[pl.BlockSpec((SEQ, S5_CH), lambda b, g: (b, g))] + _s5_param_specs(layer)
        + [state_spec, state_spec],
        out_specs=[pl.BlockSpec((SEQ, S5_CH), lambda b, g: (b, g)), state_spec, state_spec],
        out_shape=[jax.ShapeDtypeStruct((N_PROMPT, WIDTH), F32),
                   jax.ShapeDtypeStruct((BATCH, S5_NGB, 1, S5_ST), F32),
                   jax.ShapeDtypeStruct((BATCH, S5_NGB, 1, S5_ST), F32)],
        scratch_shapes=[pltpu.VMEM((SEQ, S5_ST), F32), pltpu.VMEM((SEQ, S5_ST), F32)],
        compiler_params=_params("parallel", "parallel"),
        name="s5_prompt",
    )(u_perm, *s5p, zeros, zeros)
    return out, hr, hi


def _s5_sample_body(u_ref, par_ref, bre_ref, bim_ref, cre_ref, cim_ref, d_ref, h0r_ref, h0i_ref,
                    o_ref, hr_ref, hi_ref):
    abr, abi, cr, ci = _s5_discretize(par_ref[...])
    bre, bim = bre_ref[...], bim_ref[...]
    bbr = (cr * bre - ci * bim).astype(BF16)
    bbi = (cr * bim + ci * bre).astype(BF16)
    u = u_ref[...]
    ub = u.astype(BF16)
    h0r, h0i = h0r_ref[...], h0i_ref[...]
    xr = _dot(ub, bbr) + abr * h0r - abi * h0i
    xi = _dot(ub, bbi) + abr * h0i + abi * h0r
    hr_ref[...] = xr
    hi_ref[...] = xi
    y = (_dot(xr.astype(BF16), cre_ref[...].astype(BF16))
         - _dot(xi.astype(BF16), cim_ref[...].astype(BF16)) + d_ref[...] * u)
    o_ref[...] = _gelu(y)


def _s5_sample(proj, s5p, h0r, h0i, layer):
    state_spec = pl.BlockSpec((DEC_BATCH, S5_ST), lambda g: (0, g))
    return pl.pallas_call(
        _s5_sample_body,
        grid=(S5_NGB,),
        in_specs=[pl.BlockSpec((DEC_BATCH, S5_CH), lambda g: (SAMPLE_ROW_BLOCK, C_U // S5_CH + g))]
        + _s5_param_specs(layer) + [state_spec, state_spec],
        out_specs=[pl.BlockSpec((DEC_BATCH, S5_CH), lambda g: (0, g)), state_spec, state_spec],
        out_shape=[jax.ShapeDtypeStruct((DEC_BATCH, WIDTH), F32),
                   jax.ShapeDtypeStruct((DEC_BATCH, S5_NGB * S5_ST), F32),
                   jax.ShapeDtypeStruct((DEC_BATCH, S5_NGB * S5_ST), F32)],
        compiler_params=_params("parallel"),
        name="s5_sample",
    )(proj, *s5p, h0r, h0i)


GDN_TS = 512
GDN_SCALE = HEAD_DIM ** -0.5


def _l2norm_heads(xc, o_ref):
    for h in range(HEADS):
        cs = slice(h * HEAD_DIM, (h + 1) * HEAD_DIM)
        seg = xc[:, cs]
        o_ref[:, cs] = seg * lax.rsqrt(jnp.sum(seg * seg, axis=-1, keepdims=True) + EPS)


def _conv_prompt_body(x_ref, prev_ref, w_ref, o_ref, ext_scr):
    i = pl.program_id(1)
    part = pl.program_id(2)
    x = x_ref[...]
    ext_scr[0:8, :] = jnp.where(i == 0, 0.0, prev_ref[...])
    ext_scr[8:8 + GDN_TS, :] = x
    w = w_ref[...]
    acc = x * w[3:4]
    for tap in range(CONV_TAPS - 1):
        back = CONV_TAPS - 1 - tap
        acc = acc + ext_scr[8 - back:8 - back + GDN_TS, :] * w[tap:tap + 1]
    xc = acc * _sigmoid(acc)

    @pl.when(part < 2)
    def _():
        _l2norm_heads(xc, o_ref)

    @pl.when(part == 2)
    def _():
        o_ref[...] = xc


def _conv_prompt(proj, conv_w, layer):
    n_t = SEQ // GDN_TS
    col0 = C_QKV // WIDTH

    def prev_index(b, i, p):
        return (jnp.maximum(b * (SEQ // 8) + i * (GDN_TS // 8) - 1, 0), col0 + p)

    return pl.pallas_call(
        _conv_prompt_body,
        grid=(BATCH, n_t, 3),
        in_specs=[
            pl.BlockSpec((GDN_TS, WIDTH), lambda b, i, p: (b * n_t + i, col0 + p)),
            pl.BlockSpec((8, WIDTH), prev_index),
            pl.BlockSpec((None, 8, WIDTH), lambda b, i, p: (layer, 0, p)),
        ],
        out_specs=pl.BlockSpec((GDN_TS, WIDTH), lambda b, i, p: (b * n_t + i, p)),
        out_shape=jax.ShapeDtypeStruct((N_PROMPT, CONV_WIDTH), F32),
        scratch_shapes=[pltpu.VMEM((GDN_TS + 8, WIDTH), F32)],
        compiler_params=_params("parallel", "parallel", "parallel"),
        name="gdn_conv_prompt",
    )(proj, proj, conv_w)


def _conv_sample_body(x_ref, hist_ref, w_ref, o_ref):
    part = pl.program_id(0)
    w = w_ref[...]
    acc = x_ref[...] * w[3:4]
    for tap in range(CONV_TAPS - 1):
        acc = acc + hist_ref[tap] * w[tap:tap + 1]
    xc = acc * _sigmoid(acc)

    @pl.when(part < 2)
    def _():
        _l2norm_heads(xc, o_ref)

    @pl.when(part == 2)
    def _():
        o_ref[...] = xc


def _conv_sample(proj, hist, conv_w, layer):
    col0 = C_QKV // WIDTH
    return pl.pallas_call(
        _conv_sample_body,
        grid=(3,),
        in_specs=[
            pl.BlockSpec((DEC_BATCH, WIDTH), lambda p: (SAMPLE_ROW_BLOCK, col0 + p)),
            pl.BlockSpec((None, CONV_TAPS - 1, DEC_BATCH, WIDTH), lambda p: (layer, 0, 0, p)),
            pl.BlockSpec((None, 8, WIDTH), lambda p: (layer, 0, p)),
        ],
        out_specs=pl.BlockSpec((DEC_BATCH, WIDTH), lambda p: (0, p)),
        out_shape=jax.ShapeDtypeStruct((DEC_BATCH, CONV_WIDTH), F32),
        compiler_params=_params("parallel"),
        name="gdn_conv_sample",
    )(proj, hist, conv_w)


GDN_CHUNK = 128
GDN_ROW_SLOTS = 16


def _gdn_gates_body(ba_ref, par_ref, bg_ref, gcrow_ref, *, rows, chunk):
    ba = ba_ref[...]
    par = par_ref[...]
    g = -jnp.exp(par[0:1]) * _softplus(ba + par[1:2])
    if chunk > 1:
        r = lax.broadcasted_iota(jnp.int32, (rows, rows), 0)
        c = lax.broadcasted_iota(jnp.int32, (rows, rows), 1)
        tri = jnp.where((r >= c) & ((r & -chunk) == (c & -chunk)), 1.0, 0.0)
        g = _dot(tri, g, precision=HIGHEST)
        g_t = g.T
        for cc in range(rows // chunk):
            gcrow_ref[cc * GDN_ROW_SLOTS:(cc + 1) * GDN_ROW_SLOTS, :] = (
                g_t[0:GDN_ROW_SLOTS, cc * chunk:(cc + 1) * chunk])
    lane = lax.broadcasted_iota(jnp.int32, ba.shape, 1)
    bg_ref[...] = jnp.where(lane < HEADS, _sigmoid(ba), g)


def _gdn_gates(proj, gate_par, layer, rows, n_blocks, first_block, chunk):
    out_specs = [pl.BlockSpec((rows, 128), lambda i: (i, 0))]
    out_shape = [jax.ShapeDtypeStruct((rows * n_blocks, 128), F32)]
    if chunk > 1:
        slots = rows // chunk * GDN_ROW_SLOTS
        out_specs.append(pl.BlockSpec((slots, chunk), lambda i: (i, 0)))
        out_shape.append(jax.ShapeDtypeStruct((slots * n_blocks, chunk), F32))
        body = functools.partial(_gdn_gates_body, rows=rows, chunk=chunk)
    else:
        body = functools.partial(_gdn_gates_body, gcrow_ref=None, rows=rows, chunk=chunk)
    return pl.pallas_call(
        body,
        grid=(n_blocks,),
        in_specs=[pl.BlockSpec((rows, 128), lambda i: (first_block + i, C_BA // 128)),
                  pl.BlockSpec((None, 8, 128), lambda i: (layer, 0, 0))],
        out_specs=out_specs,
        out_shape=out_shape,
        compiler_params=_params("parallel"),
        name="gdn_gates",
    )(proj, gate_par)


GDN_BASE = 8


def _gdn_chunk_body(q_ref, k_ref, v_ref, bg_ref, gcrow_ref, z_ref, s0_ref, gn_ref, o_ref, sfin_ref,
                    s_scr, mask_scr, u_scr, w_scr, qk_scr, qd_scr, kd_scr, *, n_tiles, tile_rows):
    ti = pl.program_id(1)
    n_chunks = tile_rows // GDN_CHUNK
    lanes = [slice(h * HEAD_DIM, (h + 1) * HEAD_DIM) for h in range(HEADS)]

    @pl.when(ti == 0)
    def _():
        s_scr[...] = s0_ref[...]

    row = lax.broadcasted_iota(jnp.int32, (GDN_CHUNK, GDN_CHUNK), 0)
    col = lax.broadcasted_iota(jnp.int32, (GDN_CHUNK, GDN_CHUNK), 1)

    def same_block(size):
        return (row & -size) == (col & -size)

    strict = row > col
    eye = jnp.where(row == col, 1.0, 0.0)
    mask_scr[0] = jnp.where(row >= col, 1.0, 0.0)
    mask_scr[1] = jnp.where(strict & same_block(GDN_BASE), 1.0, 0.0)
    merge_sizes = []
    size = GDN_BASE
    while size < GDN_CHUNK:
        mask_scr[2 + len(merge_sizes)] = jnp.where(strict & same_block(2 * size) & ~same_block(size), 1.0, 0.0)
        merge_sizes.append(size)
        size *= 2
    gain = gn_ref[...]

    def prepare(c, carry):
        rows = pl.ds(pl.multiple_of(c * GDN_CHUNK, GDN_CHUNK), GDN_CHUNK)
        gc_by_head = gcrow_ref[pl.ds(pl.multiple_of(c * GDN_ROW_SLOTS + HEADS, HEADS), HEADS), :]
        bg = bg_ref[rows, :]
        tile = (GDN_CHUNK, HEAD_DIM)
        beta_of = lambda h: jnp.broadcast_to(bg[:, h:h + 1], tile)
        gc_of = lambda h: jnp.broadcast_to(bg[:, HEADS + h:HEADS + h + 1], tile)
        heads = range(HEADS)
        m = []
        for h in heads:
            cs = lanes[h]
            k = k_ref[rows, cs]
            gc_lanes = gc_by_head[h:h + 1, :]
            decay = mask_scr[0] * jnp.exp(jnp.minimum(gc_of(h) - gc_lanes, 0.0))
            k16 = k.astype(BF16)
            m.append(_dot_nt((k * beta_of(h)).astype(BF16), k16) * decay)
            qk_scr[rows, cs] = (_dot_nt((q_ref[rows, cs] * GDN_SCALE).astype(BF16), k16) * decay).astype(BF16)
        base = [m[h] * mask_scr[1] for h in heads]
        b16 = [base[h].astype(BF16) for h in heads]
        power = [_dot(b16[h], b16[h]).astype(BF16) for h in heads]
        inv = [eye - base[h] for h in heads]
        inv = [inv[h] + _dot(inv[h].astype(BF16), power[h]) for h in heads]
        for _ in range(int(math.log2(GDN_BASE)) - 2):
            power = [_dot(power[h], power[h]).astype(BF16) for h in heads]
            inv = [inv[h] + _dot(inv[h].astype(BF16), power[h]) for h in heads]
        for n in range(len(merge_sizes)):
            i16 = [inv[h].astype(BF16) for h in heads]
            half = [_dot(i16[h], (m[h] * mask_scr[2 + n]).astype(BF16)).astype(BF16) for h in heads]
            inv = [inv[h] - _dot(half[h], i16[h]) for h in heads]
        for h in heads:
            cs = lanes[h]
            k = k_ref[rows, cs]
            beta = beta_of(h)
            gc = gc_of(h)
            egc = jnp.exp(gc)
            rhs = jnp.concatenate([v_ref[rows, cs] * beta, k * beta * egc], axis=-1)
            sol = rhs + _dot((inv[h] - eye).astype(BF16), rhs.astype(BF16))
            u_scr[rows, cs] = sol[:, :HEAD_DIM]
            w_scr[rows, cs] = sol[:, HEAD_DIM:].astype(BF16)
            gc_last = jnp.broadcast_to(gc[GDN_CHUNK - 1:GDN_CHUNK, :], (GDN_CHUNK, HEAD_DIM))
            qd_scr[rows, cs] = (q_ref[rows, cs] * GDN_SCALE * egc).astype(BF16)
            kd_scr[rows, cs] = (k * jnp.exp(gc_last - gc)).astype(BF16)
        return carry

    lax.fori_loop(0, n_chunks, prepare, 0)

    def advance(c, carry):
        rows = pl.ds(pl.multiple_of(c * GDN_CHUNK, GDN_CHUNK), GDN_CHUNK)
        tail = pl.ds(pl.multiple_of(c * GDN_CHUNK + GDN_CHUNK - 8, 8), 8)
        heads = range(HEADS)
        gc_last = bg_ref[tail, :][7:8, :]
        state = [s_scr[h] for h in heads]
        s16 = [state[h].astype(BF16) for h in heads]
        vn16 = [(u_scr[rows, lanes[h]] - _dot(w_scr[rows, lanes[h]], s16[h])).astype(BF16) for h in heads]
        for h in heads:
            chunk_decay = jnp.exp(jnp.broadcast_to(gc_last[:, HEADS + h:HEADS + h + 1], (1, HEAD_DIM)))
            s_scr[h] = state[h] * chunk_decay + lax.dot_general(
                kd_scr[rows, lanes[h]], vn16[h], (((0,), (0,)), ((), ())), preferred_element_type=F32)
        for h in heads:
            cs = lanes[h]
            o = _dot(qd_scr[rows, cs], s16[h]) + _dot(qk_scr[rows, cs], vn16[h])
            zg = z_ref[rows, cs]
            o_ref[rows, cs] = (_rms(o, gain) * (zg * _sigmoid(zg))).astype(BF16)
        return carry

    lax.fori_loop(0, n_chunks, advance, 0)

    @pl.when(ti == n_tiles - 1)
    def _():
        sfin_ref[...] = s_scr[...]


def _gdn_chunks(qkv, bg, gcrow, z_src, z_col_block, s0, out_norm, layer, n_seq, seq_rows):
    tile_rows = min(seq_rows, GDN_TS)
    n_tiles = seq_rows // tile_rows
    n_masks = 2 + int(math.log2(GDN_CHUNK // GDN_BASE))

    def act(col_block):
        return pl.BlockSpec((tile_rows, WIDTH), lambda b, t: (b * n_tiles + t, col_block))

    slots = tile_rows // GDN_CHUNK * GDN_ROW_SLOTS
    state_spec = pl.BlockSpec((None, HEADS, HEAD_DIM, HEAD_DIM), lambda b, t: (b, 0, 0, 0))
    return pl.pallas_call(
        functools.partial(_gdn_chunk_body, n_tiles=n_tiles, tile_rows=tile_rows),
        grid=(n_seq, n_tiles),
        in_specs=[act(0), act(1), act(2),
                  pl.BlockSpec((tile_rows, 128), lambda b, t: (b * n_tiles + t, 0)),
                  pl.BlockSpec((slots, GDN_CHUNK), lambda b, t: (b * n_tiles + t, 0)),
                  act(z_col_block), state_spec,
                  pl.BlockSpec((None, 1, HEAD_DIM), lambda b, t: (layer, 0, 0))],
        out_specs=[act(0), state_spec],
        out_shape=[jax.ShapeDtypeStruct((n_seq * seq_rows, WIDTH), BF16),
                   jax.ShapeDtypeStruct((n_seq, HEADS, HEAD_DIM, HEAD_DIM), F32)],
        scratch_shapes=[pltpu.VMEM((HEADS, HEAD_DIM, HEAD_DIM), F32),
                        pltpu.VMEM((n_masks, GDN_CHUNK, GDN_CHUNK), F32),
                        pltpu.VMEM((tile_rows, WIDTH), F32),
                        pltpu.VMEM((tile_rows, WIDTH), BF16),
                        pltpu.VMEM((tile_rows, WIDTH), BF16),
                        pltpu.VMEM((tile_rows, WIDTH), BF16),
                        pltpu.VMEM((tile_rows, WIDTH), BF16)],
        compiler_params=_params("parallel", "arbitrary"),
        name="gdn_chunks",
    )(qkv, qkv, qkv, bg, gcrow, z_src, s0, out_norm)


GDN_STEP_COLS = CONV_WIDTH + 128 + WIDTH


def _gdn_step_body(x_ref, s0_ref, gn_ref, o_ref, s_ref):
    heads = range(HEADS)
    col = lambda part, h: slice(part * WIDTH + h * HEAD_DIM, part * WIDTH + (h + 1) * HEAD_DIM)
    gain = gn_ref[...]
    bg = x_ref[:, CONV_WIDTH:CONV_WIDTH + 128]
    z0 = CONV_WIDTH + 128
    k = [x_ref[:, col(1, h)] for h in heads]
    q = [x_ref[:, col(0, h)] * GDN_SCALE for h in heads]
    k16 = [k[h].astype(BF16) for h in heads]
    state = [s0_ref[h] for h in heads]
    s16 = [state[h].astype(BF16) for h in heads]
    k_s = [_dot(k16[h], s16[h]) for h in heads]
    q_s = [_dot(q[h].astype(BF16), s16[h]) for h in heads]
    decay = [jnp.exp(jnp.broadcast_to(bg[:, HEADS + h:HEADS + h + 1], (8, HEAD_DIM))) for h in heads]
    beta = [jnp.broadcast_to(bg[:, h:h + 1], (8, HEAD_DIM)) for h in heads]
    v_new = [beta[h] * (x_ref[:, col(2, h)] - decay[h] * k_s[h]) for h in heads]
    for h in heads:
        s_ref[h] = state[h] * decay[h][0:1, :] + lax.dot_general(
            k16[h], v_new[h].astype(BF16), (((0,), (0,)), ((), ())), preferred_element_type=F32)
    for h in heads:
        qk = jnp.sum(q[h].astype(BF16).astype(F32) * k16[h].astype(F32), axis=-1, keepdims=True)
        o = decay[h] * q_s[h] + qk * v_new[h].astype(BF16).astype(F32)
        zg = x_ref[:, z0 + h * HEAD_DIM:z0 + (h + 1) * HEAD_DIM]
        o_ref[:, h * HEAD_DIM:(h + 1) * HEAD_DIM] =(_rms(o, gain) * (zg * _sigmoid(zg))).astype(BF16)


def _gdn_step(x8, state, out_norm, layer):
    return pl.pallas_call(
        _gdn_step_body,
        grid=(DEC_BATCH,),
        in_specs=[pl.BlockSpec((None, 8, GDN_STEP_COLS), lambda b: (b, 0, 0)),
                  pl.BlockSpec((None, None, HEADS, HEAD_DIM, HEAD_DIM), lambda b: (layer, b, 0, 0, 0)),
                  pl.BlockSpec((None, 1, HEAD_DIM), lambda b: (layer, 0, 0))],
        out_specs=[pl.BlockSpec((None, 8, WIDTH), lambda b: (b, 0, 0)),
                   pl.BlockSpec((None, HEADS, HEAD_DIM, HEAD_DIM), lambda b: (b, 0, 0, 0))],
        out_shape=[jax.ShapeDtypeStruct((DEC_BATCH, 8, WIDTH), BF16),
                   jax.ShapeDtypeStruct((DEC_BATCH, HEADS, HEAD_DIM, HEAD_DIM), F32)],
        compiler_params=_params("parallel"),
        name="gdn_step",
    )(x8, state, out_norm)


def _pack_body(a_ref, b_ref, o_ref):
    n = pl.program_id(1)
    first_gate = C_GATE // PACK_TN
    last = PACK_TILES - 1
    shift = 2 * HEADS
    a = a_ref[...]
    shifted = jnp.concatenate([a[shift:, :], b_ref[...]], axis=0)
    row = lax.broadcasted_iota(jnp.int32, a.shape, 0)
    logits = jnp.where(row < shift, a, 0.0)
    tile = jnp.where(n < first_gate, a, jnp.where(n == last, logits, shifted))
    o_ref[...] = tile.T.astype(BF16)


def _pack_w_in(w_in):
    first_gate = C_GATE // PACK_TN
    last = PACK_TILES - 1
    shift = 2 * HEADS
    w_t = jnp.swapaxes(w_in, 1, 2)

    def a_index(d, n):
        return (d, jnp.where(n == last, first_gate, n), 0)

    def b_index(d, n):
        return (d, jnp.where((n >= first_gate) & (n < last), (n + 1) * (PACK_TN // shift), 0), 0)

    return pl.pallas_call(
        _pack_body,
        grid=(DEPTH, PACK_TILES),
        in_specs=[pl.BlockSpec((None, PACK_TN, D_MODEL), a_index),
                  pl.BlockSpec((None, shift, D_MODEL), b_index)],
        out_specs=pl.BlockSpec((None, D_MODEL, PACK_TN), lambda d, n: (d, 0, n)),
        out_shape=jax.ShapeDtypeStruct((DEPTH, D_MODEL, PROJ_COLS), BF16),
        compiler_params=_params("parallel", "arbitrary"),
        name="pack_w_in",
    )(w_t, w_t)


def _s5_params(a_re, a_im, log_dt, b_re, b_im, c_re, c_im, d):
    eye = jnp.eye(S5_GB, dtype=F32)

    def state_rows(x):
        return x.reshape(DEPTH, S5_NGB, 1, S5_ST)

    par = jnp.concatenate(
        [state_rows(a_re), state_rows(a_im),
         state_rows(jnp.repeat(log_dt, SSM_STATE, axis=-1)),
         jnp.zeros((DEPTH, S5_NGB, 5, S5_ST), F32)], axis=2)

    def b_blockdiag(b):
        b = b.reshape(DEPTH, S5_NGB, S5_GB, SSM_STATE, SSM_GROUP)
        return jnp.einsum('dbgpc,gh->dbgchp', b, eye).reshape(DEPTH, S5_NGB, S5_CH, S5_ST)

    def c_blockdiag(c):
        c = c.reshape(DEPTH, S5_NGB, S5_GB, SSM_GROUP, SSM_STATE)
        return jnp.einsum('dbgcp,gh->dbgphc', c, eye).reshape(DEPTH, S5_NGB, S5_ST, S5_CH)

    return (par, b_blockdiag(b_re), b_blockdiag(b_im), c_blockdiag(c_re), c_blockdiag(c_im),
            d.reshape(DEPTH, S5_NGB, 1, S5_CH))


def _segment_major(x):
    w = x.shape[-1]
    return x.reshape(BATCH, S5_LANES, S5_SEG, w).transpose(0, 2, 1, 3).reshape(N_PROMPT, w)


def _token_major(x):
    w = x.shape[-1]
    return x.reshape(BATCH, S5_SEG, S5_LANES, w).transpose(0, 2, 1, 3).reshape(N_PROMPT, w)


def _pad_rows(x):
    return jnp.concatenate([x, jnp.zeros((ROWS - x.shape[0], x.shape[1]), x.dtype)], axis=0)


def kernel(x_prompt, x_sample, cache_k, cache_v, page_table, state_ssm, state_conv, state_gdn, ffn1_norm, ffn1_w_gate, ffn1_w_up, ffn1_w_down, mix_norm, w_in, sb_bias, ssm_a_re, ssm_a_im, ssm_log_dt, ssm_b_re, ssm_b_im, ssm_c_re, ssm_c_im, ssm_d, ssm_w_glu, gdn_conv_w, gdn_a_log, gdn_dt_bias, gdn_out_norm, w_branch, w_out, ffn2_norm, ffn2_w_gate, ffn2_w_up, ffn2_w_down, final_norm):
    bf = lambda w: w.astype(BF16)
    gain3 = lambda g: g.reshape(DEPTH, 1, D_MODEL)
    w_in_p = _pack_w_in(w_in)
    ffn1 = (gain3(ffn1_norm), bf(ffn1_w_gate), bf(ffn1_w_up), bf(ffn1_w_down))
    ffn2 = (gain3(ffn2_norm), bf(ffn2_w_gate), bf(ffn2_w_up), bf(ffn2_w_down))
    mix_gain = gain3(mix_norm)
    w_glu, w_br, w_o = bf(ssm_w_glu), bf(w_branch).reshape(DEPTH, 3, WIDTH, D_MODEL), bf(w_out)
    s5p = _s5_params(ssm_a_re, ssm_a_im, ssm_log_dt, ssm_b_re, ssm_b_im, ssm_c_re, ssm_c_im, ssm_d)
    bias_flat = sb_bias.reshape(DEPTH * HEADS)
    bias_lanes = jnp.tile(sb_bias, (1, PAGE)).reshape(DEPTH, 1, DEC_LANES)
    page_flat = page_table.reshape(DEC_BATCH * N_PAGES)
    conv_w8 = jnp.pad(gdn_conv_w, ((0, 0), (0, 8 - CONV_TAPS), (0, 0)))
    conv_hist = state_conv.transpose(0, 2, 1, 3)
    lane_pad = jnp.zeros((DEPTH, 128 - 2 * HEADS), F32)
    gate_par = jnp.stack([jnp.concatenate([jnp.zeros((DEPTH, HEADS), F32), gdn_a_log, lane_pad], axis=1),
                          jnp.concatenate([jnp.zeros((DEPTH, HEADS), F32), gdn_dt_bias, lane_pad], axis=1)]
                         + [jnp.zeros((DEPTH, 128), F32)] * 6, axis=1)
    out_norm = gdn_out_norm.reshape(DEPTH, 1, HEAD_DIM)
    ssm_h0r = state_ssm[..., 0].reshape(DEPTH, DEC_BATCH, WIDTH * SSM_STATE // SSM_GROUP)
    ssm_h0i = state_ssm[..., 1].reshape(DEPTH, DEC_BATCH, WIDTH * SSM_STATE // SSM_GROUP)

    x = _pad_rows(jnp.concatenate([x_prompt.reshape(N_PROMPT, D_MODEL),
                                   x_sample.reshape(DEC_BATCH, D_MODEL)], axis=0))
    sample = slice(N_PROMPT, N_PROMPT + DEC_BATCH)
    ks, vs, sp, ss, cp, cs_, gp, gs = ([] for _ in range(8))
    k_all = jnp.zeros((DEPTH, N_PROMPT, WIDTH), F32)
    v_all = jnp.zeros((DEPTH, N_PROMPT, WIDTH), F32)
    for l in range(DEPTH):
        x = _ffn(x, *ffn1, l)
        proj = _in_proj(x, mix_gain, w_in_p, l)

        oa_p, k_all, v_all = _sb_prompt(proj, bias_flat, l, k_all, v_all)
        q_s = proj[sample, C_Q:C_Q + WIDTH].reshape(DEC_BATCH, HEADS, HEAD_DIM)
        oa_s = _sb_decode(q_s, cache_k, cache_v, page_flat, bias_lanes, l)
        oa = _pad_rows(jnp.concatenate([oa_p, oa_s.reshape(DEC_BATCH, WIDTH).astype(BF16)], axis=0))

        gb_p, hr_p, hi_p = _s5_prompt(_segment_major(proj[:N_PROMPT, C_U:C_U + WIDTH]), s5p, l)
        gb_s, hr_s, hi_s = _s5_sample(proj, s5p, ssm_h0r[l], ssm_h0i[l], l)
        ob = _glu(_pad_rows(jnp.concatenate([_token_major(gb_p), gb_s], axis=0)), w_glu, l)

        qkv_p = _conv_prompt(proj, conv_w8, l)
        bg_p, gcrow_p = _gdn_gates(proj, gate_par, l, GDN_TS, N_PROMPT // GDN_TS, 0, GDN_CHUNK)
        oc_p, gdn_p = _gdn_chunks(qkv_p, bg_p, gcrow_p, proj, C_Z // WIDTH,
                                  jnp.zeros((BATCH, HEADS, HEAD_DIM, HEAD_DIM), F32), out_norm, l, BATCH, SEQ)
        qkv_s = _conv_sample(proj, conv_hist, conv_w8, l)
        (bg_s,) = _gdn_gates(proj, gate_par, l, DEC_BATCH, 1, SAMPLE_ROW_BLOCK, 1)
        step_in = jnp.concatenate([qkv_s, bg_s, proj[sample, C_Z:C_Z + WIDTH]], axis=1)
        oc_s, gdn_s = _gdn_step(jnp.pad(step_in[:, None, :], ((0, 0), (0, 7), (0, 0))), state_gdn, out_norm, l)
        oc = _pad_rows(jnp.concatenate([oc_p, oc_s[:, 0, :]], axis=0))

        x = _merge(x, oa, ob, oc, proj, w_br, w_o, l)
        x = _ffn(x, *ffn2, l)

        ks.append(proj[sample, C_K:C_K + WIDTH].reshape(DEC_BATCH, 1, HEADS, HEAD_DIM))
        vs.append(proj[sample, C_V:C_V + WIDTH].reshape(DEC_BATCH, 1, HEADS, HEAD_DIM))
        state = lambda r, i, n: jnp.stack([r.reshape(n, WIDTH // SSM_GROUP, SSM_STATE),
                                           i.reshape(n, WIDTH // SSM_GROUP, SSM_STATE)], axis=-1)
        sp.append(state(hr_p, hi_p, BATCH))
        ss.append(state(hr_s, hi_s, DEC_BATCH))
        qkv_cols = slice(C_QKV, C_QKV + CONV_WIDTH)
        cp.append(jnp.stack([proj[(b + 1) * SEQ - (CONV_TAPS - 1):(b + 1) * SEQ, qkv_cols] for b in range(BATCH)]))
        cs_.append(jnp.concatenate([state_conv[l][:, 1:], proj[sample, qkv_cols][:, None, :]], axis=1))
        gp.append(gdn_p)
        gs.append(gdn_s)

    gain = final_norm.reshape(1, D_MODEL)
    y_p = _final_norm(x, gain, GDN_TS, N_PROMPT // GDN_TS, 0)
    y_s = _final_norm(x, gain, DEC_BATCH, 1, SAMPLE_ROW_BLOCK)
    kv_shape = (DEPTH, BATCH, SEQ, HEADS, HEAD_DIM)
    return (y_p.reshape(BATCH, SEQ, D_MODEL), y_s.reshape(DEC_BATCH, 1, D_MODEL),
            k_all.reshape(kv_shape), v_all.reshape(kv_shape), jnp.stack(ks), jnp.stack(vs),
            jnp.stack(sp), jnp.stack(ss), jnp.stack(cp), jnp.stack(cs_),
            jnp.stack(gp), jnp.stack(gs))
```

```python
import functools
import math

import jax
import jax.numpy as jnp
from jax import lax
from jax.experimental import pallas as pl
from jax.experimental.pallas import tpu as pltpu

F32 = jnp.float32
BF16 = jnp.bfloat16
HIGHEST = lax.Precision.HIGHEST

D_MODEL = 2048
BATCH = 4
SEQ = 2048
DEPTH = 2
DEC_BATCH = 32
PAGE = 128
N_PAGES = 64
HEADS = 8
HEAD_DIM = 128
WIDTH = HEADS * HEAD_DIM
SSM_GROUP = 16
SSM_STATE = 64
CONV_TAPS = 4
CONV_WIDTH = 3 * WIDTH
FFN_DIM = 5632
EPS = 1e-6

N_PROMPT = BATCH * SEQ
ROWS = 8320
TM = ROWS // 13
TM_GLU = ROWS // 20
SAMPLE_ROW_BLOCK = N_PROMPT // DEC_BATCH

C_Q, C_K, C_V, C_U, C_QKV, C_Z, C_GATE, C_BA = 0, 1024, 2048, 3072, 4096, 7168, 8192, 14336
PROJ_COLS = 14592
TN_PROJ = 768
PACK_TN = 512
PACK_TILES = -(-PROJ_COLS // PACK_TN)
TF = 512
TM_BIG = ROWS // 8
TM_MERGE = ROWS // 52

VMEM_LIMIT = 56 * 1024 * 1024


def _params(*sem):
    return pltpu.CompilerParams(dimension_semantics=sem, vmem_limit_bytes=VMEM_LIMIT)


def _rms(x, gain):
    ms = jnp.mean(x * x, axis=-1, keepdims=True)
    return x * lax.rsqrt(ms + EPS) * gain


def _softplus(x):
    return jnp.maximum(x, 0.0) + jnp.log(1.0 + jnp.exp(-jnp.abs(x)))


def _sigmoid(x):
    return 1.0 / (1.0 + jnp.exp(-x))


def _dot(a, b, **kw):
    return jnp.dot(a, b, preferred_element_type=F32, **kw)


def _dot_nt(a, b, **kw):
    return lax.dot_general(a, b, (((1,), (1,)), ((), ())), preferred_element_type=F32, **kw)


def _ffn_body(x_ref, g_ref, wg_ref, wu_ref, wd_ref, o_ref, h_scr):
    @pl.when(pl.program_id(1) == 0)
    def _():
        x = x_ref[...]
        h_scr[...] = _rms(x, g_ref[...]).astype(BF16)
        o_ref[...] = x

    h = h_scr[...]
    a = _dot(h, wg_ref[...])
    b = _dot(h, wu_ref[...])
    act = (a * _sigmoid(a)) * b * 0.5
    o_ref[...] += _dot(act.astype(BF16), wd_ref[...])


def _ffn(x, gain, wg, wu, wd, layer):
    return pl.pallas_call(
        _ffn_body,
        grid=(ROWS // TM, FFN_DIM // TF),
        in_specs=[
            pl.BlockSpec((TM, D_MODEL), lambda i, f: (i, 0)),
            pl.BlockSpec((None, 1, D_MODEL), lambda i, f: (layer, 0, 0)),
            pl.BlockSpec((None, D_MODEL, TF), lambda i, f: (layer, 0, f)),
            pl.BlockSpec((None, D_MODEL, TF), lambda i, f: (layer, 0, f)),
            pl.BlockSpec((None, TF, D_MODEL), lambda i, f: (layer, f, 0)),
        ],
        out_specs=pl.BlockSpec((TM, D_MODEL), lambda i, f: (i, 0)),
        out_shape=jax.ShapeDtypeStruct((ROWS, D_MODEL), F32),
        scratch_shapes=[pltpu.VMEM((TM, D_MODEL), BF16)],
        compiler_params=_params("parallel", "arbitrary"),
        name="ffn",
    )(x, gain, wg, wu, wd)


def _proj_body(x_ref, g_ref, w_ref, o_ref, h_scr):
    @pl.when(pl.program_id(1) == 0)
    def _():
        h_scr[...] = _rms(x_ref[...], g_ref[...]).astype(BF16)

    o_ref[...] = _dot(h_scr[...], w_ref[...])


def _in_proj(x, gain, w, layer):
    return pl.pallas_call(
        _proj_body,
        grid=(ROWS // TM_BIG, PROJ_COLS // TN_PROJ),
        in_specs=[
            pl.BlockSpec((TM_BIG, D_MODEL), lambda i, n: (i, 0), pipeline_mode=pl.Buffered(1)),
            pl.BlockSpec((None, 1, D_MODEL), lambda i, n: (layer, 0, 0)),
            pl.BlockSpec((None, D_MODEL, TN_PROJ), lambda i, n: (layer, 0, n)),
        ],
        out_specs=pl.BlockSpec((TM_BIG, TN_PROJ), lambda i, n: (i, n)),
        out_shape=jax.ShapeDtypeStruct((ROWS, PROJ_COLS), F32),
        scratch_shapes=[pltpu.VMEM((TM_BIG, D_MODEL), BF16)],
        compiler_params=_params("parallel", "arbitrary"),
        name="in_proj",
    )(x, gain, w)


def _glu_body(g_ref, w_ref, o_ref):
    g = g_ref[...]
    o_ref[...] = (g * _sigmoid(_dot(g.astype(BF16), w_ref[...]))).astype(BF16)


def _glu(g, w, layer):
    return pl.pallas_call(
        _glu_body,
        grid=(ROWS // TM_GLU,),
        in_specs=[
            pl.BlockSpec((TM_GLU, WIDTH), lambda i: (i, 0)),
            pl.BlockSpec((None, WIDTH, WIDTH), lambda i: (layer, 0, 0)),
        ],
        out_specs=pl.BlockSpec((TM_GLU, WIDTH), lambda i: (i, 0)),
        out_shape=jax.ShapeDtypeStruct((ROWS, WIDTH), BF16),
        compiler_params=_params("parallel"),
        name="glu",
    )(g, w)


def _merge_body(x_ref, oa_ref, ob_ref, oc_ref, ga_ref, gb_ref, gc_ref, wa_ref, wb_ref, wc_ref, wo_ref, o_ref):
    m = (_sigmoid(ga_ref[...]) * _dot(oa_ref[...], wa_ref[...])
         + _sigmoid(gb_ref[...]) * _dot(ob_ref[...], wb_ref[...])
         + _sigmoid(gc_ref[...]) * _dot(oc_ref[...], wc_ref[...]))
    o_ref[...] = x_ref[...] + _dot(m.astype(BF16), wo_ref[...])


def _merge(x, oa, ob, oc, proj, w_branch, w_out, layer):
    gate_blk = C_GATE // D_MODEL
    o_spec = pl.BlockSpec((TM_MERGE, WIDTH), lambda i: (i, 0))

    def gate_spec(which):
        return pl.BlockSpec((TM_MERGE, D_MODEL), lambda i: (i, gate_blk + which))

    def w_spec(which):
        return pl.BlockSpec((None, None, WIDTH, D_MODEL), lambda i: (layer, which, 0, 0),
                            pipeline_mode=pl.Buffered(1))

    return pl.pallas_call(
        _merge_body,
        grid=(ROWS // TM_MERGE,),
        in_specs=[
            pl.BlockSpec((TM_MERGE, D_MODEL), lambda i: (i, 0)),
            o_spec, o_spec, o_spec,
            gate_spec(0), gate_spec(1), gate_spec(2),
            w_spec(0), w_spec(1), w_spec(2),
            pl.BlockSpec((None, D_MODEL, D_MODEL), lambda i: (layer, 0, 0), pipeline_mode=pl.Buffered(1)),
        ],
        out_specs=pl.BlockSpec((TM_MERGE, D_MODEL), lambda i: (i, 0)),
        out_shape=jax.ShapeDtypeStruct((ROWS, D_MODEL), F32),
        compiler_params=_params("parallel"),
        name="merge",
    )(x, oa, ob, oc, proj, proj, proj, w_branch, w_branch, w_branch, w_out)


def _final_norm_body(x_ref, g_ref, o_ref):
    o_ref[...] = _rms(x_ref[...], g_ref[...])


def _final_norm(x, gain, rows, n_blocks, first_block):
    return pl.pallas_call(
        _final_norm_body,
        grid=(n_blocks,),
        in_specs=[pl.BlockSpec((rows, D_MODEL), lambda i: (first_block + i, 0)),
                  pl.BlockSpec((1, D_MODEL), lambda i: (0, 0))],
        out_specs=pl.BlockSpec((rows, D_MODEL), lambda i: (i, 0)),
        out_shape=jax.ShapeDtypeStruct((rows * n_blocks, D_MODEL), F32),
        compiler_params=_params("parallel"),
        name="final_norm",
    )(x, gain)


SB_BQ = 256
SB_BK = 256
SB_HB = 4
SB_SCALE = HEAD_DIM ** -0.5


def _stick_terms(z):
    t = jnp.log(1.0 + jnp.exp(-jnp.abs(z)))
    return jnp.minimum(z, 0.0) - t, jnp.minimum(-z, 0.0) - t


LOG2E = 1.0 / math.log(2.0)


def _sb_prompt_body(bias_ref, q_ref, k_ref, v_ref, *rest, layer):
    o_ref, k_all_ref, v_all_ref, q_scr, acc_scr, run_scr = rest[2:]
    hg = pl.program_id(1)
    qi = pl.program_id(2)

    @pl.when(qi == 0)
    def _():
        k_all_ref[...] = k_ref[...]
        v_all_ref[...] = v_ref[...]

    heads = range(SB_HB)
    lanes = [slice(hh * HEAD_DIM, (hh + 1) * HEAD_DIM) for hh in heads]
    row = lax.broadcasted_iota(jnp.int32, (SB_BQ, SB_BK), 0)
    col = lax.broadcasted_iota(jnp.int32, (SB_BQ, SB_BK), 1)
    q_pos = qi * SB_BQ + row
    r2 = lax.broadcasted_iota(jnp.int32, (SB_BK, SB_BK), 0)
    c2 = lax.broadcasted_iota(jnp.int32, (SB_BK, SB_BK), 1)
    later_key = jnp.where(r2 > c2, 1.0, 0.0).astype(BF16)
    bias2 = [bias_ref[layer * HEADS + hg * SB_HB + hh] * LOG2E for hh in heads]

    for hh in heads:
        q_scr[hh] = (q_ref[:, lanes[hh]] * (SB_SCALE * LOG2E)).astype(BF16)
    acc_scr[...] = jnp.zeros_like(acc_scr)
    run_scr[...] = jnp.zeros_like(run_scr)

    def visit(blocks, masked):
        units = [(n, hh) for n in range(len(blocks)) for hh in heads]
        chains = range(len(units))
        k_rows = [pl.ds(pl.multiple_of(j * SB_BK, SB_BK), SB_BK) for j in blocks]
        mask = (blocks[0] * SB_BK + col) < q_pos
        z = [_dot_nt(q_scr[hh], k_ref[k_rows[n], lanes[hh]].astype(BF16)) + bias2[hh] for n, hh in units]
        t = [jnp.log(1.0 + jnp.exp2(-jnp.abs(z[i]))) * LOG2E for i in chains]
        log_hit = [jnp.minimum(z[i], 0.0) - t[i] for i in chains]
        log_fail = [jnp.minimum(-z[i], 0.0) - t[i] for i in chains]
        if masked:
            log_fail = [jnp.where(mask, log_fail[i], 0.0) for i in chains]
        suffix = [_dot(log_fail[i].astype(BF16), later_key) for i in chains]
        w = []
        run = [run_scr[hh] for hh in heads]
        for i, (n, hh) in enumerate(units):
            w_i = jnp.exp2(log_hit[i] + suffix[i] + run[hh])
            block_total = suffix[i][:, 0:1] + log_fail[i][:, 0:1]
            run[hh] = run[hh] + jnp.broadcast_to(block_total, (SB_BQ, SB_BK))
            w.append(jnp.where(mask, w_i, 0.0) if masked else w_i)
        for hh in heads:
            run_scr[hh] = run[hh]
        for hh in heads:
            acc = acc_scr[hh]
            for i, (n, h2) in enumerate(units):
                if h2 == hh:
                    acc = acc + _dot(w[i].astype(BF16), v_ref[k_rows[n], lanes[hh]].astype(BF16))
            acc_scr[hh] = acc

    visit([qi], True)

    def below_diagonal(it, carry):
        visit([qi - 1 - 2 * it, qi - 2 - 2 * it], False)
        return carry

    lax.fori_loop(0, qi // 2, below_diagonal, 0)

    @pl.when(qi % 2 == 1)
    def _():
        visit([0], False)
    for hh in heads:
        o_ref[:, lanes[hh]] = acc_scr[hh].astype(BF16)


def _sb_prompt(proj, bias_flat, layer, k_all, v_all):
    hw = SB_HB * HEAD_DIM
    n_hg = HEADS // SB_HB
    nq = SEQ // SB_BQ
    all_spec = pl.BlockSpec((None, SEQ, hw), lambda b, h, i, bias: (layer, b, h))
    all_shape = jax.ShapeDtypeStruct((DEPTH, N_PROMPT, WIDTH), F32)
    grid_spec = pltpu.PrefetchScalarGridSpec(
        num_scalar_prefetch=1,
        grid=(BATCH, n_hg, nq),
        in_specs=[
            pl.BlockSpec((SB_BQ, hw), lambda b, h, i, bias: (b * nq + i, C_Q // hw + h)),
            pl.BlockSpec((SEQ, hw), lambda b, h, i, bias: (b, C_K // hw + h)),
            pl.BlockSpec((SEQ, hw), lambda b, h, i, bias: (b, C_V // hw + h)),
            pl.BlockSpec(memory_space=pl.ANY),
            pl.BlockSpec(memory_space=pl.ANY),
        ],
        out_specs=[pl.BlockSpec((SB_BQ, hw), lambda b, h, i, bias: (b * nq + i, h)), all_spec, all_spec],
        scratch_shapes=[pltpu.VMEM((SB_HB, SB_BQ, HEAD_DIM), BF16),
                        pltpu.VMEM((SB_HB, SB_BQ, HEAD_DIM), F32),
                        pltpu.VMEM((SB_HB, SB_BQ, SB_BK), F32)],
    )
    return pl.pallas_call(
        functools.partial(_sb_prompt_body, layer=layer),
        grid_spec=grid_spec,
        out_shape=[jax.ShapeDtypeStruct((N_PROMPT, WIDTH), BF16), all_shape, all_shape],
        input_output_aliases={4: 1, 5: 2},
        compiler_params=_params("parallel", "parallel", "arbitrary"),
        name="sb_prompt",
    )(bias_flat, proj, proj, proj, k_all, v_all)


DEC_PAGES = 16
DEC_STEPS = N_PAGES // DEC_PAGES
DEC_LANES = PAGE * HEADS


def _lane_shift_up(x, s):
    n = x.shape[1]
    lane = lax.broadcasted_iota(jnp.int32, x.shape, 1)
    return jnp.where(lane < n - s, pltpu.roll(x, n - s, 1), 0.0)


def _sb_decode_body(pt_ref, q_ref, bias_ref, *refs):
    k_refs = refs[:DEC_PAGES]
    v_refs = refs[DEC_PAGES:2 * DEC_PAGES]
    o_ref, acc_scr, run_scr = refs[2 * DEC_PAGES:]
    g = pl.program_id(1)

    @pl.when(g == 0)
    def _():
        acc_scr[...] = jnp.zeros_like(acc_scr)
        run_scr[...] = jnp.zeros_like(run_scr)

    sub = lax.broadcasted_iota(jnp.int32, (HEADS, DEC_LANES), 0)
    lane = lax.broadcasted_iota(jnp.int32, (HEADS, DEC_LANES), 1)
    own_head = (lane & (HEADS - 1)) == sub
    q = q_ref[...].astype(BF16)
    rows = []
    for i in range(DEC_PAGES):
        k2 = k_refs[i][...].reshape(DEC_LANES, HEAD_DIM).astype(BF16)
        zt = _dot_nt(q, k2)
        rows.append(jnp.sum(jnp.where(own_head, zt, 0.0), axis=0, keepdims=True))
    z = jnp.concatenate(rows, axis=0) * SB_SCALE + bias_ref[...]
    log_hit, log_fail = _stick_terms(z)

    suffix = log_fail
    for s in (HEADS, 4 * HEADS):
        suffix = (suffix + _lane_shift_up(suffix, s)) + (_lane_shift_up(suffix, 2 * s) + _lane_shift_up(suffix, 3 * s))
    s = 16 * HEADS
    while s < DEC_LANES:
        suffix = suffix + jnp.concatenate([suffix[:, s:], jnp.zeros((DEC_PAGES, s), F32)], axis=1)
        s *= 2
    tile_lane = lax.broadcasted_iota(jnp.int32, (DEC_PAGES, 16 * HEADS), 1)
    tot = jnp.where(tile_lane < HEADS, suffix[:, :16 * HEADS], 0.0)
    for s in (HEADS, 4 * HEADS):
        tot = (tot + pltpu.roll(tot, s, 1)) + (pltpu.roll(tot, 2 * s, 1) + pltpu.roll(tot, 3 * s, 1))
    tot = jnp.concatenate([tot] * (DEC_LANES // (16 * HEADS)), axis=1)
    page_r = lax.broadcasted_iota(jnp.int32, (DEC_PAGES, DEC_PAGES), 0)
    page_c = lax.broadcasted_iota(jnp.int32, (DEC_PAGES, DEC_PAGES), 1)
    later = jnp.where(page_c > page_r, 1.0, 0.0)
    after = _dot(later, tot, precision=HIGHEST)
    run = run_scr[...]
    log_between = (suffix - log_fail) + after + run
    run_scr[...] = run + jnp.sum(tot, axis=0, keepdims=True)
    w = jnp.exp(log_hit + log_between)

    acc = acc_scr[...]
    for i in range(DEC_PAGES):
        wm = jnp.where(own_head, jnp.broadcast_to(w[i:i + 1, :], (HEADS, DEC_LANES)), 0.0).astype(BF16)
        v2 = v_refs[i][...].reshape(DEC_LANES, HEAD_DIM).astype(BF16)
        acc = acc + _dot(wm, v2)
    acc_scr[...] = acc

    @pl.when(g == DEC_STEPS - 1)
    def _():
        o_ref[...] = acc


def _sb_decode(q, cache_k, cache_v, page_table_flat, bias_lanes, layer):
    def page_spec(i):
        def index(b, g, pt):
            return (layer, pt[b * N_PAGES + (DEC_STEPS - 1 - g) * DEC_PAGES + i], 0, 0, 0)
        return pl.BlockSpec((None, None, PAGE, HEADS, HEAD_DIM), index)

    grid_spec = pltpu.PrefetchScalarGridSpec(
        num_scalar_prefetch=1,
        grid=(DEC_BATCH, DEC_STEPS),
        in_specs=[pl.BlockSpec((None, HEADS, HEAD_DIM), lambda b, g, pt: (b, 0, 0)),
                  pl.BlockSpec((None, 1, DEC_LANES), lambda b, g, pt: (layer, 0, 0))]
        + [page_spec(i) for i in range(DEC_PAGES)] * 2,
        out_specs=pl.BlockSpec((None, HEADS, HEAD_DIM), lambda b, g, pt: (b, 0, 0)),
        scratch_shapes=[pltpu.VMEM((HEADS, HEAD_DIM), F32), pltpu.VMEM((1, DEC_LANES), F32)],
    )
    return pl.pallas_call(
        _sb_decode_body,
        grid_spec=grid_spec,
        out_shape=jax.ShapeDtypeStruct((DEC_BATCH, HEADS, HEAD_DIM), F32),
        compiler_params=_params("parallel", "arbitrary"),
        name="sb_decode",
    )(page_table_flat, q, bias_lanes, *([cache_k] * DEC_PAGES), *([cache_v] * DEC_PAGES))


S5_GB = 8
S5_CH = S5_GB * SSM_GROUP
S5_ST = S5_GB * SSM_STATE
S5_NGB = WIDTH // S5_CH
S5_LANES = 8
S5_SEG = SEQ // S5_LANES


def _s5_discretize(par):
    ar, ai, ldt = par[0:1], par[1:2], par[2:3]
    dt = jnp.exp(ldt)
    mag = jnp.exp(dt * ar)
    abr = mag * jnp.cos(dt * ai)
    abi = mag * jnp.sin(dt * ai)
    den = ar * ar + ai * ai
    cr = ((abr - 1.0) * ar + abi * ai) / den
    ci = (abi * ar - (abr - 1.0) * ai) / den
    return abr, abi, cr, ci


def _gelu(y):
    return 0.5 * y * (1.0 + jnp.tanh(math.sqrt(2.0 / math.pi) * (y + 0.044715 * (y * y * y))))


def _s5_prompt_body(u_ref, par_ref, bre_ref, bim_ref, cre_ref, cim_ref, d_ref, h0r_ref, h0i_ref,
                    o_ref, hr_ref, hi_ref, xr_scr, xi_scr):
    abr, abi, cr, ci = _s5_discretize(par_ref[...])
    bre, bim = bre_ref[...], bim_ref[...]
    bbr = (cr * bre - ci * bim).astype(BF16)
    bbi = (cr * bim + ci * bre).astype(BF16)
    u = u_ref[...]
    ub = u.astype(BF16)
    xr_scr[...] = _dot(ub, bbr)
    xi_scr[...] = _dot(ub, bbi)

    shape = (S5_LANES, S5_ST)
    ar8 = jnp.broadcast_to(abr, shape)
    ai8 = jnp.broadcast_to(abi, shape)

    def local_step(t, carry):
        xr, xi = carry
        rows = pl.ds(pl.multiple_of(t * S5_LANES, S5_LANES), S5_LANES)
        nr = ar8 * xr - ai8 * xi + xr_scr[rows, :]
        ni = ar8 * xi + ai8 * xr + xi_scr[rows, :]
        xr_scr[rows, :] = nr
        xi_scr[rows, :] = ni
        return nr, ni

    zero = jnp.zeros(shape, F32)
    fr, fi = lax.fori_loop(0, S5_SEG, local_step, (zero, zero), unroll=8)

    pr, pi = abr, abi
    for _ in range(int(math.log2(S5_SEG))):
        pr, pi = pr * pr - pi * pi, 2.0 * pr * pi
    hr, hi = h0r_ref[...], h0i_ref[...]
    in_r, in_i = [], []
    for j in range(S5_LANES):
        in_r.append(hr)
        in_i.append(hi)
        hr, hi = fr[j:j + 1] + pr * hr - pi * hi, fi[j:j + 1] + pr * hi + pi * hr
    hr_ref[...] = hr
    hi_ref[...] = hi
    cin_r = jnp.concatenate(in_r, axis=0)
    cin_i = jnp.concatenate(in_i, axis=0)

    def fix_step(t, carry):
        wr, wi = carry
        rows = pl.ds(pl.multiple_of(t * S5_LANES, S5_LANES), S5_LANES)
        xr_scr[rows, :] = xr_scr[rows, :] + wr * cin_r - wi * cin_i
        xi_scr[rows, :] = xi_scr[rows, :] + wr * cin_i + wi * cin_r
        return wr * ar8 - wi * ai8, wr * ai8 + wi * ar8

    lax.fori_loop(0, S5_SEG, fix_step, (ar8, ai8), unroll=8)

    y = (_dot(xr_scr[...].astype(BF16), cre_ref[...].astype(BF16))
         - _dot(xi_scr[...].astype(BF16), cim_ref[...].astype(BF16)) + d_ref[...] * u)
    o_ref[...] = _gelu(y)


def _s5_param_specs(layer):
    lead = (lambda *a: a[-1])

    def spec(shape):
        return pl.BlockSpec((None, None) + shape, lambda *a: (layer, lead(*a), 0, 0))

    return [spec((8, S5_ST)), spec((S5_CH, S5_ST)), spec((S5_CH, S5_ST)),
            spec((S5_ST, S5_CH)), spec((S5_ST, S5_CH)), spec((1, S5_CH))]


def _s5_prompt(u_perm, s5p, layer):
    zeros = jnp.zeros((BATCH, S5_NGB, 1, S5_ST), F32)
    state_spec = pl.BlockSpec((None, None, 1, S5_ST), lambda b, g: (b, g, 0, 0))
    out, hr, hi = pl.pallas_call(
        _s5_prompt_body,
        grid=(BATCH, S5_NGB),
        in_specs=[pl.BlockSpec((SEQ, S5_CH), lambda b, g: (b, g))] + _s5_param_specs(layer)
        + [state_spec, state_spec],
        out_specs=[pl.BlockSpec((SEQ, S5_CH), lambda b, g: (b, g)), state_spec, state_spec],
        out_shape=[jax.ShapeDtypeStruct((N_PROMPT, WIDTH), F32),
                   jax.ShapeDtypeStruct((BATCH, S5_NGB, 1, S5_ST), F32),
                   jax.ShapeDtypeStruct((BATCH, S5_NGB, 1, S5_ST), F32)],
        scratch_shapes=[pltpu.VMEM((SEQ, S5_ST), F32), pltpu.VMEM((SEQ, S5_ST), F32)],
        compiler_params=_params("parallel", "parallel"),
        name="s5_prompt",
    )(u_perm, *s5p, zeros, zeros)
    return out, hr, hi


def _s5_sample_body(u_ref, par_ref, bre_ref, bim_ref, cre_ref, cim_ref, d_ref, h0r_ref, h0i_ref,
                    o_ref, hr_ref, hi_ref):
    abr, abi, cr, ci = _s5_discretize(par_ref[...])
    bre, bim = bre_ref[...], bim_ref[...]
    bbr = (cr * bre - ci * bim).astype(BF16)
    bbi = (cr * bim + ci * bre).astype(BF16)
    u = u_ref[...]
    ub = u.astype(BF16)
    h0r, h0i = h0r_ref[...], h0i_ref[...]
    xr = _dot(ub, bbr) + abr * h0r - abi * h0i
    xi = _dot(ub, bbi) + abr * h0i + abi * h0r
    hr_ref[...] = xr
    hi_ref[...] = xi
    y = (_dot(xr.astype(BF16), cre_ref[...].astype(BF16))
         - _dot(xi.astype(BF16), cim_ref[...].astype(BF16)) + d_ref[...] * u)
    o_ref[...] = _gelu(y)


def _s5_sample(proj, s5p, h0r, h0i, layer):
    state_spec = pl.BlockSpec((DEC_BATCH, S5_ST), lambda g: (0, g))
    return pl.pallas_call(
        _s5_sample_body,
        grid=(S5_NGB,),
        in_specs=[pl.BlockSpec((DEC_BATCH, S5_CH), lambda g: (SAMPLE_ROW_BLOCK, C_U // S5_CH + g))]
        + _s5_param_specs(layer) + [state_spec, state_spec],
        out_specs=[pl.BlockSpec((DEC_BATCH, S5_CH), lambda g: (0, g)), state_spec, state_spec],
        out_shape=[jax.ShapeDtypeStruct((DEC_BATCH, WIDTH), F32),
                   jax.ShapeDtypeStruct((DEC_BATCH, S5_NGB * S5_ST), F32),
                   jax.ShapeDtypeStruct((DEC_BATCH, S5_NGB * S5_ST), F32)],
        compiler_params=_params("parallel"),
        name="s5_sample",
    )(proj, *s5p, h0r, h0i)


GDN_TS = 512
GDN_SCALE = HEAD_DIM ** -0.5


def _l2norm_heads(xc, o_ref):
    for h in range(HEADS):
        cs = slice(h * HEAD_DIM, (h + 1) * HEAD_DIM)
        seg = xc[:, cs]
        o_ref[:, cs] = seg * lax.rsqrt(jnp.sum(seg * seg, axis=-1, keepdims=True) + EPS)


def _conv_prompt_body(x_ref, prev_ref, w_ref, o_ref, ext_scr):
    i = pl.program_id(1)
    part = pl.program_id(2)
    x = x_ref[...]
    ext_scr[0:8, :] = jnp.where(i == 0, 0.0, prev_ref[...])
    ext_scr[8:8 + GDN_TS, :] = x
    w = w_ref[...]
    acc = x * w[3:4]
    for tap in range(CONV_TAPS - 1):
        back = CONV_TAPS - 1 - tap
        acc = acc + ext_scr[8 - back:8 - back + GDN_TS, :] * w[tap:tap + 1]
    xc = acc * _sigmoid(acc)

    @pl.when(part < 2)
    def _():
        _l2norm_heads(xc, o_ref)

    @pl.when(part == 2)
    def _():
        o_ref[...] = xc


def _conv_prompt(proj, conv_w, layer):
    n_t = SEQ // GDN_TS
    col0 = C_QKV // WIDTH

    def prev_index(b, i, p):
        return (jnp.maximum(b * (SEQ // 8) + i * (GDN_TS // 8) - 1, 0), col0 + p)

    return pl.pallas_call(
        _conv_prompt_body,
        grid=(BATCH, n_t, 3),
        in_specs=[
            pl.BlockSpec((GDN_TS, WIDTH), lambda b, i, p: (b * n_t + i, col0 + p)),
            pl.BlockSpec((8, WIDTH), prev_index),
            pl.BlockSpec((None, 8, WIDTH), lambda b, i, p: (layer, 0, p)),
        ],
        out_specs=pl.BlockSpec((GDN_TS, WIDTH), lambda b, i, p: (b * n_t + i, p)),
        out_shape=jax.ShapeDtypeStruct((N_PROMPT, CONV_WIDTH), F32),
        scratch_shapes=[pltpu.VMEM((GDN_TS + 8, WIDTH), F32)],
        compiler_params=_params("parallel", "parallel", "parallel"),
        name="gdn_conv_prompt",
    )(proj, proj, conv_w)


def _conv_sample_body(x_ref, hist_ref, w_ref, o_ref):
    part = pl.program_id(0)
    w = w_ref[...]
    acc = x_ref[...] * w[3:4]
    for tap in range(CONV_TAPS - 1):
        acc = acc + hist_ref[tap] * w[tap:tap + 1]
    xc = acc * _sigmoid(acc)

    @pl.when(part < 2)
    def _():
        _l2norm_heads(xc, o_ref)

    @pl.when(part == 2)
    def _():
        o_ref[...] = xc


def _conv_sample(proj, hist, conv_w, layer):
    col0 = C_QKV // WIDTH
    return pl.pallas_call(
        _conv_sample_body,
        grid=(3,),
        in_specs=[
            pl.BlockSpec((DEC_BATCH, WIDTH), lambda p: (SAMPLE_ROW_BLOCK, col0 + p)),
            pl.BlockSpec((None, CONV_TAPS - 1, DEC_BATCH, WIDTH), lambda p: (layer, 0, 0, p)),
            pl.BlockSpec((None, 8, WIDTH), lambda p: (layer, 0, p)),
        ],
        out_specs=pl.BlockSpec((DEC_BATCH, WIDTH), lambda p: (0, p)),
        out_shape=jax.ShapeDtypeStruct((DEC_BATCH, CONV_WIDTH), F32),
        compiler_params=_params("parallel"),
        name="gdn_conv_sample",
    )(proj, hist, conv_w)


GDN_CHUNK = 128
GDN_ROW_SLOTS = 16


def _gdn_gates_body(ba_ref, par_ref, bg_ref, gcrow_ref, *, rows, chunk):
    ba = ba_ref[...]
    par = par_ref[...]
    g = -jnp.exp(par[0:1]) * _softplus(ba + par[1:2])
    if chunk > 1:
        r = lax.broadcasted_iota(jnp.int32, (rows, rows), 0)
        c = lax.broadcasted_iota(jnp.int32, (rows, rows), 1)
        tri = jnp.where((r >= c) & ((r & -chunk) == (c & -chunk)), 1.0, 0.0)
        g = _dot(tri, g, precision=HIGHEST)
        g_t = g.T
        for cc in range(rows // chunk):
            gcrow_ref[cc * GDN_ROW_SLOTS:(cc + 1) * GDN_ROW_SLOTS, :] = (
                g_t[0:GDN_ROW_SLOTS, cc * chunk:(cc + 1) * chunk])
    lane = lax.broadcasted_iota(jnp.int32, ba.shape, 1)
    bg_ref[...] = jnp.where(lane < HEADS, _sigmoid(ba), g)


def _gdn_gates(proj, gate_par, layer, rows, n_blocks, first_block, chunk):
    out_specs = [pl.BlockSpec((rows, 128), lambda i: (i, 0))]
    out_shape = [jax.ShapeDtypeStruct((rows * n_blocks, 128), F32)]
    if chunk > 1:
        slots = rows // chunk * GDN_ROW_SLOTS
        out_specs.append(pl.BlockSpec((slots, chunk), lambda i: (i, 0)))
        out_shape.append(jax.ShapeDtypeStruct((slots * n_blocks, chunk), F32))
        body = functools.partial(_gdn_gates_body, rows=rows, chunk=chunk)
    else:
        body = functools.partial(_gdn_gates_body, gcrow_ref=None, rows=rows, chunk=chunk)
    return pl.pallas_call(
        body,
        grid=(n_blocks,),
        in_specs=[pl.BlockSpec((rows, 128), lambda i: (first_block + i, C_BA // 128)),
                  pl.BlockSpec((None, 8, 128), lambda i: (layer, 0, 0))],
        out_specs=out_specs,
        out_shape=out_shape,
        compiler_params=_params("parallel"),
        name="gdn_gates",
    )(proj, gate_par)


GDN_BASE = 8
GDN_GROUP = 2


def _gdn_chunk_body(q_ref, k_ref, v_ref, bg_ref, gcrow_ref, z_ref, s0_ref, gn_ref, o_ref, sfin_ref,
                    s_scr, mask_scr, u_scr, w_scr, qk_scr, qd_scr, kd_scr, *, n_tiles, tile_rows):
    ti = pl.program_id(1)
    n_chunks = tile_rows // GDN_CHUNK
    lanes = [slice(h * HEAD_DIM, (h + 1) * HEAD_DIM) for h in range(HEADS)]

    @pl.when(ti == 0)
    def _():
        s_scr[...] = s0_ref[...]

    row = lax.broadcasted_iota(jnp.int32, (GDN_CHUNK, GDN_CHUNK), 0)
    col = lax.broadcasted_iota(jnp.int32, (GDN_CHUNK, GDN_CHUNK), 1)

    def same_block(size):
        return (row & -size) == (col & -size)

    strict = row > col
    eye = jnp.where(row == col, 1.0, 0.0)
    mask_scr[0] = jnp.where(row >= col, 1.0, 0.0)
    mask_scr[1] = jnp.where(strict & same_block(GDN_BASE), 1.0, 0.0)
    merge_sizes = []
    size = GDN_BASE
    while size < GDN_CHUNK:
        mask_scr[2 + len(merge_sizes)] = jnp.where(strict & same_block(2 * size) & ~same_block(size), 1.0, 0.0)
        merge_sizes.append(size)
        size *= 2
    gain = gn_ref[...]

    tile = (GDN_CHUNK, HEAD_DIM)

    def prepare(pair, carry):
        units = [(dc, h) for dc in range(GDN_GROUP) for h in range(HEADS)]
        chains = range(len(units))
        rows, bg, gc_by_head = [], [], []
        for dc in range(GDN_GROUP):
            c = pair * GDN_GROUP + dc
            rows.append(pl.ds(pl.multiple_of(c * GDN_CHUNK, GDN_CHUNK), GDN_CHUNK))
            bg.append(bg_ref[rows[dc], :])
            gc_by_head.append(gcrow_ref[pl.ds(pl.multiple_of(c * GDN_ROW_SLOTS + HEADS, HEADS), HEADS), :])
        beta_of = lambda dc, h: jnp.broadcast_to(bg[dc][:, h:h + 1], tile)
        gc_of = lambda dc, h: jnp.broadcast_to(bg[dc][:, HEADS + h:HEADS + h + 1], tile)
        m = []
        for dc, h in units:
            cs = lanes[h]
            k = k_ref[rows[dc], cs]
            gc_lanes = gc_by_head[dc][h:h + 1, :]
            decay = mask_scr[0] * jnp.exp(jnp.minimum(gc_of(dc, h) - gc_lanes, 0.0))
            k16 = k.astype(BF16)
            m.append(_dot_nt((k * beta_of(dc, h)).astype(BF16), k16) * decay)
            qk_scr[rows[dc], cs] = (
                _dot_nt((q_ref[rows[dc], cs] * GDN_SCALE).astype(BF16), k16) * decay).astype(BF16)
        base = [m[i] * mask_scr[1] for i in chains]
        b16 = [base[i].astype(BF16) for i in chains]
        power = [_dot(b16[i], b16[i]).astype(BF16) for i in chains]
        inv = [eye - base[i] for i in chains]
        inv = [inv[i] + _dot(inv[i].astype(BF16), power[i]) for i in chains]
        for _ in range(int(math.log2(GDN_BASE)) - 2):
            power = [_dot(power[i], power[i]).astype(BF16) for i in chains]
            inv = [inv[i] + _dot(inv[i].astype(BF16), power[i]) for i in chains]
        for n in range(len(merge_sizes)):
            i16 = [inv[i].astype(BF16) for i in chains]
            half = [_dot(i16[i], (m[i] * mask_scr[2 + n]).astype(BF16)).astype(BF16) for i in chains]
            inv = [inv[i] - _dot(half[i], i16[i]) for i in chains]
        for i, (dc, h) in enumerate(units):
            cs = lanes[h]
            k = k_ref[rows[dc], cs]
            beta = beta_of(dc, h)
            gc = gc_of(dc, h)
            egc = jnp.exp(gc)
            rhs = jnp.concatenate([v_ref[rows[dc], cs] * beta, k * beta * egc], axis=-1)
            sol = rhs + _dot((inv[i] - eye).astype(BF16), rhs.astype(BF16))
            u_scr[rows[dc], cs] = sol[:, :HEAD_DIM]
            w_scr[rows[dc], cs] = sol[:, HEAD_DIM:].astype(BF16)
            gc_last = jnp.broadcast_to(gc[GDN_CHUNK - 1:GDN_CHUNK, :], tile)
            qd_scr[rows[dc], cs] = (q_ref[rows[dc], cs] * GDN_SCALE * egc).astype(BF16)
            kd_scr[rows[dc], cs] = (k * jnp.exp(gc_last - gc)).astype(BF16)
        return carry

    lax.fori_loop(0, n_chunks // GDN_GROUP, prepare, 0)

    def advance(c, carry):
        rows = pl.ds(pl.multiple_of(c * GDN_CHUNK, GDN_CHUNK), GDN_CHUNK)
        tail = pl.ds(pl.multiple_of(c * GDN_CHUNK + GDN_CHUNK - 8, 8), 8)
        heads = range(HEADS)
        gc_last = bg_ref[tail, :][7:8, :]
        state = [s_scr[h] for h in heads]
        s16 = [state[h].astype(BF16) for h in heads]
        vn16 = [(u_scr[rows, lanes[h]] - _dot(w_scr[rows, lanes[h]], s16[h])).astype(BF16) for h in heads]
        for h in heads:
            chunk_decay = jnp.exp(jnp.broadcast_to(gc_last[:, HEADS + h:HEADS + h + 1], (1, HEAD_DIM)))
            s_scr[h] = state[h] * chunk_decay + lax.dot_general(
                kd_scr[rows, lanes[h]], vn16[h], (((0,), (0,)), ((), ())), preferred_element_type=F32)
        for h in heads:
            cs = lanes[h]
            o = _dot(qd_scr[rows, cs], s16[h]) + _dot(qk_scr[rows, cs], vn16[h])
            zg = z_ref[rows, cs]
            o_ref[rows, cs] = (_rms(o, gain) * (zg * _sigmoid(zg))).astype(BF16)
        return carry

    lax.fori_loop(0, n_chunks, advance, 0)

    @pl.when(ti == n_tiles - 1)
    def _():
        sfin_ref[...] = s_scr[...]


def _gdn_chunks(qkv, bg, gcrow, z_src, z_col_block, s0, out_norm, layer, n_seq, seq_rows):
    tile_rows = min(seq_rows, GDN_TS)
    n_tiles = seq_rows // tile_rows
    n_masks = 2 + int(math.log2(GDN_CHUNK // GDN_BASE))

    def act(col_block):
        return pl.BlockSpec((tile_rows, WIDTH), lambda b, t: (b * n_tiles + t, col_block))

    slots = tile_rows // GDN_CHUNK * GDN_ROW_SLOTS
    state_spec = pl.BlockSpec((None, HEADS, HEAD_DIM, HEAD_DIM), lambda b, t: (b, 0, 0, 0))
    return pl.pallas_call(
        functools.partial(_gdn_chunk_body, n_tiles=n_tiles, tile_rows=tile_rows),
        grid=(n_seq, n_tiles),
        in_specs=[act(0), act(1), act(2),
                  pl.BlockSpec((tile_rows, 128), lambda b, t: (b * n_tiles + t, 0)),
                  pl.BlockSpec((slots, GDN_CHUNK), lambda b, t: (b * n_tiles + t, 0)),
                  act(z_col_block), state_spec,
                  pl.BlockSpec((None, 1, HEAD_DIM), lambda b, t: (layer, 0, 0))],
        out_specs=[act(0), state_spec],
        out_shape=[jax.ShapeDtypeStruct((n_seq * seq_rows, WIDTH), BF16),
                   jax.ShapeDtypeStruct((n_seq, HEADS, HEAD_DIM, HEAD_DIM), F32)],
        scratch_shapes=[pltpu.VMEM((HEADS, HEAD_DIM, HEAD_DIM), F32),
                        pltpu.VMEM((n_masks, GDN_CHUNK, GDN_CHUNK), F32),
                        pltpu.VMEM((tile_rows, WIDTH), F32),
                        pltpu.VMEM((tile_rows, WIDTH), BF16),
                        pltpu.VMEM((tile_rows, WIDTH), BF16),
                        pltpu.VMEM((tile_rows, WIDTH), BF16),
                        pltpu.VMEM((tile_rows, WIDTH), BF16)],
        compiler_params=_params("parallel", "arbitrary"),
        name="gdn_chunks",
    )(qkv, qkv, qkv, bg, gcrow, z_src, s0, out_norm)


GDN_STEP_COLS = CONV_WIDTH + 128 + WIDTH


def _gdn_step_body(x_ref, s0_ref, gn_ref, o_ref, s_ref):
    heads = range(HEADS)
    col = lambda part, h: slice(part * WIDTH + h * HEAD_DIM, part * WIDTH + (h + 1) * HEAD_DIM)
    gain = gn_ref[...]
    bg = x_ref[:, CONV_WIDTH:CONV_WIDTH + 128]
    z0 = CONV_WIDTH + 128
    k = [x_ref[:, col(1, h)] for h in heads]
    q = [x_ref[:, col(0, h)] * GDN_SCALE for h in heads]
    k16 = [k[h].astype(BF16) for h in heads]
    state = [s0_ref[h] for h in heads]
    s16 = [state[h].astype(BF16) for h in heads]
    k_s = [_dot(k16[h], s16[h]) for h in heads]
    q_s = [_dot(q[h].astype(BF16), s16[h]) for h in heads]
    decay = [jnp.exp(jnp.broadcast_to(bg[:, HEADS + h:HEADS + h + 1], (8, HEAD_DIM))) for h in heads]
    beta = [jnp.broadcast_to(bg[:, h:h + 1], (8, HEAD_DIM)) for h in heads]
    v_new = [beta[h] * (x_ref[:, col(2, h)] - decay[h] * k_s[h]) for h in heads]
    for h in heads:
        s_ref[h] = state[h] * decay[h][0:1, :] + lax.dot_general(
            k16[h], v_new[h].astype(BF16), (((0,), (0,)), ((), ())), preferred_element_type=F32)
    for h in heads:
        qk = jnp.sum(q[h].astype(BF16).astype(F32) * k16[h].astype(F32), axis=-1, keepdims=True)
        o = decay[h] * q_s[h] + qk * v_new[h].astype(BF16).astype(F32)
        zg = x_ref[:, z0 + h * HEAD_DIM:z0 + (h + 1) * HEAD_DIM]
        o_ref[:, h * HEAD_DIM:(h + 1) * HEAD_DIM] =(_rms(o, gain) * (zg * _sigmoid(zg))).astype(BF16)


def _gdn_step(x8, state, out_norm, layer):
    return pl.pallas_call(
        _gdn_step_body,
        grid=(DEC_BATCH,),
        in_specs=[pl.BlockSpec((None, 8, GDN_STEP_COLS), lambda b: (b, 0, 0)),
                  pl.BlockSpec((None, None, HEADS, HEAD_DIM, HEAD_DIM), lambda b: (layer, b, 0, 0, 0)),
                  pl.BlockSpec((None, 1, HEAD_DIM), lambda b: (layer, 0, 0))],
        out_specs=[pl.BlockSpec((None, 8, WIDTH), lambda b: (b, 0, 0)),
                   pl.BlockSpec((None, HEADS, HEAD_DIM, HEAD_DIM), lambda b: (b, 0, 0, 0))],
        out_shape=[jax.ShapeDtypeStruct((DEC_BATCH, 8, WIDTH), BF16),
                   jax.ShapeDtypeStruct((DEC_BATCH, HEADS, HEAD_DIM, HEAD_DIM), F32)],
        compiler_params=_params("parallel"),
        name="gdn_step",
    )(x8, state, out_norm)


def _pack_body(a_ref, b_ref, o_ref):
    n = pl.program_id(1)
    first_gate = C_GATE // PACK_TN
    last = PACK_TILES - 1
    shift = 2 * HEADS
    a = a_ref[...]
    shifted = jnp.concatenate([a[shift:, :], b_ref[...]], axis=0)
    row = lax.broadcasted_iota(jnp.int32, a.shape, 0)
    logits = jnp.where(row < shift, a, 0.0)
    tile = jnp.where(n < first_gate, a, jnp.where(n == last, logits, shifted))
    o_ref[...] = tile.T.astype(BF16)


def _pack_w_in(w_in):
    first_gate = C_GATE // PACK_TN
    last = PACK_TILES - 1
    shift = 2 * HEADS
    w_t = jnp.swapaxes(w_in, 1, 2)

    def a_index(d, n):
        return (d, jnp.where(n == last, first_gate, n), 0)

    def b_index(d, n):
        return (d, jnp.where((n >= first_gate) & (n < last), (n + 1) * (PACK_TN // shift), 0), 0)

    return pl.pallas_call(
        _pack_body,
        grid=(DEPTH, PACK_TILES),
        in_specs=[pl.BlockSpec((None, PACK_TN, D_MODEL), a_index),
                  pl.BlockSpec((None, shift, D_MODEL), b_index)],
        out_specs=pl.BlockSpec((None, D_MODEL, PACK_TN), lambda d, n: (d, 0, n)),
        out_shape=jax.ShapeDtypeStruct((DEPTH, D_MODEL, PROJ_COLS), BF16),
        compiler_params=_params("parallel", "arbitrary"),
        name="pack_w_in",
    )(w_t, w_t)


def _s5_params(a_re, a_im, log_dt, b_re, b_im, c_re, c_im, d):
    eye = jnp.eye(S5_GB, dtype=F32)

    def state_rows(x):
        return x.reshape(DEPTH, S5_NGB, 1, S5_ST)

    par = jnp.concatenate(
        [state_rows(a_re), state_rows(a_im),
         state_rows(jnp.repeat(log_dt, SSM_STATE, axis=-1)),
         jnp.zeros((DEPTH, S5_NGB, 5, S5_ST), F32)], axis=2)

    def b_blockdiag(b):
        b = b.reshape(DEPTH, S5_NGB, S5_GB, SSM_STATE, SSM_GROUP)
        return jnp.einsum('dbgpc,gh->dbgchp', b, eye).reshape(DEPTH, S5_NGB, S5_CH, S5_ST)

    def c_blockdiag(c):
        c = c.reshape(DEPTH, S5_NGB, S5_GB, SSM_GROUP, SSM_STATE)
        return jnp.einsum('dbgcp,gh->dbgphc', c, eye).reshape(DEPTH, S5_NGB, S5_ST, S5_CH)

    return (par, b_blockdiag(b_re), b_blockdiag(b_im), c_blockdiag(c_re), c_blockdiag(c_im),
            d.reshape(DEPTH, S5_NGB, 1, S5_CH))


def _segment_major(x):
    w = x.shape[-1]
    return x.reshape(BATCH, S5_LANES, S5_SEG, w).transpose(0, 2, 1, 3).reshape(N_PROMPT, w)


def _token_major(x):
    w = x.shape[-1]
    return x.reshape(BATCH, S5_SEG, S5_LANES, w).transpose(0, 2, 1, 3).reshape(N_PROMPT, w)


def _pad_rows(x):
    return jnp.concatenate([x, jnp.zeros((ROWS - x.shape[0], x.shape[1]), x.dtype)], axis=0)


def kernel(x_prompt, x_sample, cache_k, cache_v, page_table, state_ssm, state_conv, state_gdn, ffn1_norm, ffn1_w_gate, ffn1_w_up, ffn1_w_down, mix_norm, w_in, sb_bias, ssm_a_re, ssm_a_im, ssm_log_dt, ssm_b_re, ssm_b_im, ssm_c_re, ssm_c_im, ssm_d, ssm_w_glu, gdn_conv_w, gdn_a_log, gdn_dt_bias, gdn_out_norm, w_branch, w_out, ffn2_norm, ffn2_w_gate, ffn2_w_up, ffn2_w_down, final_norm):
    bf = lambda w: w.astype(BF16)
    gain3 = lambda g: g.reshape(DEPTH, 1, D_MODEL)
    w_in_p = _pack_w_in(w_in)
    ffn1 = (gain3(ffn1_norm), bf(ffn1_w_gate), bf(ffn1_w_up), bf(ffn1_w_down))
    ffn2 = (gain3(ffn2_norm), bf(ffn2_w_gate), bf(ffn2_w_up), bf(ffn2_w_down))
    mix_gain = gain3(mix_norm)
    w_glu, w_br, w_o = bf(ssm_w_glu), bf(w_branch).reshape(DEPTH, 3, WIDTH, D_MODEL), bf(w_out)
    s5p = _s5_params(ssm_a_re, ssm_a_im, ssm_log_dt, ssm_b_re, ssm_b_im, ssm_c_re, ssm_c_im, ssm_d)
    bias_flat = sb_bias.reshape(DEPTH * HEADS)
    bias_lanes = jnp.tile(sb_bias, (1, PAGE)).reshape(DEPTH, 1, DEC_LANES)
    page_flat = page_table.reshape(DEC_BATCH * N_PAGES)
    conv_w8 = jnp.pad(gdn_conv_w, ((0, 0), (0, 8 - CONV_TAPS), (0, 0)))
    conv_hist = state_conv.transpose(0, 2, 1, 3)
    lane_pad = jnp.zeros((DEPTH, 128 - 2 * HEADS), F32)
    gate_par = jnp.stack([jnp.concatenate([jnp.zeros((DEPTH, HEADS), F32), gdn_a_log, lane_pad], axis=1),
                          jnp.concatenate([jnp.zeros((DEPTH, HEADS), F32), gdn_dt_bias, lane_pad], axis=1)]
                         + [jnp.zeros((DEPTH, 128), F32)] * 6, axis=1)
    out_norm = gdn_out_norm.reshape(DEPTH, 1, HEAD_DIM)
    ssm_h0r = state_ssm[..., 0].reshape(DEPTH, DEC_BATCH, WIDTH * SSM_STATE // SSM_GROUP)
    ssm_h0i = state_ssm[..., 1].reshape(DEPTH, DEC_BATCH, WIDTH * SSM_STATE // SSM_GROUP)

    x = _pad_rows(jnp.concatenate([x_prompt.reshape(N_PROMPT, D_MODEL),
                                   x_sample.reshape(DEC_BATCH, D_MODEL)], axis=0))
    sample = slice(N_PROMPT, N_PROMPT + DEC_BATCH)
    ks, vs, sp, ss, cp, cs_, gp, gs = ([] for _ in range(8))
    k_all = jnp.zeros((DEPTH, N_PROMPT, WIDTH), F32)
    v_all = jnp.zeros((DEPTH, N_PROMPT, WIDTH), F32)
    for l in range(DEPTH):
        x = _ffn(x, *ffn1, l)
        proj = _in_proj(x, mix_gain, w_in_p, l)

        oa_p, k_all, v_all = _sb_prompt(proj, bias_flat, l, k_all, v_all)
        q_s = proj[sample, C_Q:C_Q + WIDTH].reshape(DEC_BATCH, HEADS, HEAD_DIM)
        oa_s = _sb_decode(q_s, cache_k, cache_v, page_flat, bias_lanes, l)
        oa = _pad_rows(jnp.concatenate([oa_p, oa_s.reshape(DEC_BATCH, WIDTH).astype(BF16)], axis=0))

        gb_p, hr_p, hi_p = _s5_prompt(_segment_major(proj[:N_PROMPT, C_U:C_U + WIDTH]), s5p, l)
        gb_s, hr_s, hi_s = _s5_sample(proj, s5p, ssm_h0r[l], ssm_h0i[l], l)
        ob = _glu(_pad_rows(jnp.concatenate([_token_major(gb_p), gb_s], axis=0)), w_glu, l)

        qkv_p = _conv_prompt(proj, conv_w8, l)
        bg_p, gcrow_p = _gdn_gates(proj, gate_par, l, GDN_TS, N_PROMPT // GDN_TS, 0, GDN_CHUNK)
        oc_p, gdn_p = _gdn_chunks(qkv_p, bg_p, gcrow_p, proj, C_Z // WIDTH,
                                  jnp.zeros((BATCH, HEADS, HEAD_DIM, HEAD_DIM), F32), out_norm, l, BATCH, SEQ)
        qkv_s = _conv_sample(proj, conv_hist, conv_w8, l)
        (bg_s,) = _gdn_gates(proj, gate_par, l, DEC_BATCH, 1, SAMPLE_ROW_BLOCK, 1)
        step_in = jnp.concatenate([qkv_s, bg_s, proj[sample, C_Z:C_Z + WIDTH]], axis=1)
        oc_s, gdn_s = _gdn_step(jnp.pad(step_in[:, None, :], ((0, 0), (0, 7), (0, 0))), state_gdn, out_norm, l)
        oc = _pad_rows(jnp.concatenate([oc_p, oc_s[:, 0, :]], axis=0))

        x = _merge(x, oa, ob, oc, proj, w_br, w_o, l)
        x = _ffn(x, *ffn2, l)

        ks.append(proj[sample, C_K:C_K + WIDTH].reshape(DEC_BATCH, 1, HEADS, HEAD_DIM))
        vs.append(proj[sample, C_V:C_V + WIDTH].reshape(DEC_BATCH, 1, HEADS, HEAD_DIM))
        state = lambda r, i, n: jnp.stack([r.reshape(n, WIDTH // SSM_GROUP, SSM_STATE),
                                           i.reshape(n, WIDTH // SSM_GROUP, SSM_STATE)], axis=-1)
        sp.append(state(hr_p, hi_p, BATCH))
        ss.append(state(hr_s, hi_s, DEC_BATCH))
        qkv_cols = slice(C_QKV, C_QKV + CONV_WIDTH)
        cp.append(jnp.stack([proj[(b + 1) * SEQ - (CONV_TAPS - 1):(b + 1) * SEQ, qkv_cols] for b in range(BATCH)]))
        cs_.append(jnp.concatenate([state_conv[l][:, 1:], proj[sample, qkv_cols][:, None, :]], axis=1))
        gp.append(gdn_p)
        gs.append(gdn_s)

    gain = final_norm.reshape(1, D_MODEL)
    y_p = _final_norm(x, gain, GDN_TS, N_PROMPT // GDN_TS, 0)
    y_s = _final_norm(x, gain, DEC_BATCH, 1, SAMPLE_ROW_BLOCK)
    kv_shape = (DEPTH, BATCH, SEQ, HEADS, HEAD_DIM)
    return (y_p.reshape(BATCH, SEQ, D_MODEL), y_s.reshape(DEC_BATCH, 1, D_MODEL),
            k_all.reshape(kv_shape), v_all.reshape(kv_shape), jnp.stack(ks), jnp.stack(vs),
            jnp.stack(sp), jnp.stack(ss), jnp.stack(cp), jnp.stack(cs_),
            jnp.stack(gp), jnp.stack(gs))
```

```python
import functools
import math

import jax
import jax.numpy as jnp
from jax import lax
from jax.experimental import pallas as pl
from jax.experimental.pallas import tpu as pltpu

F32 = jnp.float32
BF16 = jnp.bfloat16
HIGHEST = lax.Precision.HIGHEST

D_MODEL = 2048
BATCH = 4
SEQ = 2048
DEPTH = 2
DEC_BATCH = 32
PAGE = 128
N_PAGES = 64
HEADS = 8
HEAD_DIM = 128
WIDTH = HEADS * HEAD_DIM
SSM_GROUP = 16
SSM_STATE = 64
CONV_TAPS = 4
CONV_WIDTH = 3 * WIDTH
FFN_DIM = 5632
EPS = 1e-6

N_PROMPT = BATCH * SEQ
ROWS = 8320
TM = ROWS // 10
TM_GLU = ROWS // 20
SAMPLE_ROW_BLOCK = N_PROMPT // DEC_BATCH

C_Q, C_K, C_V, C_U, C_QKV, C_Z, C_GATE, C_BA = 0, 1024, 2048, 3072, 4096, 7168, 8192, 14336
PROJ_COLS = 14592
TN_PROJ = 768
PACK_TN = 512
PACK_TILES = -(-PROJ_COLS // PACK_TN)
TF = 512
TM_BIG = ROWS // 8
TM_MERGE = ROWS // 52

VMEM_LIMIT = 56 * 1024 * 1024


def _params(*sem):
    return pltpu.CompilerParams(dimension_semantics=sem, vmem_limit_bytes=VMEM_LIMIT)


def _rms(x, gain):
    ms = jnp.mean(x * x, axis=-1, keepdims=True)
    return x * lax.rsqrt(ms + EPS) * gain


def _softplus(x):
    return jnp.maximum(x, 0.0) + jnp.log(1.0 + jnp.exp(-jnp.abs(x)))


def _sigmoid(x):
    return 1.0 / (1.0 + jnp.exp(-x))


def _dot(a, b, **kw):
    return jnp.dot(a, b, preferred_element_type=F32, **kw)


def _dot_nt(a, b, **kw):
    return lax.dot_general(a, b, (((1,), (1,)), ((), ())), preferred_element_type=F32, **kw)


def _ffn_body(x_ref, g_ref, wg_ref, wu_ref, wd_ref, o_ref, h_scr):
    @pl.when(pl.program_id(1) == 0)
    def _():
        x = x_ref[...]
        h_scr[...] = _rms(x, g_ref[...]).astype(BF16)
        o_ref[...] = x

    h = h_scr[...]
    a = _dot(h, wg_ref[...])
    b = _dot(h, wu_ref[...])
    act = (a * _sigmoid(a)) * b * 0.5
    o_ref[...] += _dot(act.astype(BF16), wd_ref[...])


def _ffn(x, gain, wg, wu, wd, layer):
    return pl.pallas_call(
        _ffn_body,
        grid=(ROWS // TM, FFN_DIM // TF),
        in_specs=[
            pl.BlockSpec((TM, D_MODEL), lambda i, f: (i, 0)),
            pl.BlockSpec((None, 1, D_MODEL), lambda i, f: (layer, 0, 0)),
            pl.BlockSpec((None, D_MODEL, TF), lambda i, f: (layer, 0, f)),
            pl.BlockSpec((None, D_MODEL, TF), lambda i, f: (layer, 0, f)),
            pl.BlockSpec((None, TF, D_MODEL), lambda i, f: (layer, f, 0)),
        ],
        out_specs=pl.BlockSpec((TM, D_MODEL), lambda i, f: (i, 0)),
        out_shape=jax.ShapeDtypeStruct((ROWS, D_MODEL), F32),
        scratch_shapes=[pltpu.VMEM((TM, D_MODEL), BF16)],
        compiler_params=_params("parallel", "arbitrary"),
        name="ffn",
    )(x, gain, wg, wu, wd)


def _proj_body(x_ref, g_ref, w_ref, o_ref, h_scr):
    @pl.when(pl.program_id(1) == 0)
    def _():
        h_scr[...] = _rms(x_ref[...], g_ref[...]).astype(BF16)

    o_ref[...] = _dot(h_scr[...], w_ref[...])


def _in_proj(x, gain, w, layer):
    return pl.pallas_call(
        _proj_body,
        grid=(ROWS // TM_BIG, PROJ_COLS // TN_PROJ),
        in_specs=[
            pl.BlockSpec((TM_BIG, D_MODEL), lambda i, n: (i, 0), pipeline_mode=pl.Buffered(1)),
            pl.BlockSpec((None, 1, D_MODEL), lambda i, n: (layer, 0, 0)),
            pl.BlockSpec((None, D_MODEL, TN_PROJ), lambda i, n: (layer, 0, n)),
        ],
        out_specs=pl.BlockSpec((TM_BIG, TN_PROJ), lambda i, n: (i, n)),
        out_shape=jax.ShapeDtypeStruct((ROWS, PROJ_COLS), F32),
        scratch_shapes=[pltpu.VMEM((TM_BIG, D_MODEL), BF16)],
        compiler_params=_params("parallel", "arbitrary"),
        name="in_proj",
    )(x, gain, w)


def _glu_body(g_ref, w_ref, o_ref):
    g = g_ref[...]
    o_ref[...] = (g * _sigmoid(_dot(g.astype(BF16), w_ref[...]))).astype(BF16)


def _glu(g, w, layer):
    return pl.pallas_call(
        _glu_body,
        grid=(ROWS // TM_GLU,),
        in_specs=[
            pl.BlockSpec((TM_GLU, WIDTH), lambda i: (i, 0)),
            pl.BlockSpec((None, WIDTH, WIDTH), lambda i: (layer, 0, 0)),
        ],
        out_specs=pl.BlockSpec((TM_GLU, WIDTH), lambda i: (i, 0)),
        out_shape=jax.ShapeDtypeStruct((ROWS, WIDTH), BF16),
        compiler_params=_params("parallel"),
        name="glu",
    )(g, w)


def _merge_body(x_ref, oa_ref, ob_ref, oc_ref, ga_ref, gb_ref, gc_ref, wa_ref, wb_ref, wc_ref, wo_ref, o_ref):
    m = (_sigmoid(ga_ref[...]) * _dot(oa_ref[...], wa_ref[...])
         + _sigmoid(gb_ref[...]) * _dot(ob_ref[...], wb_ref[...])
         + _sigmoid(gc_ref[...]) * _dot(oc_ref[...], wc_ref[...]))
    o_ref[...] = x_ref[...] + _dot(m.astype(BF16), wo_ref[...])


def _merge(x, oa, ob, oc, proj, w_branch, w_out, layer):
    gate_blk = C_GATE // D_MODEL
    o_spec = pl.BlockSpec((TM_MERGE, WIDTH), lambda i: (i, 0))

    def gate_spec(which):
        return pl.BlockSpec((TM_MERGE, D_MODEL), lambda i: (i, gate_blk + which))

    def w_spec(which):
        return pl.BlockSpec((None, None, WIDTH, D_MODEL), lambda i: (layer, which, 0, 0),
                            pipeline_mode=pl.Buffered(1))

    return pl.pallas_call(
        _merge_body,
        grid=(ROWS // TM_MERGE,),
        in_specs=[
            pl.BlockSpec((TM_MERGE, D_MODEL), lambda i: (i, 0)),
            o_spec, o_spec, o_spec,
            gate_spec(0), gate_spec(1), gate_spec(2),
            w_spec(0), w_spec(1), w_spec(2),
            pl.BlockSpec((None, D_MODEL, D_MODEL), lambda i: (layer, 0, 0), pipeline_mode=pl.Buffered(1)),
        ],
        out_specs=pl.BlockSpec((TM_MERGE, D_MODEL), lambda i: (i, 0)),
        out_shape=jax.ShapeDtypeStruct((ROWS, D_MODEL), F32),
        compiler_params=_params("parallel"),
        name="merge",
    )(x, oa, ob, oc, proj, proj, proj, w_branch, w_branch, w_branch, w_out)


def _final_norm_body(x_ref, g_ref, o_ref):
    o_ref[...] = _rms(x_ref[...], g_ref[...])


def _final_norm(x, gain, rows, n_blocks, first_block):
    return pl.pallas_call(
        _final_norm_body,
        grid=(n_blocks,),
        in_specs=[pl.BlockSpec((rows, D_MODEL), lambda i: (first_block + i, 0)),
                  pl.BlockSpec((1, D_MODEL), lambda i: (0, 0))],
        out_specs=pl.BlockSpec((rows, D_MODEL), lambda i: (i, 0)),
        out_shape=jax.ShapeDtypeStruct((rows * n_blocks, D_MODEL), F32),
        compiler_params=_params("parallel"),
        name="final_norm",
    )(x, gain)


SB_BQ = 256
SB_BK = 256
SB_HB = 4
SB_SCALE = HEAD_DIM ** -0.5


def _stick_terms(z):
    t = jnp.log(1.0 + jnp.exp(-jnp.abs(z)))
    return jnp.minimum(z, 0.0) - t, jnp.minimum(-z, 0.0) - t


LOG2E = 1.0 / math.log(2.0)


def _sb_prompt_body(bias_ref, q_ref, k_ref, v_ref, *rest, layer):
    o_ref, k_all_ref, v_all_ref, q_scr, acc_scr, run_scr = rest[2:]
    hg = pl.program_id(1)
    qi = pl.program_id(2)

    @pl.when(qi == 0)
    def _():
        k_all_ref[...] = k_ref[...]
        v_all_ref[...] = v_ref[...]

    heads = range(SB_HB)
    lanes = [slice(hh * HEAD_DIM, (hh + 1) * HEAD_DIM) for hh in heads]
    row = lax.broadcasted_iota(jnp.int32, (SB_BQ, SB_BK), 0)
    col = lax.broadcasted_iota(jnp.int32, (SB_BQ, SB_BK), 1)
    q_pos = qi * SB_BQ + row
    r2 = lax.broadcasted_iota(jnp.int32, (SB_BK, SB_BK), 0)
    c2 = lax.broadcasted_iota(jnp.int32, (SB_BK, SB_BK), 1)
    later_key = jnp.where(r2 > c2, 1.0, 0.0).astype(BF16)
    bias2 = [bias_ref[layer * HEADS + hg * SB_HB + hh] * LOG2E for hh in heads]

    for hh in heads:
        q_scr[hh] = (q_ref[:, lanes[hh]] * (SB_SCALE * LOG2E)).astype(BF16)
    acc_scr[...] = jnp.zeros_like(acc_scr)
    run_scr[...] = jnp.zeros_like(run_scr)

    def visit(blocks, masked):
        units = [(n, hh) for n in range(len(blocks)) for hh in heads]
        chains = range(len(units))
        k_rows = [pl.ds(pl.multiple_of(j * SB_BK, SB_BK), SB_BK) for j in blocks]
        mask = (blocks[0] * SB_BK + col) < q_pos
        z = [_dot_nt(q_scr[hh], k_ref[k_rows[n], lanes[hh]].astype(BF16)) + bias2[hh] for n, hh in units]
        t = [jnp.log(1.0 + jnp.exp2(-jnp.abs(z[i]))) * LOG2E for i in chains]
        log_hit = [jnp.minimum(z[i], 0.0) - t[i] for i in chains]
        log_fail = [jnp.minimum(-z[i], 0.0) - t[i] for i in chains]
        if masked:
            log_fail = [jnp.where(mask, log_fail[i], 0.0) for i in chains]
        suffix = [_dot(log_fail[i].astype(BF16), later_key) for i in chains]
        w = []
        run = [run_scr[hh] for hh in heads]
        for i, (n, hh) in enumerate(units):
            w_i = jnp.exp2(log_hit[i] + suffix[i] + run[hh])
            block_total = suffix[i][:, 0:1] + log_fail[i][:, 0:1]
            run[hh] = run[hh] + jnp.broadcast_to(block_total, (SB_BQ, SB_BK))
            w.append(jnp.where(mask, w_i, 0.0) if masked else w_i)
        for hh in heads:
            run_scr[hh] = run[hh]
        for hh in heads:
            acc = acc_scr[hh]
            for i, (n, h2) in enumerate(units):
                if h2 == hh:
                    acc = acc + _dot(w[i].astype(BF16), v_ref[k_rows[n], lanes[hh]].astype(BF16))
            acc_scr[hh] = acc

    visit([qi], True)

    def below_diagonal(it, carry):
        visit([qi - 1 - 2 * it, qi - 2 - 2 * it], False)
        return carry

    lax.fori_loop(0, qi // 2, below_diagonal, 0)

    @pl.when(qi % 2 == 1)
    def _():
        visit([0], False)
    for hh in heads:
        o_ref[:, lanes[hh]] = acc_scr[hh].astype(BF16)


def _sb_prompt(proj, bias_flat, layer, k_all, v_all):
    hw = SB_HB * HEAD_DIM
    n_hg = HEADS // SB_HB
    nq = SEQ // SB_BQ
    all_spec = pl.BlockSpec((None, SEQ, hw), lambda b, h, i, bias: (layer, b, h))
    all_shape = jax.ShapeDtypeStruct((DEPTH, N_PROMPT, WIDTH), F32)
    grid_spec = pltpu.PrefetchScalarGridSpec(
        num_scalar_prefetch=1,
        grid=(BATCH, n_hg, nq),
        in_specs=[
            pl.BlockSpec((SB_BQ, hw), lambda b, h, i, bias: (b * nq + i, C_Q // hw + h)),
            pl.BlockSpec((SEQ, hw), lambda b, h, i, bias: (b, C_K // hw + h)),
            pl.BlockSpec((SEQ, hw), lambda b, h, i, bias: (b, C_V // hw + h)),
            pl.BlockSpec(memory_space=pl.ANY),
            pl.BlockSpec(memory_space=pl.ANY),
        ],
        out_specs=[pl.BlockSpec((SB_BQ, hw), lambda b, h, i, bias: (b * nq + i, h)), all_spec, all_spec],
        scratch_shapes=[pltpu.VMEM((SB_HB, SB_BQ, HEAD_DIM), BF16),
                        pltpu.VMEM((SB_HB, SB_BQ, HEAD_DIM), F32),
                        pltpu.VMEM((SB_HB, SB_BQ, SB_BK), F32)],
    )
    return pl.pallas_call(
        functools.partial(_sb_prompt_body, layer=layer),
        grid_spec=grid_spec,
        out_shape=[jax.ShapeDtypeStruct((N_PROMPT, WIDTH), BF16), all_shape, all_shape],
        input_output_aliases={4: 1, 5: 2},
        compiler_params=_params("parallel", "parallel", "arbitrary"),
        name="sb_prompt",
    )(bias_flat, proj, proj, proj, k_all, v_all)


DEC_PAGES = 16
DEC_STEPS = N_PAGES // DEC_PAGES
DEC_LANES = PAGE * HEADS


def _lane_shift_up(x, s):
    n = x.shape[1]
    lane = lax.broadcasted_iota(jnp.int32, x.shape, 1)
    return jnp.where(lane < n - s, pltpu.roll(x, n - s, 1), 0.0)


def _sb_decode_body(pt_ref, q_ref, bias_ref, *refs):
    k_refs = refs[:DEC_PAGES]
    v_refs = refs[DEC_PAGES:2 * DEC_PAGES]
    o_ref, acc_scr, run_scr = refs[2 * DEC_PAGES:]
    g = pl.program_id(1)

    @pl.when(g == 0)
    def _():
        acc_scr[...] = jnp.zeros_like(acc_scr)
        run_scr[...] = jnp.zeros_like(run_scr)

    sub = lax.broadcasted_iota(jnp.int32, (HEADS, DEC_LANES), 0)
    lane = lax.broadcasted_iota(jnp.int32, (HEADS, DEC_LANES), 1)
    own_head = (lane & (HEADS - 1)) == sub
    q = q_ref[...].astype(BF16)
    rows = []
    for i in range(DEC_PAGES):
        k2 = k_refs[i][...].reshape(DEC_LANES, HEAD_DIM).astype(BF16)
        zt = _dot_nt(q, k2)
        rows.append(jnp.sum(jnp.where(own_head, zt, 0.0), axis=0, keepdims=True))
    z = jnp.concatenate(rows, axis=0) * SB_SCALE + bias_ref[...]
    log_hit, log_fail = _stick_terms(z)

    suffix = log_fail
    for s in (HEADS, 4 * HEADS):
        suffix = (suffix + _lane_shift_up(suffix, s)) + (_lane_shift_up(suffix, 2 * s) + _lane_shift_up(suffix, 3 * s))
    s = 16 * HEADS
    while s < DEC_LANES:
        suffix = suffix + jnp.concatenate([suffix[:, s:], jnp.zeros((DEC_PAGES, s), F32)], axis=1)
        s *= 2
    tile_lane = lax.broadcasted_iota(jnp.int32, (DEC_PAGES, 16 * HEADS), 1)
    tot = jnp.where(tile_lane < HEADS, suffix[:, :16 * HEADS], 0.0)
    for s in (HEADS, 4 * HEADS):
        tot = (tot + pltpu.roll(tot, s, 1)) + (pltpu.roll(tot, 2 * s, 1) + pltpu.roll(tot, 3 * s, 1))
    tot = jnp.concatenate([tot] * (DEC_LANES // (16 * HEADS)), axis=1)
    page_r = lax.broadcasted_iota(jnp.int32, (DEC_PAGES, DEC_PAGES), 0)
    page_c = lax.broadcasted_iota(jnp.int32, (DEC_PAGES, DEC_PAGES), 1)
    later = jnp.where(page_c > page_r, 1.0, 0.0)
    after = _dot(later, tot, precision=HIGHEST)
    run = run_scr[...]
    log_between = (suffix - log_fail) + after + run
    run_scr[...] = run + jnp.sum(tot, axis=0, keepdims=True)
    w = jnp.exp(log_hit + log_between)

    acc = acc_scr[...]
    for i in range(DEC_PAGES):
        wm = jnp.where(own_head, jnp.broadcast_to(w[i:i + 1, :], (HEADS, DEC_LANES)), 0.0).astype(BF16)
        v2 = v_refs[i][...].reshape(DEC_LANES, HEAD_DIM).astype(BF16)
        acc = acc + _dot(wm, v2)
    acc_scr[...] = acc

    @pl.when(g == DEC_STEPS - 1)
    def _():
        o_ref[...] = acc


def _sb_decode(q, cache_k, cache_v, page_table_flat, bias_lanes, layer):
    def page_spec(i):
        def index(b, g, pt):
            return (layer, pt[b * N_PAGES + (DEC_STEPS - 1 - g) * DEC_PAGES + i], 0, 0, 0)
        return pl.BlockSpec((None, None, PAGE, HEADS, HEAD_DIM), index)

    grid_spec = pltpu.PrefetchScalarGridSpec(
        num_scalar_prefetch=1,
        grid=(DEC_BATCH, DEC_STEPS),
        in_specs=[pl.BlockSpec((None, HEADS, HEAD_DIM), lambda b, g, pt: (b, 0, 0)),
                  pl.BlockSpec((None, 1, DEC_LANES), lambda b, g, pt: (layer, 0, 0))]
        + [page_spec(i) for i in range(DEC_PAGES)] * 2,
        out_specs=pl.BlockSpec((None, HEADS, HEAD_DIM), lambda b, g, pt: (b, 0, 0)),
        scratch_shapes=[pltpu.VMEM((HEADS, HEAD_DIM), F32), pltpu.VMEM((1, DEC_LANES), F32)],
    )
    return pl.pallas_call(
        _sb_decode_body,
        grid_spec=grid_spec,
        out_shape=jax.ShapeDtypeStruct((DEC_BATCH, HEADS, HEAD_DIM), F32),
        compiler_params=_params("parallel", "arbitrary"),
        name="sb_decode",
    )(page_table_flat, q, bias_lanes, *([cache_k] * DEC_PAGES), *([cache_v] * DEC_PAGES))


S5_GB = 8
S5_CH = S5_GB * SSM_GROUP
S5_ST = S5_GB * SSM_STATE
S5_NGB = WIDTH // S5_CH
S5_LANES = 8
S5_SEG = SEQ // S5_LANES


def _s5_discretize(par):
    ar, ai, ldt = par[0:1], par[1:2], par[2:3]
    dt = jnp.exp(ldt)
    mag = jnp.exp(dt * ar)
    abr = mag * jnp.cos(dt * ai)
    abi = mag * jnp.sin(dt * ai)
    den = ar * ar + ai * ai
    cr = ((abr - 1.0) * ar + abi * ai) / den
    ci = (abi * ar - (abr - 1.0) * ai) / den
    return abr, abi, cr, ci


def _gelu(y):
    return 0.5 * y * (1.0 + jnp.tanh(math.sqrt(2.0 / math.pi) * (y + 0.044715 * (y * y * y))))


def _s5_prompt_body(u_ref, par_ref, bre_ref, bim_ref, cre_ref, cim_ref, d_ref, h0r_ref, h0i_ref,
                    o_ref, hr_ref, hi_ref, xr_scr, xi_scr):
    abr, abi, cr, ci = _s5_discretize(par_ref[...])
    bre, bim = bre_ref[...], bim_ref[...]
    bbr = (cr * bre - ci * bim).astype(BF16)
    bbi = (cr * bim + ci * bre).astype(BF16)
    u = u_ref[...]
    ub = u.astype(BF16)
    xr_scr[...] = _dot(ub, bbr)
    xi_scr[...] = _dot(ub, bbi)

    shape = (S5_LANES, S5_ST)
    ar8 = jnp.broadcast_to(abr, shape)
    ai8 = jnp.broadcast_to(abi, shape)

    def local_step(t, carry):
        xr, xi = carry
        rows = pl.ds(pl.multiple_of(t * S5_LANES, S5_LANES), S5_LANES)
        nr = ar8 * xr - ai8 * xi + xr_scr[rows, :]
        ni = ar8 * xi + ai8 * xr + xi_scr[rows, :]
        xr_scr[rows, :] = nr
        xi_scr[rows, :] = ni
        return nr, ni

    zero = jnp.zeros(shape, F32)
    fr, fi = lax.fori_loop(0, S5_SEG, local_step, (zero, zero), unroll=8)

    pr, pi = abr, abi
    for _ in range(int(math.log2(S5_SEG))):
        pr, pi = pr * pr - pi * pi, 2.0 * pr * pi
    hr, hi = h0r_ref[...], h0i_ref[...]
    in_r, in_i = [], []
    for j in range(S5_LANES):
        in_r.append(hr)
        in_i.append(hi)
        hr, hi = fr[j:j + 1] + pr * hr - pi * hi, fi[j:j + 1] + pr * hi + pi * hr
    hr_ref[...] = hr
    hi_ref[...] = hi
    cin_r = jnp.concatenate(in_r, axis=0)
    cin_i = jnp.concatenate(in_i, axis=0)

    def fix_step(t, carry):
        dr, di = carry
        rows = pl.ds(pl.multiple_of(t * S5_LANES, S5_LANES), S5_LANES)
        xr_scr[rows, :] = xr_scr[rows, :] + dr
        xi_scr[rows, :] = xi_scr[rows, :] + di
        return dr * ar8 - di * ai8, dr * ai8 + di * ar8

    first = (ar8 * cin_r - ai8 * cin_i, ar8 * cin_i + ai8 * cin_r)
    lax.fori_loop(0, S5_SEG, fix_step, first, unroll=8)

    y = (_dot(xr_scr[...].astype(BF16), cre_ref[...].astype(BF16))
         - _dot(xi_scr[...].astype(BF16), cim_ref[...].astype(BF16)) + d_ref[...] * u)
    o_ref[...] = _gelu(y)


def _s5_param_specs(layer):
    lead = (lambda *a: a[-1])

    def spec(shape):
        return pl.BlockSpec((None, None) + shape, lambda *a: (layer, lead(*a), 0, 0))

    return [spec((8, S5_ST)), spec((S5_CH, S5_ST)), spec((S5_CH, S5_ST)),
            spec((S5_ST, S5_CH)), spec((S5_ST, S5_CH)), spec((1, S5_CH))]


def _s5_prompt(u_perm, s5p, layer):
    zeros = jnp.zeros((BATCH, S5_NGB, 1, S5_ST), F32)
    state_spec = pl.BlockSpec((None, None, 1, S5_ST), lambda b, g: (b, g, 0, 0))
    out, hr, hi = pl.pallas_call(
        _s5_prompt_body,
        grid=(BATCH, S5_NGB),
        in_specs=[pl.BlockSpec((SEQ, S5_CH), lambda b, g: (b, g))] + _s5_param_specs(layer)
        + [state_spec, state_spec],
        out_specs=[pl.BlockSpec((SEQ, S5_CH), lambda b, g: (b, g)), state_spec, state_spec],
        out_shape=[jax.ShapeDtypeStruct((N_PROMPT, WIDTH), F32),
                   jax.ShapeDtypeStruct((BATCH, S5_NGB, 1, S5_ST), F32),
                   jax.ShapeDtypeStruct((BATCH, S5_NGB, 1, S5_ST), F32)],
        scratch_shapes=[pltpu.VMEM((SEQ, S5_ST), F32), pltpu.VMEM((SEQ, S5_ST), F32)],
        compiler_params=_params("parallel", "parallel"),
        name="s5_prompt",
    )(u_perm, *s5p, zeros, zeros)
    return out, hr, hi


def _s5_sample_body(u_ref, par_ref, bre_ref, bim_ref, cre_ref, cim_ref, d_ref, h0r_ref, h0i_ref,
                    o_ref, hr_ref, hi_ref):
    abr, abi, cr, ci = _s5_discretize(par_ref[...])
    bre, bim = bre_ref[...], bim_ref[...]
    bbr = (cr * bre - ci * bim).astype(BF16)
    bbi = (cr * bim + ci * bre).astype(BF16)
    u = u_ref[...]
    ub = u.astype(BF16)
    h0r, h0i = h0r_ref[...], h0i_ref[...]
    xr = _dot(ub, bbr) + abr * h0r - abi * h0i
    xi = _dot(ub, bbi) + abr * h0i + abi * h0r
    hr_ref[...] = xr
    hi_ref[...] = xi
    y = (_dot(xr.astype(BF16), cre_ref[...].astype(BF16))
         - _dot(xi.astype(BF16), cim_ref[...].astype(BF16)) + d_ref[...] * u)
    o_ref[...] = _gelu(y)


def _s5_sample(proj, s5p, h0r, h0i, layer):
    state_spec = pl.BlockSpec((DEC_BATCH, S5_ST), lambda g: (0, g))
    return pl.pallas_call(
        _s5_sample_body,
        grid=(S5_NGB,),
        in_specs=[pl.BlockSpec((DEC_BATCH, S5_CH), lambda g: (SAMPLE_ROW_BLOCK, C_U // S5_CH + g))]
        + _s5_param_specs(layer) + [state_spec, state_spec],
        out_specs=[pl.BlockSpec((DEC_BATCH, S5_CH), lambda g: (0, g)), state_spec, state_spec],
        out_shape=[jax.ShapeDtypeStruct((DEC_BATCH, WIDTH), F32),
                   jax.ShapeDtypeStruct((DEC_BATCH, S5_NGB * S5_ST), F32),
                   jax.ShapeDtypeStruct((DEC_BATCH, S5_NGB * S5_ST), F32)],
        compiler_params=_params("parallel"),
        name="s5_sample",
    )(proj, *s5p, h0r, h0i)


GDN_TS = 512
GDN_SCALE = HEAD_DIM ** -0.5


def _l2norm_heads(xc, o_ref):
    for h in range(HEADS):
        cs = slice(h * HEAD_DIM, (h + 1) * HEAD_DIM)
        seg = xc[:, cs]
        o_ref[:, cs] = seg * lax.rsqrt(jnp.sum(seg * seg, axis=-1, keepdims=True) + EPS)


def _conv_prompt_body(x_ref, prev_ref, w_ref, o_ref, ext_scr):
    i = pl.program_id(1)
    part = pl.program_id(2)
    x = x_ref[...]
    ext_scr[0:8, :] = jnp.where(i == 0, 0.0, prev_ref[...])
    ext_scr[8:8 + GDN_TS, :] = x
    w = w_ref[...]
    acc = x * w[3:4]
    for tap in range(CONV_TAPS - 1):
        back = CONV_TAPS - 1 - tap
        acc = acc + ext_scr[8 - back:8 - back + GDN_TS, :] * w[tap:tap + 1]
    xc = acc * _sigmoid(acc)

    @pl.when(part < 2)
    def _():
        _l2norm_heads(xc, o_ref)

    @pl.when(part == 2)
    def _():
        o_ref[...] = xc


def _conv_prompt(proj, conv_w, layer):
    n_t = SEQ // GDN_TS
    col0 = C_QKV // WIDTH

    def prev_index(b, i, p):
        return (jnp.maximum(b * (SEQ // 8) + i * (GDN_TS // 8) - 1, 0), col0 + p)

    return pl.pallas_call(
        _conv_prompt_body,
        grid=(BATCH, n_t, 3),
        in_specs=[
            pl.BlockSpec((GDN_TS, WIDTH), lambda b, i, p: (b * n_t + i, col0 + p)),
            pl.BlockSpec((8, WIDTH), prev_index),
            pl.BlockSpec((None, 8, WIDTH), lambda b, i, p: (layer, 0, p)),
        ],
        out_specs=pl.BlockSpec((GDN_TS, WIDTH), lambda b, i, p: (b * n_t + i, p)),
        out_shape=jax.ShapeDtypeStruct((N_PROMPT, CONV_WIDTH), F32),
        scratch_shapes=[pltpu.VMEM((GDN_TS + 8, WIDTH), F32)],
        compiler_params=_params("parallel", "parallel", "parallel"),
        name="gdn_conv_prompt",
    )(proj, proj, conv_w)


def _conv_sample_body(x_ref, hist_ref, w_ref, o_ref):
    part = pl.program_id(0)
    w = w_ref[...]
    acc = x_ref[...] * w[3:4]
    for tap in range(CONV_TAPS - 1):
        acc = acc + hist_ref[tap] * w[tap:tap + 1]
    xc = acc * _sigmoid(acc)

    @pl.when(part < 2)
    def _():
        _l2norm_heads(xc, o_ref)

    @pl.when(part == 2)
    def _():
        o_ref[...] = xc


def _conv_sample(proj, hist, conv_w, layer):
    col0 = C_QKV // WIDTH
    return pl.pallas_call(
        _conv_sample_body,
        grid=(3,),
        in_specs=[
            pl.BlockSpec((DEC_BATCH, WIDTH), lambda p: (SAMPLE_ROW_BLOCK, col0 + p)),
            pl.BlockSpec((None, CONV_TAPS - 1, DEC_BATCH, WIDTH), lambda p: (layer, 0, 0, p)),
            pl.BlockSpec((None, 8, WIDTH), lambda p: (layer, 0, p)),
        ],
        out_specs=pl.BlockSpec((DEC_BATCH, WIDTH), lambda p: (0, p)),
        out_shape=jax.ShapeDtypeStruct((DEC_BATCH, CONV_WIDTH), F32),
        compiler_params=_params("parallel"),
        name="gdn_conv_sample",
    )(proj, hist, conv_w)


GDN_CHUNK = 128
GDN_ROW_SLOTS = 16


def _gdn_gates_body(ba_ref, par_ref, bg_ref, gcrow_ref, *, rows, chunk):
    ba = ba_ref[...]
    par = par_ref[...]
    g = -jnp.exp(par[0:1]) * _softplus(ba + par[1:2])
    if chunk > 1:
        r = lax.broadcasted_iota(jnp.int32, (rows, rows), 0)
        c = lax.broadcasted_iota(jnp.int32, (rows, rows), 1)
        tri = jnp.where((r >= c) & ((r & -chunk) == (c & -chunk)), 1.0, 0.0)
        g = _dot(tri, g, precision=HIGHEST)
        g_t = g.T
        for cc in range(rows // chunk):
            gcrow_ref[cc * GDN_ROW_SLOTS:(cc + 1) * GDN_ROW_SLOTS, :] = (
                g_t[0:GDN_ROW_SLOTS, cc * chunk:(cc + 1) * chunk])
    lane = lax.broadcasted_iota(jnp.int32, ba.shape, 1)
    bg_ref[...] = jnp.where(lane < HEADS, _sigmoid(ba), g)


def _gdn_gates(proj, gate_par, layer, rows, n_blocks, first_block, chunk):
    out_specs = [pl.BlockSpec((rows, 128), lambda i: (i, 0))]
    out_shape = [jax.ShapeDtypeStruct((rows * n_blocks, 128), F32)]
    if chunk > 1:
        slots = rows // chunk * GDN_ROW_SLOTS
        out_specs.append(pl.BlockSpec((slots, chunk), lambda i: (i, 0)))
        out_shape.append(jax.ShapeDtypeStruct((slots * n_blocks, chunk), F32))
        body = functools.partial(_gdn_gates_body, rows=rows, chunk=chunk)
    else:
        body = functools.partial(_gdn_gates_body, gcrow_ref=None, rows=rows, chunk=chunk)
    return pl.pallas_call(
        body,
        grid=(n_blocks,),
        in_specs=[pl.BlockSpec((rows, 128), lambda i: (first_block + i, C_BA // 128)),
                  pl.BlockSpec((None, 8, 128), lambda i: (layer, 0, 0))],
        out_specs=out_specs,
        out_shape=out_shape,
        compiler_params=_params("parallel"),
        name="gdn_gates",
    )(proj, gate_par)


GDN_BASE = 8
GDN_GROUP = 2


def _gdn_chunk_body(q_ref, k_ref, v_ref, bg_ref, gcrow_ref, z_ref, s0_ref, gn_ref, o_ref, sfin_ref,
                    s_scr, mask_scr, u_scr, w_scr, qk_scr, qd_scr, kd_scr, *, n_tiles, tile_rows):
    ti = pl.program_id(1)
    n_chunks = tile_rows // GDN_CHUNK
    lanes = [slice(h * HEAD_DIM, (h + 1) * HEAD_DIM) for h in range(HEADS)]

    @pl.when(ti == 0)
    def _():
        s_scr[...] = s0_ref[...]

    row = lax.broadcasted_iota(jnp.int32, (GDN_CHUNK, GDN_CHUNK), 0)
    col = lax.broadcasted_iota(jnp.int32, (GDN_CHUNK, GDN_CHUNK), 1)

    def same_block(size):
        return (row & -size) == (col & -size)

    strict = row > col
    eye = jnp.where(row == col, 1.0, 0.0)
    mask_scr[0] = jnp.where(row >= col, 1.0, 0.0)
    mask_scr[1] = jnp.where(strict & same_block(GDN_BASE), 1.0, 0.0)
    merge_sizes = []
    size = GDN_BASE
    while size < GDN_CHUNK:
        mask_scr[2 + len(merge_sizes)] = jnp.where(strict & same_block(2 * size) & ~same_block(size), 1.0, 0.0)
        merge_sizes.append(size)
        size *= 2
    gain = gn_ref[...]

    tile = (GDN_CHUNK, HEAD_DIM)

    def prepare(pair, carry):
        units = [(dc, h) for dc in range(GDN_GROUP) for h in range(HEADS)]
        chains = range(len(units))
        rows, bg, gc_by_head = [], [], []
        for dc in range(GDN_GROUP):
            c = pair * GDN_GROUP + dc
            rows.append(pl.ds(pl.multiple_of(c * GDN_CHUNK, GDN_CHUNK), GDN_CHUNK))
            bg.append(bg_ref[rows[dc], :])
            gc_by_head.append(gcrow_ref[pl.ds(pl.multiple_of(c * GDN_ROW_SLOTS + HEADS, HEADS), HEADS), :])
        beta_of = lambda dc, h: jnp.broadcast_to(bg[dc][:, h:h + 1], tile)
        gc_of = lambda dc, h: jnp.broadcast_to(bg[dc][:, HEADS + h:HEADS + h + 1], tile)
        m = []
        for dc, h in units:
            cs = lanes[h]
            k = k_ref[rows[dc], cs]
            gc_lanes = gc_by_head[dc][h:h + 1, :]
            decay = mask_scr[0] * jnp.exp(jnp.minimum(gc_of(dc, h) - gc_lanes, 0.0))
            k16 = k.astype(BF16)
            m.append(_dot_nt((k * beta_of(dc, h)).astype(BF16), k16) * decay)
            qk_scr[rows[dc], cs] = (
                _dot_nt((q_ref[rows[dc], cs] * GDN_SCALE).astype(BF16), k16) * decay).astype(BF16)
        base = [m[i] * mask_scr[1] for i in chains]
        b16 = [base[i].astype(BF16) for i in chains]
        power = [_dot(b16[i], b16[i]).astype(BF16) for i in chains]
        inv = [eye - base[i] for i in chains]
        inv = [inv[i] + _dot(inv[i].astype(BF16), power[i]) for i in chains]
        for _ in range(int(math.log2(GDN_BASE)) - 2):
            power = [_dot(power[i], power[i]).astype(BF16) for i in chains]
            inv = [inv[i] + _dot(inv[i].astype(BF16), power[i]) for i in chains]
        for n in range(len(merge_sizes)):
            i16 = [inv[i].astype(BF16) for i in chains]
            half = [_dot(i16[i], (m[i] * mask_scr[2 + n]).astype(BF16)).astype(BF16) for i in chains]
            inv = [inv[i] - _dot(half[i], i16[i]) for i in chains]
        for i, (dc, h) in enumerate(units):
            cs = lanes[h]
            k = k_ref[rows[dc], cs]
            beta = beta_of(dc, h)
            gc = gc_of(dc, h)
            egc = jnp.exp(gc)
            rhs = jnp.concatenate([v_ref[rows[dc], cs] * beta, k * beta * egc], axis=-1)
            sol = rhs + _dot((inv[i] - eye).astype(BF16), rhs.astype(BF16))
            u_scr[rows[dc], cs] = sol[:, :HEAD_DIM]
            w_scr[rows[dc], cs] = sol[:, HEAD_DIM:].astype(BF16)
            gc_last = jnp.broadcast_to(gc[GDN_CHUNK - 1:GDN_CHUNK, :], tile)
            qd_scr[rows[dc], cs] = (q_ref[rows[dc], cs] * GDN_SCALE * egc).astype(BF16)
            kd_scr[rows[dc], cs] = (k * jnp.exp(gc_last - gc)).astype(BF16)
        return carry

    lax.fori_loop(0, n_chunks // GDN_GROUP, prepare, 0)

    def advance(c, carry):
        rows = pl.ds(pl.multiple_of(c * GDN_CHUNK, GDN_CHUNK), GDN_CHUNK)
        tail = pl.ds(pl.multiple_of(c * GDN_CHUNK + GDN_CHUNK - 8, 8), 8)
        heads = range(HEADS)
        gc_last = bg_ref[tail, :][7:8, :]
        state = [s_scr[h] for h in heads]
        s16 = [state[h].astype(BF16) for h in heads]
        vn16 = [(u_scr[rows, lanes[h]] - _dot(w_scr[rows, lanes[h]], s16[h])).astype(BF16) for h in heads]
        for h in heads:
            chunk_decay = jnp.exp(jnp.broadcast_to(gc_last[:, HEADS + h:HEADS + h + 1], (1, HEAD_DIM)))
            s_scr[h] = state[h] * chunk_decay + lax.dot_general(
                kd_scr[rows, lanes[h]], vn16[h], (((0,), (0,)), ((), ())), preferred_element_type=F32)
        for h in heads:
            cs = lanes[h]
            o = _dot(qd_scr[rows, cs], s16[h]) + _dot(qk_scr[rows, cs], vn16[h])
            zg = z_ref[rows, cs]
            o_ref[rows, cs] = (_rms(o, gain) * (zg * _sigmoid(zg))).astype(BF16)
        return carry

    lax.fori_loop(0, n_chunks, advance, 0)

    @pl.when(ti == n_tiles - 1)
    def _():
        sfin_ref[...] = s_scr[...]


def _gdn_chunks(qkv, bg, gcrow, z_src, z_col_block, s0, out_norm, layer, n_seq, seq_rows):
    tile_rows = min(seq_rows, GDN_TS)
    n_tiles = seq_rows // tile_rows
    n_masks = 2 + int(math.log2(GDN_CHUNK // GDN_BASE))

    def act(col_block):
        return pl.BlockSpec((tile_rows, WIDTH), lambda b, t: (b * n_tiles + t, col_block))

    slots = tile_rows // GDN_CHUNK * GDN_ROW_SLOTS
    state_spec = pl.BlockSpec((None, HEADS, HEAD_DIM, HEAD_DIM), lambda b, t: (b, 0, 0, 0))
    return pl.pallas_call(
        functools.partial(_gdn_chunk_body, n_tiles=n_tiles, tile_rows=tile_rows),
        grid=(n_seq, n_tiles),
        in_specs=[act(0), act(1), act(2),
                  pl.BlockSpec((tile_rows, 128), lambda b, t: (b * n_tiles + t, 0)),
                  pl.BlockSpec((slots, GDN_CHUNK), lambda b, t: (b * n_tiles + t, 0)),
                  act(z_col_block), state_spec,
                  pl.BlockSpec((None, 1, HEAD_DIM), lambda b, t: (layer, 0, 0))],
        out_specs=[act(0), state_spec],
        out_shape=[jax.ShapeDtypeStruct((n_seq * seq_rows, WIDTH), BF16),
                   jax.ShapeDtypeStruct((n_seq, HEADS, HEAD_DIM, HEAD_DIM), F32)],
        scratch_shapes=[pltpu.VMEM((HEADS, HEAD_DIM, HEAD_DIM), F32),
                        pltpu.VMEM((n_masks, GDN_CHUNK, GDN_CHUNK), F32),
                        pltpu.VMEM((tile_rows, WIDTH), F32),
                        pltpu.VMEM((tile_rows, WIDTH), BF16),
                        pltpu.VMEM((tile_rows, WIDTH), BF16),
                        pltpu.VMEM((tile_rows, WIDTH), BF16),
                        pltpu.VMEM((tile_rows, WIDTH), BF16)],
        compiler_params=_params("parallel", "arbitrary"),
        name="gdn_chunks",
    )(qkv, qkv, qkv, bg, gcrow, z_src, s0, out_norm)


GDN_STEP_COLS = CONV_WIDTH + 128 + WIDTH


def _gdn_step_body(x_ref, s0_ref, gn_ref, o_ref, s_ref):
    heads = range(HEADS)
    col = lambda part, h: slice(part * WIDTH + h * HEAD_DIM, part * WIDTH + (h + 1) * HEAD_DIM)
    gain = gn_ref[...]
    bg = x_ref[:, CONV_WIDTH:CONV_WIDTH + 128]
    z0 = CONV_WIDTH + 128
    k = [x_ref[:, col(1, h)] for h in heads]
    q = [x_ref[:, col(0, h)] * GDN_SCALE for h in heads]
    k16 = [k[h].astype(BF16) for h in heads]
    state = [s0_ref[h] for h in heads]
    s16 = [state[h].astype(BF16) for h in heads]
    k_s = [_dot(k16[h], s16[h]) for h in heads]
    q_s = [_dot(q[h].astype(BF16), s16[h]) for h in heads]
    decay = [jnp.exp(jnp.broadcast_to(bg[:, HEADS + h:HEADS + h + 1], (8, HEAD_DIM))) for h in heads]
    beta = [jnp.broadcast_to(bg[:, h:h + 1], (8, HEAD_DIM)) for h in heads]
    v_new = [beta[h] * (x_ref[:, col(2, h)] - decay[h] * k_s[h]) for h in heads]
    for h in heads:
        s_ref[h] = state[h] * decay[h][0:1, :] + lax.dot_general(
            k16[h], v_new[h].astype(BF16), (((0,), (0,)), ((), ())), preferred_element_type=F32)
    for h in heads:
        qk = jnp.sum(q[h].astype(BF16).astype(F32) * k16[h].astype(F32), axis=-1, keepdims=True)
        o = decay[h] * q_s[h] + qk * v_new[h].astype(BF16).astype(F32)
        zg = x_ref[:, z0 + h * HEAD_DIM:z0 + (h + 1) * HEAD_DIM]
        o_ref[:, h * HEAD_DIM:(h + 1) * HEAD_DIM] =(_rms(o, gain) * (zg * _sigmoid(zg))).astype(BF16)


def _gdn_step(x8, state, out_norm, layer):
    return pl.pallas_call(
        _gdn_step_body,
        grid=(DEC_BATCH,),
        in_specs=[pl.BlockSpec((None, 8, GDN_STEP_COLS), lambda b: (b, 0, 0)),
                  pl.BlockSpec((None, None, HEADS, HEAD_DIM, HEAD_DIM), lambda b: (layer, b, 0, 0, 0)),
                  pl.BlockSpec((None, 1, HEAD_DIM), lambda b: (layer, 0, 0))],
        out_specs=[pl.BlockSpec((None, 8, WIDTH), lambda b: (b, 0, 0)),
                   pl.BlockSpec((None, HEADS, HEAD_DIM, HEAD_DIM), lambda b: (b, 0, 0, 0))],
        out_shape=[jax.ShapeDtypeStruct((DEC_BATCH, 8, WIDTH), BF16),
                   jax.ShapeDtypeStruct((DEC_BATCH, HEADS, HEAD_DIM, HEAD_DIM), F32)],
        compiler_params=_params("parallel"),
        name="gdn_step",
    )(x8, state, out_norm)


def _pack_body(a_ref, b_ref, o_ref):
    n = pl.program_id(1)
    first_gate = C_GATE // PACK_TN
    last = PACK_TILES - 1
    shift = 2 * HEADS
    a = a_ref[...]
    shifted = jnp.concatenate([a[shift:, :], b_ref[...]], axis=0)
    row = lax.broadcasted_iota(jnp.int32, a.shape, 0)
    logits = jnp.where(row < shift, a, 0.0)
    tile = jnp.where(n < first_gate, a, jnp.where(n == last, logits, shifted))
    o_ref[...] = tile.T.astype(BF16)


def _pack_w_in(w_in):
    first_gate = C_GATE // PACK_TN
    last = PACK_TILES - 1
    shift = 2 * HEADS
    w_t = jnp.swapaxes(w_in, 1, 2)

    def a_index(d, n):
        return (d, jnp.where(n == last, first_gate, n), 0)

    def b_index(d, n):
        return (d, jnp.where((n >= first_gate) & (n < last), (n + 1) * (PACK_TN // shift), 0), 0)

    return pl.pallas_call(
        _pack_body,
        grid=(DEPTH, PACK_TILES),
        in_specs=[pl.BlockSpec((None, PACK_TN, D_MODEL), a_index),
                  pl.BlockSpec((None, shift, D_MODEL), b_index)],
        out_specs=pl.BlockSpec((None, D_MODEL, PACK_TN), lambda d, n: (d, 0, n)),
        out_shape=jax.ShapeDtypeStruct((DEPTH, D_MODEL, PROJ_COLS), BF16),
        compiler_params=_params("parallel", "arbitrary"),
        name="pack_w_in",
    )(w_t, w_t)


def _s5_params(a_re, a_im, log_dt, b_re, b_im, c_re, c_im, d):
    eye = jnp.eye(S5_GB, dtype=F32)

    def state_rows(x):
        return x.reshape(DEPTH, S5_NGB, 1, S5_ST)

    par = jnp.concatenate(
        [state_rows(a_re), state_rows(a_im),
         state_rows(jnp.repeat(log_dt, SSM_STATE, axis=-1)),
         jnp.zeros((DEPTH, S5_NGB, 5, S5_ST), F32)], axis=2)

    def b_blockdiag(b):
        b = b.reshape(DEPTH, S5_NGB, S5_GB, SSM_STATE, SSM_GROUP)
        return jnp.einsum('dbgpc,gh->dbgchp', b, eye).reshape(DEPTH, S5_NGB, S5_CH, S5_ST)

    def c_blockdiag(c):
        c = c.reshape(DEPTH, S5_NGB, S5_GB, SSM_GROUP, SSM_STATE)
        return jnp.einsum('dbgcp,gh->dbgphc', c, eye).reshape(DEPTH, S5_NGB, S5_ST, S5_CH)

    return (par, b_blockdiag(b_re), b_blockdiag(b_im), c_blockdiag(c_re), c_blockdiag(c_im),
            d.reshape(DEPTH, S5_NGB, 1, S5_CH))


def _segment_major(x):
    w = x.shape[-1]
    return x.reshape(BATCH, S5_LANES, S5_SEG, w).transpose(0, 2, 1, 3).reshape(N_PROMPT, w)


def _token_major(x):
    w = x.shape[-1]
    return x.reshape(BATCH, S5_SEG, S5_LANES, w).transpose(0, 2, 1, 3).reshape(N_PROMPT, w)


def _pad_rows(x):
    return jnp.concatenate([x, jnp.zeros((ROWS - x.shape[0], x.shape[1]), x.dtype)], axis=0)


def kernel(x_prompt, x_sample, cache_k, cache_v, page_table, state_ssm, state_conv, state_gdn, ffn1_norm, ffn1_w_gate, ffn1_w_up, ffn1_w_down, mix_norm, w_in, sb_bias, ssm_a_re, ssm_a_im, ssm_log_dt, ssm_b_re, ssm_b_im, ssm_c_re, ssm_c_im, ssm_d, ssm_w_glu, gdn_conv_w, gdn_a_log, gdn_dt_bias, gdn_out_norm, w_branch, w_out, ffn2_norm, ffn2_w_gate, ffn2_w_up, ffn2_w_down, final_norm):
    bf = lambda w: w.astype(BF16)
    gain3 = lambda g: g.reshape(DEPTH, 1, D_MODEL)
    w_in_p = _pack_w_in(w_in)
    ffn1 = (gain3(ffn1_norm), bf(ffn1_w_gate), bf(ffn1_w_up), bf(ffn1_w_down))
    ffn2 = (gain3(ffn2_norm), bf(ffn2_w_gate), bf(ffn2_w_up), bf(ffn2_w_down))
    mix_gain = gain3(mix_norm)
    w_glu, w_br, w_o = bf(ssm_w_glu), bf(w_branch).reshape(DEPTH, 3, WIDTH, D_MODEL), bf(w_out)
    s5p = _s5_params(ssm_a_re, ssm_a_im, ssm_log_dt, ssm_b_re, ssm_b_im, ssm_c_re, ssm_c_im, ssm_d)
    bias_flat = sb_bias.reshape(DEPTH * HEADS)
    bias_lanes = jnp.tile(sb_bias, (1, PAGE)).reshape(DEPTH, 1, DEC_LANES)
    page_flat = page_table.reshape(DEC_BATCH * N_PAGES)
    conv_w8 = jnp.pad(gdn_conv_w, ((0, 0), (0, 8 - CONV_TAPS), (0, 0)))
    conv_hist = state_conv.transpose(0, 2, 1, 3)
    lane_pad = jnp.zeros((DEPTH, 128 - 2 * HEADS), F32)
    gate_par = jnp.stack([jnp.concatenate([jnp.zeros((DEPTH, HEADS), F32), gdn_a_log, lane_pad], axis=1),
                          jnp.concatenate([jnp.zeros((DEPTH, HEADS), F32), gdn_dt_bias, lane_pad], axis=1)]
                         + [jnp.zeros((DEPTH, 128), F32)] * 6, axis=1)
    out_norm = gdn_out_norm.reshape(DEPTH, 1, HEAD_DIM)
    ssm_h0r = state_ssm[..., 0].reshape(DEPTH, DEC_BATCH, WIDTH * SSM_STATE // SSM_GROUP)
    ssm_h0i = state_ssm[..., 1].reshape(DEPTH, DEC_BATCH, WIDTH * SSM_STATE // SSM_GROUP)

    x = _pad_rows(jnp.concatenate([x_prompt.reshape(N_PROMPT, D_MODEL),
                                   x_sample.reshape(DEC_BATCH, D_MODEL)], axis=0))
    sample = slice(N_PROMPT, N_PROMPT + DEC_BATCH)
    ks, vs, sp, ss, cp, cs_, gp, gs = ([] for _ in range(8))
    k_all = jnp.zeros((DEPTH, N_PROMPT, WIDTH), F32)
    v_all = jnp.zeros((DEPTH, N_PROMPT, WIDTH), F32)
    for l in range(DEPTH):
        x = _ffn(x, *ffn1, l)
        proj = _in_proj(x, mix_gain, w_in_p, l)

        oa_p, k_all, v_all = _sb_prompt(proj, bias_flat, l, k_all, v_all)
        q_s = proj[sample, C_Q:C_Q + WIDTH].reshape(DEC_BATCH, HEADS, HEAD_DIM)
        oa_s = _sb_decode(q_s, cache_k, cache_v, page_flat, bias_lanes, l)
        oa = _pad_rows(jnp.concatenate([oa_p, oa_s.reshape(DEC_BATCH, WIDTH).astype(BF16)], axis=0))

        gb_p, hr_p, hi_p = _s5_prompt(_segment_major(proj[:N_PROMPT, C_U:C_U + WIDTH]), s5p, l)
        gb_s, hr_s, hi_s = _s5_sample(proj, s5p, ssm_h0r[l], ssm_h0i[l], l)
        ob = _glu(_pad_rows(jnp.concatenate([_token_major(gb_p), gb_s], axis=0)), w_glu, l)

        qkv_p = _conv_prompt(proj, conv_w8, l)
        bg_p, gcrow_p = _gdn_gates(proj, gate_par, l, GDN_TS, N_PROMPT // GDN_TS, 0, GDN_CHUNK)
        oc_p, gdn_p = _gdn_chunks(qkv_p, bg_p, gcrow_p, proj, C_Z // WIDTH,
                                  jnp.zeros((BATCH, HEADS, HEAD_DIM, HEAD_DIM), F32), out_norm, l, BATCH, SEQ)
        qkv_s = _conv_sample(proj, conv_hist, conv_w8, l)
        (bg_s,) = _gdn_gates(proj, gate_par, l, DEC_BATCH, 1, SAMPLE_ROW_BLOCK, 1)
        step_in = jnp.concatenate([qkv_s, bg_s, proj[sample, C_Z:C_Z + WIDTH]], axis=1)
        oc_s, gdn_s = _gdn_step(jnp.pad(step_in[:, None, :], ((0, 0), (0, 7), (0, 0))), state_gdn, out_norm, l)
        oc = _pad_rows(jnp.concatenate([oc_p, oc_s[:, 0, :]], axis=0))

        x = _merge(x, oa, ob, oc, proj, w_br, w_o, l)
        x = _ffn(x, *ffn2, l)

        ks.append(proj[sample, C_K:C_K + WIDTH].reshape(DEC_BATCH, 1, HEADS, HEAD_DIM))
        vs.append(proj[sample, C_V:C_V + WIDTH].reshape(DEC_BATCH, 1, HEADS, HEAD_DIM))
        state = lambda r, i, n: jnp.stack([r.reshape(n, WIDTH // SSM_GROUP, SSM_STATE),
                                           i.reshape(n, WIDTH // SSM_GROUP, SSM_STATE)], axis=-1)
        sp.append(state(hr_p, hi_p, BATCH))
        ss.append(state(hr_s, hi_s, DEC_BATCH))
        qkv_cols = slice(C_QKV, C_QKV + CONV_WIDTH)
        cp.append(jnp.stack([proj[(b + 1) * SEQ - (CONV_TAPS - 1):(b + 1) * SEQ, qkv_cols] for b in range(BATCH)]))
        cs_.append(jnp.concatenate([state_conv[l][:, 1:], proj[sample, qkv_cols][:, None, :]], axis=1))
        gp.append(gdn_p)
        gs.append(gdn_s)

    gain = final_norm.reshape(1, D_MODEL)
    y_p = _final_norm(x, gain, GDN_TS, N_PROMPT // GDN_TS, 0)
    y_s = _final_norm(x, gain, DEC_BATCH, 1, SAMPLE_ROW_BLOCK)
    kv_shape = (DEPTH, BATCH, SEQ, HEADS, HEAD_DIM)
    return (y_p.reshape(BATCH, SEQ, D_MODEL), y_s.reshape(DEC_BATCH, 1, D_MODEL),
            k_all.reshape(kv_shape), v_all.reshape(kv_shape), jnp.stack(ks), jnp.stack(vs),
            jnp.stack(sp), jnp.stack(ss), jnp.stack(cp), jnp.stack(cs_),
            jnp.stack(gp), jnp.stack(gs))
```

```python
import functools
import math

import jax
import jax.numpy as jnp
from jax import lax
from jax.experimental import pallas as pl
from jax.experimental.pallas import tpu as pltpu

F32 = jnp.float32
BF16 = jnp.bfloat16
HIGHEST = lax.Precision.HIGHEST

D_MODEL = 2048
BATCH = 4
SEQ = 2048
DEPTH = 2
DEC_BATCH = 32
PAGE = 128
N_PAGES = 64
HEADS = 8
HEAD_DIM = 128
WIDTH = HEADS * HEAD_DIM
SSM_GROUP = 16
SSM_STATE = 64
CONV_TAPS = 4
CONV_WIDTH = 3 * WIDTH
FFN_DIM = 5632
EPS = 1e-6

N_PROMPT = BATCH * SEQ
ROWS = 8320
TM = ROWS // 10
TM_GLU = ROWS // 20
SAMPLE_ROW_BLOCK = N_PROMPT // DEC_BATCH

C_Q, C_K, C_V, C_U, C_QKV, C_Z, C_GATE, C_BA = 0, 1024, 2048, 3072, 4096, 7168, 8192, 14336
PROJ_COLS = 14592
TN_PROJ = 768
PACK_TN = 512
PACK_TILES = -(-PROJ_COLS // PACK_TN)
TF = 256
TM_BIG = ROWS // 8
TM_MERGE = ROWS // 52

VMEM_LIMIT = 56 * 1024 * 1024


def _params(*sem):
    return pltpu.CompilerParams(dimension_semantics=sem, vmem_limit_bytes=VMEM_LIMIT)


def _rms(x, gain):
    ms = jnp.mean(x * x, axis=-1, keepdims=True)
    return x * lax.rsqrt(ms + EPS) * gain


def _softplus(x):
    return jnp.maximum(x, 0.0) + jnp.log(1.0 + jnp.exp(-jnp.abs(x)))


def _sigmoid(x):
    return 1.0 / (1.0 + jnp.exp(-x))


def _dot(a, b, **kw):
    return jnp.dot(a, b, preferred_element_type=F32, **kw)


def _dot_nt(a, b, **kw):
    return lax.dot_general(a, b, (((1,), (1,)), ((), ())), preferred_element_type=F32, **kw)


def _ffn_body(x_ref, g_ref, wg_ref, wu_ref, wd_ref, o_ref, h_scr):
    @pl.when(pl.program_id(1) == 0)
    def _():
        x = x_ref[...]
        h_scr[...] = _rms(x, g_ref[...]).astype(BF16)
        o_ref[...] = x

    h = h_scr[...]
    a = _dot(h, wg_ref[...].astype(BF16))
    b = _dot(h, wu_ref[...].astype(BF16))
    act = (a * _sigmoid(a)) * b * 0.5
    o_ref[...] += _dot(act.astype(BF16), wd_ref[...].astype(BF16))


def _ffn(x, gain, wg, wu, wd, layer):
    return pl.pallas_call(
        _ffn_body,
        grid=(ROWS // TM, FFN_DIM // TF),
        in_specs=[
            pl.BlockSpec((TM, D_MODEL), lambda i, f: (i, 0)),
            pl.BlockSpec((None, 1, D_MODEL), lambda i, f: (layer, 0, 0)),
            pl.BlockSpec((None, D_MODEL, TF), lambda i, f: (layer, 0, f)),
            pl.BlockSpec((None, D_MODEL, TF), lambda i, f: (layer, 0, f)),
            pl.BlockSpec((None, TF, D_MODEL), lambda i, f: (layer, f, 0)),
        ],
        out_specs=pl.BlockSpec((TM, D_MODEL), lambda i, f: (i, 0)),
        out_shape=jax.ShapeDtypeStruct((ROWS, D_MODEL), F32),
        scratch_shapes=[pltpu.VMEM((TM, D_MODEL), BF16)],
        compiler_params=_params("parallel", "arbitrary"),
        name="ffn",
    )(x, gain, wg, wu, wd)


def _proj_body(x_ref, g_ref, w_ref, o_ref, h_scr):
    @pl.when(pl.program_id(1) == 0)
    def _():
        h_scr[...] = _rms(x_ref[...], g_ref[...]).astype(BF16)

    o_ref[...] = _dot(h_scr[...], w_ref[...])


def _in_proj(x, gain, w, layer):
    return pl.pallas_call(
        _proj_body,
        grid=(ROWS // TM_BIG, PROJ_COLS // TN_PROJ),
        in_specs=[
            pl.BlockSpec((TM_BIG, D_MODEL), lambda i, n: (i, 0), pipeline_mode=pl.Buffered(1)),
            pl.BlockSpec((None, 1, D_MODEL), lambda i, n: (layer, 0, 0)),
            pl.BlockSpec((None, D_MODEL, TN_PROJ), lambda i, n: (layer, 0, n)),
        ],
        out_specs=pl.BlockSpec((TM_BIG, TN_PROJ), lambda i, n: (i, n)),
        out_shape=jax.ShapeDtypeStruct((ROWS, PROJ_COLS), F32),
        scratch_shapes=[pltpu.VMEM((TM_BIG, D_MODEL), BF16)],
        compiler_params=_params("parallel", "arbitrary"),
        name="in_proj",
    )(x, gain, w)


def _glu_body(g_ref, w_ref, o_ref):
    g = g_ref[...]
    o_ref[...] = (g * _sigmoid(_dot(g.astype(BF16), w_ref[...]))).astype(BF16)


def _glu(g, w, layer):
    return pl.pallas_call(
        _glu_body,
        grid=(ROWS // TM_GLU,),
        in_specs=[
            pl.BlockSpec((TM_GLU, WIDTH), lambda i: (i, 0)),
            pl.BlockSpec((None, WIDTH, WIDTH), lambda i: (layer, 0, 0)),
        ],
        out_specs=pl.BlockSpec((TM_GLU, WIDTH), lambda i: (i, 0)),
        out_shape=jax.ShapeDtypeStruct((ROWS, WIDTH), BF16),
        compiler_params=_params("parallel"),
        name="glu",
    )(g, w)


def _merge_body(x_ref, oa_ref, ob_ref, oc_ref, ga_ref, gb_ref, gc_ref, wa_ref, wb_ref, wc_ref, wo_ref, o_ref):
    m = (_sigmoid(ga_ref[...]) * _dot(oa_ref[...], wa_ref[...])
         + _sigmoid(gb_ref[...]) * _dot(ob_ref[...], wb_ref[...])
         + _sigmoid(gc_ref[...]) * _dot(oc_ref[...], wc_ref[...]))
    o_ref[...] = x_ref[...] + _dot(m.astype(BF16), wo_ref[...])


def _merge(x, oa, ob, oc, proj, w_branch, w_out, layer):
    gate_blk = C_GATE // D_MODEL
    o_spec = pl.BlockSpec((TM_MERGE, WIDTH), lambda i: (i, 0))

    def gate_spec(which):
        return pl.BlockSpec((TM_MERGE, D_MODEL), lambda i: (i, gate_blk + which))

    def w_spec(which):
        return pl.BlockSpec((None, None, WIDTH, D_MODEL), lambda i: (layer, which, 0, 0),
                            pipeline_mode=pl.Buffered(1))

    return pl.pallas_call(
        _merge_body,
        grid=(ROWS // TM_MERGE,),
        in_specs=[
            pl.BlockSpec((TM_MERGE, D_MODEL), lambda i: (i, 0)),
            o_spec, o_spec, o_spec,
            gate_spec(0), gate_spec(1), gate_spec(2),
            w_spec(0), w_spec(1), w_spec(2),
            pl.BlockSpec((None, D_MODEL, D_MODEL), lambda i: (layer, 0, 0), pipeline_mode=pl.Buffered(1)),
        ],
        out_specs=pl.BlockSpec((TM_MERGE, D_MODEL), lambda i: (i, 0)),
        out_shape=jax.ShapeDtypeStruct((ROWS, D_MODEL), F32),
        compiler_params=_params("parallel"),
        name="merge",
    )(x, oa, ob, oc, proj, proj, proj, w_branch, w_branch, w_branch, w_out)


def _final_norm_body(x_ref, g_ref, o_ref):
    o_ref[...] = _rms(x_ref[...], g_ref[...])


def _final_norm(x, gain, rows, n_blocks, first_block):
    return pl.pallas_call(
        _final_norm_body,
        grid=(n_blocks,),
        in_specs=[pl.BlockSpec((rows, D_MODEL), lambda i: (first_block + i, 0)),
                  pl.BlockSpec((1, D_MODEL), lambda i: (0, 0))],
        out_specs=pl.BlockSpec((rows, D_MODEL), lambda i: (i, 0)),
        out_shape=jax.ShapeDtypeStruct((rows * n_blocks, D_MODEL), F32),
        compiler_params=_params("parallel"),
        name="final_norm",
    )(x, gain)


SB_BQ = 256
SB_BK = 256
SB_HB = 4
SB_SCALE = HEAD_DIM ** -0.5


def _stick_terms(z):
    t = jnp.log(1.0 + jnp.exp(-jnp.abs(z)))
    return jnp.minimum(z, 0.0) - t, jnp.minimum(-z, 0.0) - t


LOG2E = 1.0 / math.log(2.0)


def _sb_prompt_body(bias_ref, q_ref, k_ref, v_ref, *rest, layer):
    o_ref, k_all_ref, v_all_ref, q_scr, acc_scr, run_scr = rest[2:]
    hg = pl.program_id(1)
    qi = pl.program_id(2)

    @pl.when(qi == 0)
    def _():
        k_all_ref[...] = k_ref[...]
        v_all_ref[...] = v_ref[...]

    heads = range(SB_HB)
    lanes = [slice(hh * HEAD_DIM, (hh + 1) * HEAD_DIM) for hh in heads]
    row = lax.broadcasted_iota(jnp.int32, (SB_BQ, SB_BK), 0)
    col = lax.broadcasted_iota(jnp.int32, (SB_BQ, SB_BK), 1)
    q_pos = qi * SB_BQ + row
    r2 = lax.broadcasted_iota(jnp.int32, (SB_BK, SB_BK), 0)
    c2 = lax.broadcasted_iota(jnp.int32, (SB_BK, SB_BK), 1)
    later_key = jnp.where(r2 > c2, 1.0, 0.0).astype(BF16)
    bias2 = [bias_ref[layer * HEADS + hg * SB_HB + hh] * LOG2E for hh in heads]

    for hh in heads:
        q_scr[hh] = (q_ref[:, lanes[hh]] * (SB_SCALE * LOG2E)).astype(BF16)
    acc_scr[...] = jnp.zeros_like(acc_scr)
    run_scr[...] = jnp.zeros_like(run_scr)

    def visit(blocks, masked):
        units = [(n, hh) for n in range(len(blocks)) for hh in heads]
        chains = range(len(units))
        k_rows = [pl.ds(pl.multiple_of(j * SB_BK, SB_BK), SB_BK) for j in blocks]
        mask = (blocks[0] * SB_BK + col) < q_pos
        z = [_dot_nt(q_scr[hh], k_ref[k_rows[n], lanes[hh]].astype(BF16)) + bias2[hh] for n, hh in units]
        t = [jnp.log(1.0 + jnp.exp2(-jnp.abs(z[i]))) * LOG2E for i in chains]
        log_hit = [jnp.minimum(z[i], 0.0) - t[i] for i in chains]
        log_fail = [jnp.minimum(-z[i], 0.0) - t[i] for i in chains]
        if masked:
            log_fail = [jnp.where(mask, log_fail[i], 0.0) for i in chains]
        suffix = [_dot(log_fail[i].astype(BF16), later_key) for i in chains]
        w = []
        run = [run_scr[hh] for hh in heads]
        for i, (n, hh) in enumerate(units):
            w_i = jnp.exp2(log_hit[i] + suffix[i] + run[hh])
            block_total = suffix[i][:, 0:1] + log_fail[i][:, 0:1]
            run[hh] = run[hh] + jnp.broadcast_to(block_total, (SB_BQ, SB_BK))
            w.append(jnp.where(mask, w_i, 0.0) if masked else w_i)
        for hh in heads:
            run_scr[hh] = run[hh]
        for hh in heads:
            acc = acc_scr[hh]
            for i, (n, h2) in enumerate(units):
                if h2 == hh:
                    acc = acc + _dot(w[i].astype(BF16), v_ref[k_rows[n], lanes[hh]].astype(BF16))
            acc_scr[hh] = acc

    visit([qi], True)

    def below_diagonal(it, carry):
        visit([qi - 1 - 2 * it, qi - 2 - 2 * it], False)
        return carry

    lax.fori_loop(0, qi // 2, below_diagonal, 0)

    @pl.when(qi % 2 == 1)
    def _():
        visit([0], False)
    for hh in heads:
        o_ref[:, lanes[hh]] = acc_scr[hh].astype(BF16)


def _sb_prompt(proj, bias_flat, layer, k_all, v_all):
    hw = SB_HB * HEAD_DIM
    n_hg = HEADS // SB_HB
    nq = SEQ // SB_BQ
    all_spec = pl.BlockSpec((None, SEQ, hw), lambda b, h, i, bias: (layer, b, h))
    all_shape = jax.ShapeDtypeStruct((DEPTH, N_PROMPT, WIDTH), F32)
    grid_spec = pltpu.PrefetchScalarGridSpec(
        num_scalar_prefetch=1,
        grid=(BATCH, n_hg, nq),
        in_specs=[
            pl.BlockSpec((SB_BQ, hw), lambda b, h, i, bias: (b * nq + i, C_Q // hw + h)),
            pl.BlockSpec((SEQ, hw), lambda b, h, i, bias: (b, C_K // hw + h)),
            pl.BlockSpec((SEQ, hw), lambda b, h, i, bias: (b, C_V // hw + h)),
            pl.BlockSpec(memory_space=pl.ANY),
            pl.BlockSpec(memory_space=pl.ANY),
        ],
        out_specs=[pl.BlockSpec((SB_BQ, hw), lambda b, h, i, bias: (b * nq + i, h)), all_spec, all_spec],
        scratch_shapes=[pltpu.VMEM((SB_HB, SB_BQ, HEAD_DIM), BF16),
                        pltpu.VMEM((SB_HB, SB_BQ, HEAD_DIM), F32),
                        pltpu.VMEM((SB_HB, SB_BQ, SB_BK), F32)],
    )
    return pl.pallas_call(
        functools.partial(_sb_prompt_body, layer=layer),
        grid_spec=grid_spec,
        out_shape=[jax.ShapeDtypeStruct((N_PROMPT, WIDTH), BF16), all_shape, all_shape],
        input_output_aliases={4: 1, 5: 2},
        compiler_params=_params("parallel", "parallel", "arbitrary"),
        name="sb_prompt",
    )(bias_flat, proj, proj, proj, k_all, v_all)


DEC_PAGES = 16
DEC_STEPS = N_PAGES // DEC_PAGES
DEC_LANES = PAGE * HEADS


def _lane_shift_up(x, s):
    n = x.shape[1]
    lane = lax.broadcasted_iota(jnp.int32, x.shape, 1)
    return jnp.where(lane < n - s, pltpu.roll(x, n - s, 1), 0.0)


def _sb_decode_body(pt_ref, q_ref, bias_ref, *refs):
    k_refs = refs[:DEC_PAGES]
    v_refs = refs[DEC_PAGES:2 * DEC_PAGES]
    o_ref, acc_scr, run_scr = refs[2 * DEC_PAGES:]
    g = pl.program_id(1)

    @pl.when(g == 0)
    def _():
        acc_scr[...] = jnp.zeros_like(acc_scr)
        run_scr[...] = jnp.zeros_like(run_scr)

    sub = lax.broadcasted_iota(jnp.int32, (HEADS, DEC_LANES), 0)
    lane = lax.broadcasted_iota(jnp.int32, (HEADS, DEC_LANES), 1)
    own_head = (lane & (HEADS - 1)) == sub
    q = q_ref[...].astype(BF16)
    rows = []
    for i in range(DEC_PAGES):
        k2 = k_refs[i][...].reshape(DEC_LANES, HEAD_DIM).astype(BF16)
        zt = _dot_nt(q, k2)
        rows.append(jnp.sum(jnp.where(own_head, zt, 0.0), axis=0, keepdims=True))
    z = jnp.concatenate(rows, axis=0) * SB_SCALE + bias_ref[...]
    log_hit, log_fail = _stick_terms(z)

    suffix = log_fail
    for s in (HEADS, 4 * HEADS):
        suffix = (suffix + _lane_shift_up(suffix, s)) + (_lane_shift_up(suffix, 2 * s) + _lane_shift_up(suffix, 3 * s))
    s = 16 * HEADS
    while s < DEC_LANES:
        suffix = suffix + jnp.concatenate([suffix[:, s:], jnp.zeros((DEC_PAGES, s), F32)], axis=1)
        s *= 2
    tile_lane = lax.broadcasted_iota(jnp.int32, (DEC_PAGES, 16 * HEADS), 1)
    tot = jnp.where(tile_lane < HEADS, suffix[:, :16 * HEADS], 0.0)
    for s in (HEADS, 4 * HEADS):
        tot = (tot + pltpu.roll(tot, s, 1)) + (pltpu.roll(tot, 2 * s, 1) + pltpu.roll(tot, 3 * s, 1))
    tot = jnp.concatenate([tot] * (DEC_LANES // (16 * HEADS)), axis=1)
    page_r = lax.broadcasted_iota(jnp.int32, (DEC_PAGES, DEC_PAGES), 0)
    page_c = lax.broadcasted_iota(jnp.int32, (DEC_PAGES, DEC_PAGES), 1)
    later = jnp.where(page_c > page_r, 1.0, 0.0)
    after = _dot(later, tot, precision=HIGHEST)
    run = run_scr[...]
    log_between = (suffix - log_fail) + after + run
    run_scr[...] = run + jnp.sum(tot, axis=0, keepdims=True)
    w = jnp.exp(log_hit + log_between)

    acc = acc_scr[...]
    for i in range(DEC_PAGES):
        wm = jnp.where(own_head, jnp.broadcast_to(w[i:i + 1, :], (HEADS, DEC_LANES)), 0.0).astype(BF16)
        v2 = v_refs[i][...].reshape(DEC_LANES, HEAD_DIM).astype(BF16)
        acc = acc + _dot(wm, v2)
    acc_scr[...] = acc

    @pl.when(g == DEC_STEPS - 1)
    def _():
        o_ref[...] = acc


def _sb_decode(q, cache_k, cache_v, page_table_flat, bias_lanes, layer):
    def page_spec(i):
        def index(b, g, pt):
            return (layer, pt[b * N_PAGES + (DEC_STEPS - 1 - g) * DEC_PAGES + i], 0, 0, 0)
        return pl.BlockSpec((None, None, PAGE, HEADS, HEAD_DIM), index)

    grid_spec = pltpu.PrefetchScalarGridSpec(
        num_scalar_prefetch=1,
        grid=(DEC_BATCH, DEC_STEPS),
        in_specs=[pl.BlockSpec((None, HEADS, HEAD_DIM), lambda b, g, pt: (b, 0, 0)),
                  pl.BlockSpec((None, 1, DEC_LANES), lambda b, g, pt: (layer, 0, 0))]
        + [page_spec(i) for i in range(DEC_PAGES)] * 2,
        out_specs=pl.BlockSpec((None, HEADS, HEAD_DIM), lambda b, g, pt: (b, 0, 0)),
        scratch_shapes=[pltpu.VMEM((HEADS, HEAD_DIM), F32), pltpu.VMEM((1, DEC_LANES), F32)],
    )
    return pl.pallas_call(
        _sb_decode_body,
        grid_spec=grid_spec,
        out_shape=jax.ShapeDtypeStruct((DEC_BATCH, HEADS, HEAD_DIM), F32),
        compiler_params=_params("parallel", "arbitrary"),
        name="sb_decode",
    )(page_table_flat, q, bias_lanes, *([cache_k] * DEC_PAGES), *([cache_v] * DEC_PAGES))


S5_GB = 8
S5_CH = S5_GB * SSM_GROUP
S5_ST = S5_GB * SSM_STATE
S5_NGB = WIDTH // S5_CH
S5_LANES = 8
S5_SEG = SEQ // S5_LANES


def _s5_discretize(par):
    ar, ai, ldt = par[0:1], par[1:2], par[2:3]
    dt = jnp.exp(ldt)
    mag = jnp.exp(dt * ar)
    abr = mag * jnp.cos(dt * ai)
    abi = mag * jnp.sin(dt * ai)
    den = ar * ar + ai * ai
    cr = ((abr - 1.0) * ar + abi * ai) / den
    ci = (abi * ar - (abr - 1.0) * ai) / den
    return abr, abi, cr, ci


def _gelu(y):
    return 0.5 * y * (1.0 + jnp.tanh(math.sqrt(2.0 / math.pi) * (y + 0.044715 * (y * y * y))))


def _s5_prompt_body(u_ref, par_ref, bre_ref, bim_ref, cre_ref, cim_ref, d_ref, h0r_ref, h0i_ref,
                    o_ref, hr_ref, hi_ref, xr_scr, xi_scr):
    abr, abi, cr, ci = _s5_discretize(par_ref[...])
    bre, bim = bre_ref[...], bim_ref[...]
    bbr = (cr * bre - ci * bim).astype(BF16)
    bbi = (cr * bim + ci * bre).astype(BF16)
    u = u_ref[...]
    ub = u.astype(BF16)
    xr_scr[...] = _dot(ub, bbr)
    xi_scr[...] = _dot(ub, bbi)

    shape = (S5_LANES, S5_ST)
    ar8 = jnp.broadcast_to(abr, shape)
    ai8 = jnp.broadcast_to(abi, shape)

    def local_step(t, carry):
        xr, xi = carry
        rows = pl.ds(pl.multiple_of(t * S5_LANES, S5_LANES), S5_LANES)
        nr = ar8 * xr - ai8 * xi + xr_scr[rows, :]
        ni = ar8 * xi + ai8 * xr + xi_scr[rows, :]
        xr_scr[rows, :] = nr
        xi_scr[rows, :] = ni
        return nr, ni

    zero = jnp.zeros(shape, F32)
    fr, fi = lax.fori_loop(0, S5_SEG, local_step, (zero, zero), unroll=8)

    pr, pi = abr, abi
    for _ in range(int(math.log2(S5_SEG))):
        pr, pi = pr * pr - pi * pi, 2.0 * pr * pi
    hr, hi = h0r_ref[...], h0i_ref[...]
    in_r, in_i = [], []
    for j in range(S5_LANES):
        in_r.append(hr)
        in_i.append(hi)
        hr, hi = fr[j:j + 1] + pr * hr - pi * hi, fi[j:j + 1] + pr * hi + pi * hr
    hr_ref[...] = hr
    hi_ref[...] = hi
    cin_r = jnp.concatenate(in_r, axis=0)
    cin_i = jnp.concatenate(in_i, axis=0)

    def fix_step(t, carry):
        dr, di = carry
        rows = pl.ds(pl.multiple_of(t * S5_LANES, S5_LANES), S5_LANES)
        xr_scr[rows, :] = xr_scr[rows, :] + dr
        xi_scr[rows, :] = xi_scr[rows, :] + di
        return dr * ar8 - di * ai8, dr * ai8 + di * ar8

    first = (ar8 * cin_r - ai8 * cin_i, ar8 * cin_i + ai8 * cin_r)
    lax.fori_loop(0, S5_SEG, fix_step, first, unroll=8)

    y = (_dot(xr_scr[...].astype(BF16), cre_ref[...].astype(BF16))
         - _dot(xi_scr[...].astype(BF16), cim_ref[...].astype(BF16)) + d_ref[...] * u)
    o_ref[...] = _gelu(y)


def _s5_param_specs(layer):
    lead = (lambda *a: a[-1])

    def spec(shape):
        return pl.BlockSpec((None, None) + shape, lambda *a: (layer, lead(*a), 0, 0))

    return [spec((8, S5_ST)), spec((S5_CH, S5_ST)), spec((S5_CH, S5_ST)),
            spec((S5_ST, S5_CH)), spec((S5_ST, S5_CH)), spec((1, S5_CH))]


def _s5_prompt(u_perm, s5p, layer):
    zeros = jnp.zeros((BATCH, S5_NGB, 1, S5_ST), F32)
    state_spec = pl.BlockSpec((None, None, 1, S5_ST), lambda b, g: (b, g, 0, 0))
    out, hr, hi = pl.pallas_call(
        _s5_prompt_body,
        grid=(BATCH, S5_NGB),
        in_specs=[pl.BlockSpec((SEQ, S5_CH), lambda b, g: (b, g))] + _s5_param_specs(layer)
        + [state_spec, state_spec],
        out_specs=[pl.BlockSpec((SEQ, S5_CH), lambda b, g: (b, g)), state_spec, state_spec],
        out_shape=[jax.ShapeDtypeStruct((N_PROMPT, WIDTH), F32),
                   jax.ShapeDtypeStruct((BATCH, S5_NGB, 1, S5_ST), F32),
                   jax.ShapeDtypeStruct((BATCH, S5_NGB, 1, S5_ST), F32)],
        scratch_shapes=[pltpu.VMEM((SEQ, S5_ST), F32), pltpu.VMEM((SEQ, S5_ST), F32)],
        compiler_params=_params("parallel", "parallel"),
        name="s5_prompt",
    )(u_perm, *s5p, zeros, zeros)
    return out, hr, hi


def _s5_sample_body(u_ref, par_ref, bre_ref, bim_ref, cre_ref, cim_ref, d_ref, h0r_ref, h0i_ref,
                    o_ref, hr_ref, hi_ref):
    abr, abi, cr, ci = _s5_discretize(par_ref[...])
    bre, bim = bre_ref[...], bim_ref[...]
    bbr = (cr * bre - ci * bim).astype(BF16)
    bbi = (cr * bim + ci * bre).astype(BF16)
    u = u_ref[...]
    ub = u.astype(BF16)
    h0r, h0i = h0r_ref[...], h0i_ref[...]
    xr = _dot(ub, bbr) + abr * h0r - abi * h0i
    xi = _dot(ub, bbi) + abr * h0i + abi * h0r
    hr_ref[...] = xr
    hi_ref[...] = xi
    y = (_dot(xr.astype(BF16), cre_ref[...].astype(BF16))
         - _dot(xi.astype(BF16), cim_ref[...].astype(BF16)) + d_ref[...] * u)
    o_ref[...] = _gelu(y)


def _s5_sample(proj, s5p, h0r, h0i, layer):
    state_spec = pl.BlockSpec((DEC_BATCH, S5_ST), lambda g: (0, g))
    return pl.pallas_call(
        _s5_sample_body,
        grid=(S5_NGB,),
        in_specs=[pl.BlockSpec((DEC_BATCH, S5_CH), lambda g: (SAMPLE_ROW_BLOCK, C_U // S5_CH + g))]
        + _s5_param_specs(layer) + [state_spec, state_spec],
        out_specs=[pl.BlockSpec((DEC_BATCH, S5_CH), lambda g: (0, g)), state_spec, state_spec],
        out_shape=[jax.ShapeDtypeStruct((DEC_BATCH, WIDTH), F32),
                   jax.ShapeDtypeStruct((DEC_BATCH, S5_NGB * S5_ST), F32),
                   jax.ShapeDtypeStruct((DEC_BATCH, S5_NGB * S5_ST), F32)],
        compiler_params=_params("parallel"),
        name="s5_sample",
    )(proj, *s5p, h0r, h0i)


GDN_TS = 512
GDN_SCALE = HEAD_DIM ** -0.5


def _l2norm_heads(xc, o_ref):
    for h in range(HEADS):
        cs = slice(h * HEAD_DIM, (h + 1) * HEAD_DIM)
        seg = xc[:, cs]
        o_ref[:, cs] = seg * lax.rsqrt(jnp.sum(seg * seg, axis=-1, keepdims=True) + EPS)


def _conv_prompt_body(x_ref, prev_ref, w_ref, o_ref, ext_scr):
    i = pl.program_id(1)
    part = pl.program_id(2)
    x = x_ref[...]
    ext_scr[0:8, :] = jnp.where(i == 0, 0.0, prev_ref[...])
    ext_scr[8:8 + GDN_TS, :] = x
    w = w_ref[...]
    acc = x * w[3:4]
    for tap in range(CONV_TAPS - 1):
        back = CONV_TAPS - 1 - tap
        acc = acc + ext_scr[8 - back:8 - back + GDN_TS, :] * w[tap:tap + 1]
    xc = acc * _sigmoid(acc)

    @pl.when(part < 2)
    def _():
        _l2norm_heads(xc, o_ref)

    @pl.when(part == 2)
    def _():
        o_ref[...] = xc


def _conv_prompt(proj, conv_w, layer):
    n_t = SEQ // GDN_TS
    col0 = C_QKV // WIDTH

    def prev_index(b, i, p):
        return (jnp.maximum(b * (SEQ // 8) + i * (GDN_TS // 8) - 1, 0), col0 + p)

    return pl.pallas_call(
        _conv_prompt_body,
        grid=(BATCH, n_t, 3),
        in_specs=[
            pl.BlockSpec((GDN_TS, WIDTH), lambda b, i, p: (b * n_t + i, col0 + p)),
            pl.BlockSpec((8, WIDTH), prev_index),
            pl.BlockSpec((None, 8, WIDTH), lambda b, i, p: (layer, 0, p)),
        ],
        out_specs=pl.BlockSpec((GDN_TS, WIDTH), lambda b, i, p: (b * n_t + i, p)),
        out_shape=jax.ShapeDtypeStruct((N_PROMPT, CONV_WIDTH), F32),
        scratch_shapes=[pltpu.VMEM((GDN_TS + 8, WIDTH), F32)],
        compiler_params=_params("parallel", "parallel", "parallel"),
        name="gdn_conv_prompt",
    )(proj, proj, conv_w)


def _conv_sample_body(x_ref, hist_ref, w_ref, o_ref):
    part = pl.program_id(0)
    w = w_ref[...]
    acc = x_ref[...] * w[3:4]
    for tap in range(CONV_TAPS - 1):
        acc = acc + hist_ref[tap] * w[tap:tap + 1]
    xc = acc * _sigmoid(acc)

    @pl.when(part < 2)
    def _():
        _l2norm_heads(xc, o_ref)

    @pl.when(part == 2)
    def _():
        o_ref[...] = xc


def _conv_sample(proj, hist, conv_w, layer):
    col0 = C_QKV // WIDTH
    return pl.pallas_call(
        _conv_sample_body,
        grid=(3,),
        in_specs=[
            pl.BlockSpec((DEC_BATCH, WIDTH), lambda p: (SAMPLE_ROW_BLOCK, col0 + p)),
            pl.BlockSpec((None, CONV_TAPS - 1, DEC_BATCH, WIDTH), lambda p: (layer, 0, 0, p)),
            pl.BlockSpec((None, 8, WIDTH), lambda p: (layer, 0, p)),
        ],
        out_specs=pl.BlockSpec((DEC_BATCH, WIDTH), lambda p: (0, p)),
        out_shape=jax.ShapeDtypeStruct((DEC_BATCH, CONV_WIDTH), F32),
        compiler_params=_params("parallel"),
        name="gdn_conv_sample",
    )(proj, hist, conv_w)


GDN_CHUNK = 128
GDN_ROW_SLOTS = 16


def _gdn_gates_body(ba_ref, par_ref, bg_ref, gcrow_ref, *, rows, chunk):
    ba = ba_ref[...]
    par = par_ref[...]
    g = -jnp.exp(par[0:1]) * _softplus(ba + par[1:2])
    if chunk > 1:
        r = lax.broadcasted_iota(jnp.int32, (rows, rows), 0)
        c = lax.broadcasted_iota(jnp.int32, (rows, rows), 1)
        tri = jnp.where((r >= c) & ((r & -chunk) == (c & -chunk)), 1.0, 0.0)
        g = _dot(tri, g, precision=HIGHEST)
        g_t = g.T
        for cc in range(rows // chunk):
            gcrow_ref[cc * GDN_ROW_SLOTS:(cc + 1) * GDN_ROW_SLOTS, :] = (
                g_t[0:GDN_ROW_SLOTS, cc * chunk:(cc + 1) * chunk])
    lane = lax.broadcasted_iota(jnp.int32, ba.shape, 1)
    bg_ref[...] = jnp.where(lane < HEADS, _sigmoid(ba), g)


def _gdn_gates(proj, gate_par, layer, rows, n_blocks, first_block, chunk):
    out_specs = [pl.BlockSpec((rows, 128), lambda i: (i, 0))]
    out_shape = [jax.ShapeDtypeStruct((rows * n_blocks, 128), F32)]
    if chunk > 1:
        slots = rows // chunk * GDN_ROW_SLOTS
        out_specs.append(pl.BlockSpec((slots, chunk), lambda i: (i, 0)))
        out_shape.append(jax.ShapeDtypeStruct((slots * n_blocks, chunk), F32))
        body = functools.partial(_gdn_gates_body, rows=rows, chunk=chunk)
    else:
        body = functools.partial(_gdn_gates_body, gcrow_ref=None, rows=rows, chunk=chunk)
    return pl.pallas_call(
        body,
        grid=(n_blocks,),
        in_specs=[pl.BlockSpec((rows, 128), lambda i: (first_block + i, C_BA // 128)),
                  pl.BlockSpec((None, 8, 128), lambda i: (layer, 0, 0))],
        out_specs=out_specs,
        out_shape=out_shape,
        compiler_params=_params("parallel"),
        name="gdn_gates",
    )(proj, gate_par)


GDN_BASE = 8
GDN_GROUP = 2


def _gdn_chunk_body(q_ref, k_ref, v_ref, bg_ref, gcrow_ref, z_ref, s0_ref, gn_ref, o_ref, sfin_ref,
                    s_scr, mask_scr, u_scr, w_scr, qk_scr, qd_scr, kd_scr, *, n_tiles, tile_rows):
    ti = pl.program_id(1)
    n_chunks = tile_rows // GDN_CHUNK
    lanes = [slice(h * HEAD_DIM, (h + 1) * HEAD_DIM) for h in range(HEADS)]

    @pl.when(ti == 0)
    def _():
        s_scr[...] = s0_ref[...]

    row = lax.broadcasted_iota(jnp.int32, (GDN_CHUNK, GDN_CHUNK), 0)
    col = lax.broadcasted_iota(jnp.int32, (GDN_CHUNK, GDN_CHUNK), 1)

    def same_block(size):
        return (row & -size) == (col & -size)

    strict = row > col
    eye = jnp.where(row == col, 1.0, 0.0)
    mask_scr[0] = jnp.where(row >= col, 1.0, 0.0)
    mask_scr[1] = jnp.where(strict & same_block(GDN_BASE), 1.0, 0.0)
    merge_sizes = []
    size = GDN_BASE
    while size < GDN_CHUNK:
        mask_scr[2 + len(merge_sizes)] = jnp.where(strict & same_block(2 * size) & ~same_block(size), 1.0, 0.0)
        merge_sizes.append(size)
        size *= 2
    gain = gn_ref[...]

    tile = (GDN_CHUNK, HEAD_DIM)

    def prepare(pair, carry):
        units = [(dc, h) for dc in range(GDN_GROUP) for h in range(HEADS)]
        chains = range(len(units))
        rows, bg, gc_by_head = [], [], []
        for dc in range(GDN_GROUP):
            c = pair * GDN_GROUP + dc
            rows.append(pl.ds(pl.multiple_of(c * GDN_CHUNK, GDN_CHUNK), GDN_CHUNK))
            bg.append(bg_ref[rows[dc], :])
            gc_by_head.append(gcrow_ref[pl.ds(pl.multiple_of(c * GDN_ROW_SLOTS + HEADS, HEADS), HEADS), :])
        beta_of = lambda dc, h: jnp.broadcast_to(bg[dc][:, h:h + 1], tile)
        gc_of = lambda dc, h: jnp.broadcast_to(bg[dc][:, HEADS + h:HEADS + h + 1], tile)
        m = []
        for dc, h in units:
            cs = lanes[h]
            k = k_ref[rows[dc], cs]
            gc_lanes = gc_by_head[dc][h:h + 1, :]
            decay = mask_scr[0] * jnp.exp(jnp.minimum(gc_of(dc, h) - gc_lanes, 0.0))
            k16 = k.astype(BF16)
            m.append(_dot_nt((k * beta_of(dc, h)).astype(BF16), k16) * decay)
            qk_scr[rows[dc], cs] = (
                _dot_nt((q_ref[rows[dc], cs] * GDN_SCALE).astype(BF16), k16) * decay).astype(BF16)
        base = [m[i] * mask_scr[1] for i in chains]
        b16 = [base[i].astype(BF16) for i in chains]
        power = [_dot(b16[i], b16[i]).astype(BF16) for i in chains]
        inv = [eye - base[i] for i in chains]
        inv = [inv[i] + _dot(inv[i].astype(BF16), power[i]) for i in chains]
        for _ in range(int(math.log2(GDN_BASE)) - 2):
            power = [_dot(power[i], power[i]).astype(BF16) for i in chains]
            inv = [inv[i] + _dot(inv[i].astype(BF16), power[i]) for i in chains]
        for n in range(len(merge_sizes)):
            i16 = [inv[i].astype(BF16) for i in chains]
            half = [_dot(i16[i], (m[i] * mask_scr[2 + n]).astype(BF16)).astype(BF16) for i in chains]
            inv = [inv[i] - _dot(half[i], i16[i]) for i in chains]
        for i, (dc, h) in enumerate(units):
            cs = lanes[h]
            k = k_ref[rows[dc], cs]
            beta = beta_of(dc, h)
            gc = gc_of(dc, h)
            egc = jnp.exp(gc)
            rhs = jnp.concatenate([v_ref[rows[dc], cs] * beta, k * beta * egc], axis=-1)
            sol = rhs + _dot((inv[i] - eye).astype(BF16), rhs.astype(BF16))
            u_scr[rows[dc], cs] = sol[:, :HEAD_DIM]
            w_scr[rows[dc], cs] = sol[:, HEAD_DIM:].astype(BF16)
            gc_last = jnp.broadcast_to(gc[GDN_CHUNK - 1:GDN_CHUNK, :], tile)
            qd_scr[rows[dc], cs] = (q_ref[rows[dc], cs] * GDN_SCALE * egc).astype(BF16)
            kd_scr[rows[dc], cs] = (k * jnp.exp(gc_last - gc)).astype(BF16)
        return carry

    lax.fori_loop(0, n_chunks // GDN_GROUP, prepare, 0)

    def advance(c, carry):
        rows = pl.ds(pl.multiple_of(c * GDN_CHUNK, GDN_CHUNK), GDN_CHUNK)
        tail = pl.ds(pl.multiple_of(c * GDN_CHUNK + GDN_CHUNK - 8, 8), 8)
        heads = range(HEADS)
        gc_last = bg_ref[tail, :][7:8, :]
        state = [s_scr[h] for h in heads]
        s16 = [state[h].astype(BF16) for h in heads]
        vn16 = [(u_scr[rows, lanes[h]] - _dot(w_scr[rows, lanes[h]], s16[h])).astype(BF16) for h in heads]
        for h in heads:
            chunk_decay = jnp.exp(jnp.broadcast_to(gc_last[:, HEADS + h:HEADS + h + 1], (1, HEAD_DIM)))
            s_scr[h] = state[h] * chunk_decay + lax.dot_general(
                kd_scr[rows, lanes[h]], vn16[h], (((0,), (0,)), ((), ())), preferred_element_type=F32)
        for h in heads:
            cs = lanes[h]
            o = _dot(qd_scr[rows, cs], s16[h]) + _dot(qk_scr[rows, cs], vn16[h])
            zg = z_ref[rows, cs]
            o_ref[rows, cs] = (_rms(o, gain) * (zg * _sigmoid(zg))).astype(BF16)
        return carry

    lax.fori_loop(0, n_chunks, advance, 0)

    @pl.when(ti == n_tiles - 1)
    def _():
        sfin_ref[...] = s_scr[...]


def _gdn_chunks(qkv, bg, gcrow, z_src, z_col_block, s0, out_norm, layer, n_seq, seq_rows):
    tile_rows = min(seq_rows, GDN_TS)
    n_tiles = seq_rows // tile_rows
    n_masks = 2 + int(math.log2(GDN_CHUNK // GDN_BASE))

    def act(col_block):
        return pl.BlockSpec((tile_rows, WIDTH), lambda b, t: (b * n_tiles + t, col_block))

    slots = tile_rows // GDN_CHUNK * GDN_ROW_SLOTS
    state_spec = pl.BlockSpec((None, HEADS, HEAD_DIM, HEAD_DIM), lambda b, t: (b, 0, 0, 0))
    return pl.pallas_call(
        functools.partial(_gdn_chunk_body, n_tiles=n_tiles, tile_rows=tile_rows),
        grid=(n_seq, n_tiles),
        in_specs=[act(0), act(1), act(2),
                  pl.BlockSpec((tile_rows, 128), lambda b, t: (b * n_tiles + t, 0)),
                  pl.BlockSpec((slots, GDN_CHUNK), lambda b, t: (b * n_tiles + t, 0)),
                  act(z_col_block), state_spec,
                  pl.BlockSpec((None, 1, HEAD_DIM), lambda b, t: (layer, 0, 0))],
        out_specs=[act(0), state_spec],
        out_shape=[jax.ShapeDtypeStruct((n_seq * seq_rows, WIDTH), BF16),
                   jax.ShapeDtypeStruct((n_seq, HEADS, HEAD_DIM, HEAD_DIM), F32)],
        scratch_shapes=[pltpu.VMEM((HEADS, HEAD_DIM, HEAD_DIM), F32),
                        pltpu.VMEM((n_masks, GDN_CHUNK, GDN_CHUNK), F32),
                        pltpu.VMEM((tile_rows, WIDTH), F32),
                        pltpu.VMEM((tile_rows, WIDTH), BF16),
                        pltpu.VMEM((tile_rows, WIDTH), BF16),
                        pltpu.VMEM((tile_rows, WIDTH), BF16),
                        pltpu.VMEM((tile_rows, WIDTH), BF16)],
        compiler_params=_params("parallel", "arbitrary"),
        name="gdn_chunks",
    )(qkv, qkv, qkv, bg, gcrow, z_src, s0, out_norm)


GDN_STEP_COLS = CONV_WIDTH + 128 + WIDTH


def _gdn_step_body(x_ref, s0_ref, gn_ref, o_ref, s_ref):
    heads = range(HEADS)
    col = lambda part, h: slice(part * WIDTH + h * HEAD_DIM, part * WIDTH + (h + 1) * HEAD_DIM)
    gain = gn_ref[...]
    bg = x_ref[:, CONV_WIDTH:CONV_WIDTH + 128]
    z0 = CONV_WIDTH + 128
    k = [x_ref[:, col(1, h)] for h in heads]
    q = [x_ref[:, col(0, h)] * GDN_SCALE for h in heads]
    k16 = [k[h].astype(BF16) for h in heads]
    state = [s0_ref[h] for h in heads]
    s16 = [state[h].astype(BF16) for h in heads]
    k_s = [_dot(k16[h], s16[h]) for h in heads]
    q_s = [_dot(q[h].astype(BF16), s16[h]) for h in heads]
    decay = [jnp.exp(jnp.broadcast_to(bg[:, HEADS + h:HEADS + h + 1], (8, HEAD_DIM))) for h in heads]
    beta = [jnp.broadcast_to(bg[:, h:h + 1], (8, HEAD_DIM)) for h in heads]
    v_new = [beta[h] * (x_ref[:, col(2, h)] - decay[h] * k_s[h]) for h in heads]
    for h in heads:
        s_ref[h] = state[h] * decay[h][0:1, :] + lax.dot_general(
            k16[h], v_new[h].astype(BF16), (((0,), (0,)), ((), ())), preferred_element_type=F32)
    for h in heads:
        qk = jnp.sum(q[h].astype(BF16).astype(F32) * k16[h].astype(F32), axis=-1, keepdims=True)
        o = decay[h] * q_s[h] + qk * v_new[h].astype(BF16).astype(F32)
        zg = x_ref[:, z0 + h * HEAD_DIM:z0 + (h + 1) * HEAD_DIM]
        o_ref[:, h * HEAD_DIM:(h + 1) * HEAD_DIM] =(_rms(o, gain) * (zg * _sigmoid(zg))).astype(BF16)


def _gdn_step(x8, state, out_norm, layer):
    return pl.pallas_call(
        _gdn_step_body,
        grid=(DEC_BATCH,),
        in_specs=[pl.BlockSpec((None, 8, GDN_STEP_COLS), lambda b: (b, 0, 0)),
                  pl.BlockSpec((None, None, HEADS, HEAD_DIM, HEAD_DIM), lambda b: (layer, b, 0, 0, 0)),
                  pl.BlockSpec((None, 1, HEAD_DIM), lambda b: (layer, 0, 0))],
        out_specs=[pl.BlockSpec((None, 8, WIDTH), lambda b: (b, 0, 0)),
                   pl.BlockSpec((None, HEADS, HEAD_DIM, HEAD_DIM), lambda b: (b, 0, 0, 0))],
        out_shape=[jax.ShapeDtypeStruct((DEC_BATCH, 8, WIDTH), BF16),
                   jax.ShapeDtypeStruct((DEC_BATCH, HEADS, HEAD_DIM, HEAD_DIM), F32)],
        compiler_params=_params("parallel"),
        name="gdn_step",
    )(x8, state, out_norm)


def _pack_body(a_ref, b_ref, o_ref):
    n = pl.program_id(1)
    first_gate = C_GATE // PACK_TN
    last = PACK_TILES - 1
    shift = 2 * HEADS
    a = a_ref[...]
    shifted = jnp.concatenate([a[shift:, :], b_ref[...]], axis=0)
    row = lax.broadcasted_iota(jnp.int32, a.shape, 0)
    logits = jnp.where(row < shift, a, 0.0)
    tile = jnp.where(n < first_gate, a, jnp.where(n == last, logits, shifted))
    o_ref[...] = tile.T.astype(BF16)


def _pack_w_in(w_in):
    first_gate = C_GATE // PACK_TN
    last = PACK_TILES - 1
    shift = 2 * HEADS
    w_t = jnp.swapaxes(w_in, 1, 2)

    def a_index(d, n):
        return (d, jnp.where(n == last, first_gate, n), 0)

    def b_index(d, n):
        return (d, jnp.where((n >= first_gate) & (n < last), (n + 1) * (PACK_TN // shift), 0), 0)

    return pl.pallas_call(
        _pack_body,
        grid=(DEPTH, PACK_TILES),
        in_specs=[pl.BlockSpec((None, PACK_TN, D_MODEL), a_index),
                  pl.BlockSpec((None, shift, D_MODEL), b_index)],
        out_specs=pl.BlockSpec((None, D_MODEL, PACK_TN), lambda d, n: (d, 0, n)),
        out_shape=jax.ShapeDtypeStruct((DEPTH, D_MODEL, PROJ_COLS), BF16),
        compiler_params=_params("parallel", "arbitrary"),
        name="pack_w_in",
    )(w_t, w_t)


def _s5_params(a_re, a_im, log_dt, b_re, b_im, c_re, c_im, d):
    eye = jnp.eye(S5_GB, dtype=F32)

    def state_rows(x):
        return x.reshape(DEPTH, S5_NGB, 1, S5_ST)

    par = jnp.concatenate(
        [state_rows(a_re), state_rows(a_im),
         state_rows(jnp.repeat(log_dt, SSM_STATE, axis=-1)),
         jnp.zeros((DEPTH, S5_NGB, 5, S5_ST), F32)], axis=2)

    def b_blockdiag(b):
        b = b.reshape(DEPTH, S5_NGB, S5_GB, SSM_STATE, SSM_GROUP)
        return jnp.einsum('dbgpc,gh->dbgchp', b, eye).reshape(DEPTH, S5_NGB, S5_CH, S5_ST)

    def c_blockdiag(c):
        c = c.reshape(DEPTH, S5_NGB, S5_GB, SSM_GROUP, SSM_STATE)
        return jnp.einsum('dbgcp,gh->dbgphc', c, eye).reshape(DEPTH, S5_NGB, S5_ST, S5_CH)

    return (par, b_blockdiag(b_re), b_blockdiag(b_im), c_blockdiag(c_re), c_blockdiag(c_im),
            d.reshape(DEPTH, S5_NGB, 1, S5_CH))


def _segment_major(x):
    w = x.shape[-1]
    return x.reshape(BATCH, S5_LANES, S5_SEG, w).transpose(0, 2, 1, 3).reshape(N_PROMPT, w)


def _token_major(x):
    w = x.shape[-1]
    return x.reshape(BATCH, S5_SEG, S5_LANES, w).transpose(0, 2, 1, 3).reshape(N_PROMPT, w)


def _pad_rows(x):
    return jnp.concatenate([x, jnp.zeros((ROWS - x.shape[0], x.shape[1]), x.dtype)], axis=0)


def kernel(x_prompt, x_sample, cache_k, cache_v, page_table, state_ssm, state_conv, state_gdn, ffn1_norm, ffn1_w_gate, ffn1_w_up, ffn1_w_down, mix_norm, w_in, sb_bias, ssm_a_re, ssm_a_im, ssm_log_dt, ssm_b_re, ssm_b_im, ssm_c_re, ssm_c_im, ssm_d, ssm_w_glu, gdn_conv_w, gdn_a_log, gdn_dt_bias, gdn_out_norm, w_branch, w_out, ffn2_norm, ffn2_w_gate, ffn2_w_up, ffn2_w_down, final_norm):
    bf = lambda w: w.astype(BF16)
    gain3 = lambda g: g.reshape(DEPTH, 1, D_MODEL)
    w_in_p = _pack_w_in(w_in)
    ffn1 = (gain3(ffn1_norm), ffn1_w_gate, ffn1_w_up, ffn1_w_down)
    ffn2 = (gain3(ffn2_norm), ffn2_w_gate, ffn2_w_up, ffn2_w_down)
    mix_gain = gain3(mix_norm)
    w_glu, w_br, w_o = bf(ssm_w_glu), bf(w_branch).reshape(DEPTH, 3, WIDTH, D_MODEL), bf(w_out)
    s5p = _s5_params(ssm_a_re, ssm_a_im, ssm_log_dt, ssm_b_re, ssm_b_im, ssm_c_re, ssm_c_im, ssm_d)
    bias_flat = sb_bias.reshape(DEPTH * HEADS)
    bias_lanes = jnp.tile(sb_bias, (1, PAGE)).reshape(DEPTH, 1, DEC_LANES)
    page_flat = page_table.reshape(DEC_BATCH * N_PAGES)
    conv_w8 = jnp.pad(gdn_conv_w, ((0, 0), (0, 8 - CONV_TAPS), (0, 0)))
    conv_hist = state_conv.transpose(0, 2, 1, 3)
    lane_pad = jnp.zeros((DEPTH, 128 - 2 * HEADS), F32)
    gate_par = jnp.stack([jnp.concatenate([jnp.zeros((DEPTH, HEADS), F32), gdn_a_log, lane_pad], axis=1),
                          jnp.concatenate([jnp.zeros((DEPTH, HEADS), F32), gdn_dt_bias, lane_pad], axis=1)]
                         + [jnp.zeros((DEPTH, 128), F32)] * 6, axis=1)
    out_norm = gdn_out_norm.reshape(DEPTH, 1, HEAD_DIM)
    ssm_h0r = state_ssm[..., 0].reshape(DEPTH, DEC_BATCH, WIDTH * SSM_STATE // SSM_GROUP)
    ssm_h0i = state_ssm[..., 1].reshape(DEPTH, DEC_BATCH, WIDTH * SSM_STATE // SSM_GROUP)

    x = _pad_rows(jnp.concatenate([x_prompt.reshape(N_PROMPT, D_MODEL),
                                   x_sample.reshape(DEC_BATCH, D_MODEL)], axis=0))
    sample = slice(N_PROMPT, N_PROMPT + DEC_BATCH)
    ks, vs, sp, ss, cp, cs_, gp, gs = ([] for _ in range(8))
    k_all = jnp.zeros((DEPTH, N_PROMPT, WIDTH), F32)
    v_all = jnp.zeros((DEPTH, N_PROMPT, WIDTH), F32)
    for l in range(DEPTH):
        x = _ffn(x, *ffn1, l)
        proj = _in_proj(x, mix_gain, w_in_p, l)

        oa_p, k_all, v_all = _sb_prompt(proj, bias_flat, l, k_all, v_all)
        q_s = proj[sample, C_Q:C_Q + WIDTH].reshape(DEC_BATCH, HEADS, HEAD_DIM)
        oa_s = _sb_decode(q_s, cache_k, cache_v, page_flat, bias_lanes, l)
        oa = _pad_rows(jnp.concatenate([oa_p, oa_s.reshape(DEC_BATCH, WIDTH).astype(BF16)], axis=0))

        gb_p, hr_p, hi_p = _s5_prompt(_segment_major(proj[:N_PROMPT, C_U:C_U + WIDTH]), s5p, l)
        gb_s, hr_s, hi_s = _s5_sample(proj, s5p, ssm_h0r[l], ssm_h0i[l], l)
        ob = _glu(_pad_rows(jnp.concatenate([_token_major(gb_p), gb_s], axis=0)), w_glu, l)

        qkv_p = _conv_prompt(proj, conv_w8, l)
        bg_p, gcrow_p = _gdn_gates(proj, gate_par, l, GDN_TS, N_PROMPT // GDN_TS, 0, GDN_CHUNK)
        oc_p, gdn_p = _gdn_chunks(qkv_p, bg_p, gcrow_p, proj, C_Z // WIDTH,
                                  jnp.zeros((BATCH, HEADS, HEAD_DIM, HEAD_DIM), F32), out_norm, l, BATCH, SEQ)
        qkv_s = _conv_sample(proj, conv_hist, conv_w8, l)
        (bg_s,) = _gdn_gates(proj, gate_par, l, DEC_BATCH, 1, SAMPLE_ROW_BLOCK, 1)
        step_in = jnp.concatenate([qkv_s, bg_s, proj[sample, C_Z:C_Z + WIDTH]], axis=1)
        oc_s, gdn_s = _gdn_step(jnp.pad(step_in[:, None, :], ((0, 0), (0, 7), (0, 0))), state_gdn, out_norm, l)
        oc = _pad_rows(jnp.concatenate([oc_p, oc_s[:, 0, :]], axis=0))

        x = _merge(x, oa, ob, oc, proj, w_br, w_o, l)
        x = _ffn(x, *ffn2, l)

        ks.append(proj[sample, C_K:C_K + WIDTH].reshape(DEC_BATCH, 1, HEADS, HEAD_DIM))
        vs.append(proj[sample, C_V:C_V + WIDTH].reshape(DEC_BATCH, 1, HEADS, HEAD_DIM))
        state = lambda r, i, n: jnp.stack([r.reshape(n, WIDTH // SSM_GROUP, SSM_STATE),
                                           i.reshape(n, WIDTH // SSM_GROUP, SSM_STATE)], axis=-1)
        sp.append(state(hr_p, hi_p, BATCH))
        ss.append(state(hr_s, hi_s, DEC_BATCH))
        qkv_cols = slice(C_QKV, C_QKV + CONV_WIDTH)
        cp.append(jnp.stack([proj[(b + 1) * SEQ - (CONV_TAPS - 1):(b + 1) * SEQ, qkv_cols] for b in range(BATCH)]))
        cs_.append(jnp.concatenate([state_conv[l][:, 1:], proj[sample, qkv_cols][:, None, :]], axis=1))
        gp.append(gdn_p)
        gs.append(gdn_s)

    gain = final_norm.reshape(1, D_MODEL)
    y_p = _final_norm(x, gain, GDN_TS, N_PROMPT // GDN_TS, 0)
    y_s = _final_norm(x, gain, DEC_BATCH, 1, SAMPLE_ROW_BLOCK)
    kv_shape = (DEPTH, BATCH, SEQ, HEADS, HEAD_DIM)
    return (y_p.reshape(BATCH, SEQ, D_MODEL), y_s.reshape(DEC_BATCH, 1, D_MODEL),
            k_all.reshape(kv_shape), v_all.reshape(kv_shape), jnp.stack(ks), jnp.stack(vs),
            jnp.stack(sp), jnp.stack(ss), jnp.stack(cp), jnp.stack(cs_),
            jnp.stack(gp), jnp.stack(gs))
```

```python
import functools
import math

import jax
import jax.numpy as jnp
from jax import lax
from jax.experimental import pallas as pl
from jax.experimental.pallas import tpu as pltpu

F32 = jnp.float32
BF16 = jnp.bfloat16
HIGHEST = lax.Precision.HIGHEST

D_MODEL = 2048
BATCH = 4
SEQ = 2048
DEPTH = 2
DEC_BATCH = 32
PAGE = 128
N_PAGES = 64
HEADS = 8
HEAD_DIM = 128
WIDTH = HEADS * HEAD_DIM
SSM_GROUP = 16
SSM_STATE = 64
CONV_TAPS = 4
CONV_WIDTH = 3 * WIDTH
FFN_DIM = 5632
EPS = 1e-6

N_PROMPT = BATCH * SEQ
ROWS = 8320
TM = ROWS // 10
TM_GLU = ROWS // 20
SAMPLE_ROW_BLOCK = N_PROMPT // DEC_BATCH

C_Q, C_K, C_V, C_U, C_QKV, C_Z, C_GATE, C_BA = 0, 1024, 2048, 3072, 4096, 7168, 8192, 14336
PROJ_COLS = 14592
TN_PROJ = 768
PACK_TN = 512
PACK_TILES = -(-PROJ_COLS // PACK_TN)
TF = 512
TM_BIG = ROWS // 5
TM_MERGE = ROWS // 26

VMEM_LIMIT = 56 * 1024 * 1024


def _params(*sem):
    return pltpu.CompilerParams(dimension_semantics=sem, vmem_limit_bytes=VMEM_LIMIT)


def _rms(x, gain):
    ms = jnp.mean(x * x, axis=-1, keepdims=True)
    return x * lax.rsqrt(ms + EPS) * gain


def _softplus(x):
    return jnp.maximum(x, 0.0) + jnp.log(1.0 + jnp.exp(-jnp.abs(x)))


def _sigmoid(x):
    return 1.0 / (1.0 + jnp.exp(-x))


def _dot(a, b, **kw):
    return jnp.dot(a, b, preferred_element_type=F32, **kw)


def _dot_nt(a, b, **kw):
    return lax.dot_general(a, b, (((1,), (1,)), ((), ())), preferred_element_type=F32, **kw)


def _ffn_body(x_ref, g_ref, wg_ref, wu_ref, wd_ref, o_ref, h_scr):
    @pl.when(pl.program_id(1) == 0)
    def _():
        x = x_ref[...]
        h_scr[...] = _rms(x, g_ref[...]).astype(BF16)
        o_ref[...] = x

    h = h_scr[...]
    a = _dot(h, wg_ref[...])
    b = _dot(h, wu_ref[...])
    act = (a * _sigmoid(a)) * b * 0.5
    o_ref[...] += _dot(act.astype(BF16), wd_ref[...])


def _ffn(x, gain, wg, wu, wd, layer):
    return pl.pallas_call(
        _ffn_body,
        grid=(ROWS // TM, FFN_DIM // TF),
        in_specs=[
            pl.BlockSpec((TM, D_MODEL), lambda i, f: (i, 0)),
            pl.BlockSpec((None, 1, D_MODEL), lambda i, f: (layer, 0, 0)),
            pl.BlockSpec((None, D_MODEL, TF), lambda i, f: (layer, 0, f)),
            pl.BlockSpec((None, D_MODEL, TF), lambda i, f: (layer, 0, f)),
            pl.BlockSpec((None, TF, D_MODEL), lambda i, f: (layer, f, 0)),
        ],
        out_specs=pl.BlockSpec((TM, D_MODEL), lambda i, f: (i, 0)),
        out_shape=jax.ShapeDtypeStruct((ROWS, D_MODEL), F32),
        scratch_shapes=[pltpu.VMEM((TM, D_MODEL), BF16)],
        compiler_params=_params("parallel", "arbitrary"),
        name="ffn",
    )(x, gain, wg, wu, wd)


def _proj_body(x_ref, g_ref, w_ref, o_ref, h_scr):
    @pl.when(pl.program_id(1) == 0)
    def _():
        h_scr[...] = _rms(x_ref[...], g_ref[...]).astype(BF16)

    o_ref[...] = _dot(h_scr[...], w_ref[...])


def _in_proj(x, gain, w, layer):
    return pl.pallas_call(
        _proj_body,
        grid=(ROWS // TM_BIG, PROJ_COLS // TN_PROJ),
        in_specs=[
            pl.BlockSpec((TM_BIG, D_MODEL), lambda i, n: (i, 0), pipeline_mode=pl.Buffered(1)),
            pl.BlockSpec((None, 1, D_MODEL), lambda i, n: (layer, 0, 0)),
            pl.BlockSpec((None, D_MODEL, TN_PROJ), lambda i, n: (layer, 0, n)),
        ],
        out_specs=pl.BlockSpec((TM_BIG, TN_PROJ), lambda i, n: (i, n)),
        out_shape=jax.ShapeDtypeStruct((ROWS, PROJ_COLS), F32),
        scratch_shapes=[pltpu.VMEM((TM_BIG, D_MODEL), BF16)],
        compiler_params=_params("parallel", "arbitrary"),
        name="in_proj",
    )(x, gain, w)


def _glu_body(g_ref, w_ref, o_ref):
    g = g_ref[...]
    o_ref[...] = (g * _sigmoid(_dot(g.astype(BF16), w_ref[...]))).astype(BF16)


def _glu(g, w, layer):
    return pl.pallas_call(
        _glu_body,
        grid=(ROWS // TM_GLU,),
        in_specs=[
            pl.BlockSpec((TM_GLU, WIDTH), lambda i: (i, 0)),
            pl.BlockSpec((None, WIDTH, WIDTH), lambda i: (layer, 0, 0)),
        ],
        out_specs=pl.BlockSpec((TM_GLU, WIDTH), lambda i: (i, 0)),
        out_shape=jax.ShapeDtypeStruct((ROWS, WIDTH), BF16),
        compiler_params=_params("parallel"),
        name="glu",
    )(g, w)


def _merge_body(x_ref, oa_ref, ob_ref, oc_ref, ga_ref, gb_ref, gc_ref, wa_ref, wb_ref, wc_ref, wo_ref, o_ref):
    m = (_sigmoid(ga_ref[...]) * _dot(oa_ref[...], wa_ref[...])
         + _sigmoid(gb_ref[...]) * _dot(ob_ref[...], wb_ref[...])
         + _sigmoid(gc_ref[...]) * _dot(oc_ref[...], wc_ref[...]))
    o_ref[...] = x_ref[...] + _dot(m.astype(BF16), wo_ref[...])


def _merge(x, oa, ob, oc, proj, w_branch, w_out, layer):
    gate_blk = C_GATE // D_MODEL
    o_spec = pl.BlockSpec((TM_MERGE, WIDTH), lambda i: (i, 0))

    def gate_spec(which):
        return pl.BlockSpec((TM_MERGE, D_MODEL), lambda i: (i, gate_blk + which))

    def w_spec(which):
        return pl.BlockSpec((None, None, WIDTH, D_MODEL), lambda i: (layer, which, 0, 0),
                            pipeline_mode=pl.Buffered(1))

    return pl.pallas_call(
        _merge_body,
        grid=(ROWS // TM_MERGE,),
        in_specs=[
            pl.BlockSpec((TM_MERGE, D_MODEL), lambda i: (i, 0)),
            o_spec, o_spec, o_spec,
            gate_spec(0), gate_spec(1), gate_spec(2),
            w_spec(0), w_spec(1), w_spec(2),
            pl.BlockSpec((None, D_MODEL, D_MODEL), lambda i: (layer, 0, 0), pipeline_mode=pl.Buffered(1)),
        ],
        out_specs=pl.BlockSpec((TM_MERGE, D_MODEL), lambda i: (i, 0)),
        out_shape=jax.ShapeDtypeStruct((ROWS, D_MODEL), F32),
        compiler_params=_params("parallel"),
        name="merge",
    )(x, oa, ob, oc, proj, proj, proj, w_branch, w_branch, w_branch, w_out)


def _final_norm_body(x_ref, g_ref, o_ref):
    o_ref[...] = _rms(x_ref[...], g_ref[...])


def _final_norm(x, gain, rows, n_blocks, first_block):
    return pl.pallas_call(
        _final_norm_body,
        grid=(n_blocks,),
        in_specs=[pl.BlockSpec((rows, D_MODEL), lambda i: (first_block + i, 0)),
                  pl.BlockSpec((1, D_MODEL), lambda i: (0, 0))],
        out_specs=pl.BlockSpec((rows, D_MODEL), lambda i: (i, 0)),
        out_shape=jax.ShapeDtypeStruct((rows * n_blocks, D_MODEL), F32),
        compiler_params=_params("parallel"),
        name="final_norm",
    )(x, gain)


SB_BQ = 256
SB_BK = 256
SB_HB = 4
SB_SCALE = HEAD_DIM ** -0.5


def _stick_terms(z):
    t = jnp.log(1.0 + jnp.exp(-jnp.abs(z)))
    return jnp.minimum(z, 0.0) - t, jnp.minimum(-z, 0.0) - t


LOG2E = 1.0 / math.log(2.0)


def _sb_prompt_body(bias_ref, q_ref, k_ref, v_ref, *rest, layer):
    o_ref, k_all_ref, v_all_ref, q_scr, acc_scr, run_scr = rest[2:]
    hg = pl.program_id(1)
    qi = pl.program_id(2)

    @pl.when(qi == 0)
    def _():
        k_all_ref[...] = k_ref[...]
        v_all_ref[...] = v_ref[...]

    heads = range(SB_HB)
    lanes = [slice(hh * HEAD_DIM, (hh + 1) * HEAD_DIM) for hh in heads]
    row = lax.broadcasted_iota(jnp.int32, (SB_BQ, SB_BK), 0)
    col = lax.broadcasted_iota(jnp.int32, (SB_BQ, SB_BK), 1)
    q_pos = qi * SB_BQ + row
    r2 = lax.broadcasted_iota(jnp.int32, (SB_BK, SB_BK), 0)
    c2 = lax.broadcasted_iota(jnp.int32, (SB_BK, SB_BK), 1)
    later_key = jnp.where(r2 > c2, 1.0, 0.0).astype(BF16)
    bias2 = [bias_ref[layer * HEADS + hg * SB_HB + hh] * LOG2E for hh in heads]

    for hh in heads:
        q_scr[hh] = (q_ref[:, lanes[hh]] * (SB_SCALE * LOG2E)).astype(BF16)
    acc_scr[...] = jnp.zeros_like(acc_scr)
    run_scr[...] = jnp.zeros_like(run_scr)

    def visit(blocks, masked):
        units = [(n, hh) for n in range(len(blocks)) for hh in heads]
        chains = range(len(units))
        k_rows = [pl.ds(pl.multiple_of(j * SB_BK, SB_BK), SB_BK) for j in blocks]
        mask = (blocks[0] * SB_BK + col) < q_pos
        z = [_dot_nt(q_scr[hh], k_ref[k_rows[n], lanes[hh]].astype(BF16)) + bias2[hh] for n, hh in units]
        t = [jnp.log(1.0 + jnp.exp2(-jnp.abs(z[i]))) * LOG2E for i in chains]
        log_hit = [jnp.minimum(z[i], 0.0) - t[i] for i in chains]
        log_fail = [jnp.minimum(-z[i], 0.0) - t[i] for i in chains]
        if masked:
            log_fail = [jnp.where(mask, log_fail[i], 0.0) for i in chains]
        suffix = [_dot(log_fail[i].astype(BF16), later_key) for i in chains]
        w = []
        run = [run_scr[hh] for hh in heads]
        for i, (n, hh) in enumerate(units):
            w_i = jnp.exp2(log_hit[i] + suffix[i] + run[hh])
            block_total = suffix[i][:, 0:1] + log_fail[i][:, 0:1]
            run[hh] = run[hh] + jnp.broadcast_to(block_total, (SB_BQ, SB_BK))
            w.append(jnp.where(mask, w_i, 0.0) if masked else w_i)
        for hh in heads:
            run_scr[hh] = run[hh]
        for hh in heads:
            acc = acc_scr[hh]
            for i, (n, h2) in enumerate(units):
                if h2 == hh:
                    acc = acc + _dot(w[i].astype(BF16), v_ref[k_rows[n], lanes[hh]].astype(BF16))
            acc_scr[hh] = acc

    visit([qi], True)

    def below_diagonal(it, carry):
        visit([qi - 1 - 2 * it, qi - 2 - 2 * it], False)
        return carry

    lax.fori_loop(0, qi // 2, below_diagonal, 0)

    @pl.when(qi % 2 == 1)
    def _():
        visit([0], False)
    for hh in heads:
        o_ref[:, lanes[hh]] = acc_scr[hh].astype(BF16)


def _sb_prompt(proj, bias_flat, layer, k_all, v_all):
    hw = SB_HB * HEAD_DIM
    n_hg = HEADS // SB_HB
    nq = SEQ // SB_BQ
    all_spec = pl.BlockSpec((None, SEQ, hw), lambda b, h, i, bias: (layer, b, h))
    all_shape = jax.ShapeDtypeStruct((DEPTH, N_PROMPT, WIDTH), F32)
    grid_spec = pltpu.PrefetchScalarGridSpec(
        num_scalar_prefetch=1,
        grid=(BATCH, n_hg, nq),
        in_specs=[
            pl.BlockSpec((SB_BQ, hw), lambda b, h, i, bias: (b * nq + i, C_Q // hw + h)),
            pl.BlockSpec((SEQ, hw), lambda b, h, i, bias: (b, C_K // hw + h)),
            pl.BlockSpec((SEQ, hw), lambda b, h, i, bias: (b, C_V // hw + h)),
            pl.BlockSpec(memory_space=pl.ANY),
            pl.BlockSpec(memory_space=pl.ANY),
        ],
        out_specs=[pl.BlockSpec((SB_BQ, hw), lambda b, h, i, bias: (b * nq + i, h)), all_spec, all_spec],
        scratch_shapes=[pltpu.VMEM((SB_HB, SB_BQ, HEAD_DIM), BF16),
                        pltpu.VMEM((SB_HB, SB_BQ, HEAD_DIM), F32),
                        pltpu.VMEM((SB_HB, SB_BQ, SB_BK), F32)],
    )
    return pl.pallas_call(
        functools.partial(_sb_prompt_body, layer=layer),
        grid_spec=grid_spec,
        out_shape=[jax.ShapeDtypeStruct((N_PROMPT, WIDTH), BF16), all_shape, all_shape],
        input_output_aliases={4: 1, 5: 2},
        compiler_params=_params("parallel", "parallel", "arbitrary"),
        name="sb_prompt",
    )(bias_flat, proj, proj, proj, k_all, v_all)


DEC_PAGES = 16
DEC_STEPS = N_PAGES // DEC_PAGES
DEC_LANES = PAGE * HEADS


def _lane_shift_up(x, s):
    n = x.shape[1]
    lane = lax.broadcasted_iota(jnp.int32, x.shape, 1)
    return jnp.where(lane < n - s, pltpu.roll(x, n - s, 1), 0.0)


def _sb_decode_body(pt_ref, q_ref, bias_ref, *refs):
    k_refs = refs[:DEC_PAGES]
    v_refs = refs[DEC_PAGES:2 * DEC_PAGES]
    o_ref, acc_scr, run_scr = refs[2 * DEC_PAGES:]
    g = pl.program_id(1)

    @pl.when(g == 0)
    def _():
        acc_scr[...] = jnp.zeros_like(acc_scr)
        run_scr[...] = jnp.zeros_like(run_scr)

    sub = lax.broadcasted_iota(jnp.int32, (HEADS, DEC_LANES), 0)
    lane = lax.broadcasted_iota(jnp.int32, (HEADS, DEC_LANES), 1)
    own_head = (lane & (HEADS - 1)) == sub
    q = q_ref[...].astype(BF16)
    rows = []
    for i in range(DEC_PAGES):
        k2 = k_refs[i][...].reshape(DEC_LANES, HEAD_DIM).astype(BF16)
        zt = _dot_nt(q, k2)
        rows.append(jnp.sum(jnp.where(own_head, zt, 0.0), axis=0, keepdims=True))
    z = jnp.concatenate(rows, axis=0) * SB_SCALE + bias_ref[...]
    log_hit, log_fail = _stick_terms(z)

    suffix = log_fail
    for s in (HEADS, 4 * HEADS):
        suffix = (suffix + _lane_shift_up(suffix, s)) + (_lane_shift_up(suffix, 2 * s) + _lane_shift_up(suffix, 3 * s))
    s = 16 * HEADS
    while s < DEC_LANES:
        suffix = suffix + jnp.concatenate([suffix[:, s:], jnp.zeros((DEC_PAGES, s), F32)], axis=1)
        s *= 2
    tile_lane = lax.broadcasted_iota(jnp.int32, (DEC_PAGES, 16 * HEADS), 1)
    tot = jnp.where(tile_lane < HEADS, suffix[:, :16 * HEADS], 0.0)
    for s in (HEADS, 4 * HEADS):
        tot = (tot + pltpu.roll(tot, s, 1)) + (pltpu.roll(tot, 2 * s, 1) + pltpu.roll(tot, 3 * s, 1))
    tot = jnp.concatenate([tot] * (DEC_LANES // (16 * HEADS)), axis=1)
    page_r = lax.broadcasted_iota(jnp.int32, (DEC_PAGES, DEC_PAGES), 0)
    page_c = lax.broadcasted_iota(jnp.int32, (DEC_PAGES, DEC_PAGES), 1)
    later = jnp.where(page_c > page_r, 1.0, 0.0)
    after = _dot(later, tot, precision=HIGHEST)
    run = run_scr[...]
    log_between = (suffix - log_fail) + after + run
    run_scr[...] = run + jnp.sum(tot, axis=0, keepdims=True)
    w = jnp.exp(log_hit + log_between)

    acc = acc_scr[...]
    for i in range(DEC_PAGES):
        wm = jnp.where(own_head, jnp.broadcast_to(w[i:i + 1, :], (HEADS, DEC_LANES)), 0.0).astype(BF16)
        v2 = v_refs[i][...].reshape(DEC_LANES, HEAD_DIM).astype(BF16)
        acc = acc + _dot(wm, v2)
    acc_scr[...] = acc

    @pl.when(g == DEC_STEPS - 1)
    def _():
        o_ref[...] = acc


def _sb_decode(q, cache_k, cache_v, page_table_flat, bias_lanes, layer):
    def page_spec(i):
        def index(b, g, pt):
            return (layer, pt[b * N_PAGES + (DEC_STEPS - 1 - g) * DEC_PAGES + i], 0, 0, 0)
        return pl.BlockSpec((None, None, PAGE, HEADS, HEAD_DIM), index)

    grid_spec = pltpu.PrefetchScalarGridSpec(
        num_scalar_prefetch=1,
        grid=(DEC_BATCH, DEC_STEPS),
        in_specs=[pl.BlockSpec((None, HEADS, HEAD_DIM), lambda b, g, pt: (b, 0, 0)),
                  pl.BlockSpec((None, 1, DEC_LANES), lambda b, g, pt: (layer, 0, 0))]
        + [page_spec(i) for i in range(DEC_PAGES)] * 2,
        out_specs=pl.BlockSpec((None, HEADS, HEAD_DIM), lambda b, g, pt: (b, 0, 0)),
        scratch_shapes=[pltpu.VMEM((HEADS, HEAD_DIM), F32), pltpu.VMEM((1, DEC_LANES), F32)],
    )
    return pl.pallas_call(
        _sb_decode_body,
        grid_spec=grid_spec,
        out_shape=jax.ShapeDtypeStruct((DEC_BATCH, HEADS, HEAD_DIM), F32),
        compiler_params=_params("parallel", "arbitrary"),
        name="sb_decode",
    )(page_table_flat, q, bias_lanes, *([cache_k] * DEC_PAGES), *([cache_v] * DEC_PAGES))


S5_GB = 8
S5_CH = S5_GB * SSM_GROUP
S5_ST = S5_GB * SSM_STATE
S5_NGB = WIDTH // S5_CH
S5_LANES = 8
S5_SEG = SEQ // S5_LANES


def _s5_discretize(par):
    ar, ai, ldt = par[0:1], par[1:2], par[2:3]
    dt = jnp.exp(ldt)
    mag = jnp.exp(dt * ar)
    abr = mag * jnp.cos(dt * ai)
    abi = mag * jnp.sin(dt * ai)
    den = ar * ar + ai * ai
    cr = ((abr - 1.0) * ar + abi * ai) / den
    ci = (abi * ar - (abr - 1.0) * ai) / den
    return abr, abi, cr, ci


def _gelu(y):
    return 0.5 * y * (1.0 + jnp.tanh(math.sqrt(2.0 / math.pi) * (y + 0.044715 * (y * y * y))))


def _s5_prompt_body(u_ref, par_ref, bre_ref, bim_ref, cre_ref, cim_ref, d_ref, h0r_ref, h0i_ref,
                    o_ref, hr_ref, hi_ref, xr_scr, xi_scr):
    abr, abi, cr, ci = _s5_discretize(par_ref[...])
    bre, bim = bre_ref[...], bim_ref[...]
    bbr = (cr * bre - ci * bim).astype(BF16)
    bbi = (cr * bim + ci * bre).astype(BF16)
    u = u_ref[...]
    ub = u.astype(BF16)
    xr_scr[...] = _dot(ub, bbr)
    xi_scr[...] = _dot(ub, bbi)

    shape = (S5_LANES, S5_ST)
    ar8 = jnp.broadcast_to(abr, shape)
    ai8 = jnp.broadcast_to(abi, shape)

    def local_step(t, carry):
        xr, xi = carry
        rows = pl.ds(pl.multiple_of(t * S5_LANES, S5_LANES), S5_LANES)
        nr = ar8 * xr - ai8 * xi + xr_scr[rows, :]
        ni = ar8 * xi + ai8 * xr + xi_scr[rows, :]
        xr_scr[rows, :] = nr
        xi_scr[rows, :] = ni
        return nr, ni

    zero = jnp.zeros(shape, F32)
    fr, fi = lax.fori_loop(0, S5_SEG, local_step, (zero, zero), unroll=8)

    pr, pi = abr, abi
    for _ in range(int(math.log2(S5_SEG))):
        pr, pi = pr * pr - pi * pi, 2.0 * pr * pi
    hr, hi = h0r_ref[...], h0i_ref[...]
    in_r, in_i = [], []
    for j in range(S5_LANES):
        in_r.append(hr)
        in_i.append(hi)
        hr, hi = fr[j:j + 1] + pr * hr - pi * hi, fi[j:j + 1] + pr * hi + pi * hr
    hr_ref[...] = hr
    hi_ref[...] = hi
    cin_r = jnp.concatenate(in_r, axis=0)
    cin_i = jnp.concatenate(in_i, axis=0)

    def fix_step(t, carry):
        dr, di = carry
        rows = pl.ds(pl.multiple_of(t * S5_LANES, S5_LANES), S5_LANES)
        xr_scr[rows, :] = xr_scr[rows, :] + dr
        xi_scr[rows, :] = xi_scr[rows, :] + di
        return dr * ar8 - di * ai8, dr * ai8 + di * ar8

    first = (ar8 * cin_r - ai8 * cin_i, ar8 * cin_i + ai8 * cin_r)
    lax.fori_loop(0, S5_SEG, fix_step, first, unroll=8)

    y = (_dot(xr_scr[...].astype(BF16), cre_ref[...].astype(BF16))
         - _dot(xi_scr[...].astype(BF16), cim_ref[...].astype(BF16)) + d_ref[...] * u)
    o_ref[...] = _gelu(y)


def _s5_param_specs(layer):
    lead = (lambda *a: a[-1])

    def spec(shape):
        return pl.BlockSpec((None, None) + shape, lambda *a: (layer, lead(*a), 0, 0))

    return [spec((8, S5_ST)), spec((S5_CH, S5_ST)), spec((S5_CH, S5_ST)),
            spec((S5_ST, S5_CH)), spec((S5_ST, S5_CH)), spec((1, S5_CH))]


def _s5_prompt(u_perm, s5p, layer):
    zeros = jnp.zeros((BATCH, S5_NGB, 1, S5_ST), F32)
    state_spec = pl.BlockSpec((None, None, 1, S5_ST), lambda b, g: (b, g, 0, 0))
    out, hr, hi = pl.pallas_call(
        _s5_prompt_body,
        grid=(BATCH, S5_NGB),
        in_specs=[pl.BlockSpec((SEQ, S5_CH), lambda b, g: (b, g))] + _s5_param_specs(layer)
        + [state_spec, state_spec],
        out_specs=[pl.BlockSpec((SEQ, S5_CH), lambda b, g: (b, g)), state_spec, state_spec],
        out_shape=[jax.ShapeDtypeStruct((N_PROMPT, WIDTH), F32),
                   jax.ShapeDtypeStruct((BATCH, S5_NGB, 1, S5_ST), F32),
                   jax.ShapeDtypeStruct((BATCH, S5_NGB, 1, S5_ST), F32)],
        scratch_shapes=[pltpu.VMEM((SEQ, S5_ST), F32), pltpu.VMEM((SEQ, S5_ST), F32)],
        compiler_params=_params("parallel", "parallel"),
        name="s5_prompt",
    )(u_perm, *s5p, zeros, zeros)
    return out, hr, hi


def _s5_sample_body(u_ref, par_ref, bre_ref, bim_ref, cre_ref, cim_ref, d_ref, h0r_ref, h0i_ref,
                    o_ref, hr_ref, hi_ref):
    abr, abi, cr, ci = _s5_discretize(par_ref[...])
    bre, bim = bre_ref[...], bim_ref[...]
    bbr = (cr * bre - ci * bim).astype(BF16)
    bbi = (cr * bim + ci * bre).astype(BF16)
    u = u_ref[...]
    ub = u.astype(BF16)
    h0r, h0i = h0r_ref[...], h0i_ref[...]
    xr = _dot(ub, bbr) + abr * h0r - abi * h0i
    xi = _dot(ub, bbi) + abr * h0i + abi * h0r
    hr_ref[...] = xr
    hi_ref[...] = xi
    y = (_dot(xr.astype(BF16), cre_ref[...].astype(BF16))
         - _dot(xi.astype(BF16), cim_ref[...].astype(BF16)) + d_ref[...] * u)
    o_ref[...] = _gelu(y)


def _s5_sample(proj, s5p, h0r, h0i, layer):
    state_spec = pl.BlockSpec((DEC_BATCH, S5_ST), lambda g: (0, g))
    return pl.pallas_call(
        _s5_sample_body,
        grid=(S5_NGB,),
        in_specs=[pl.BlockSpec((DEC_BATCH, S5_CH), lambda g: (SAMPLE_ROW_BLOCK, C_U // S5_CH + g))]
        + _s5_param_specs(layer) + [state_spec, state_spec],
        out_specs=[pl.BlockSpec((DEC_BATCH, S5_CH), lambda g: (0, g)), state_spec, state_spec],
        out_shape=[jax.ShapeDtypeStruct((DEC_BATCH, WIDTH), F32),
                   jax.ShapeDtypeStruct((DEC_BATCH, S5_NGB * S5_ST), F32),
                   jax.ShapeDtypeStruct((DEC_BATCH, S5_NGB * S5_ST), F32)],
        compiler_params=_params("parallel"),
        name="s5_sample",
    )(proj, *s5p, h0r, h0i)


GDN_TS = 512
GDN_SCALE = HEAD_DIM ** -0.5


def _l2norm_heads(xc, o_ref):
    for h in range(HEADS):
        cs = slice(h * HEAD_DIM, (h + 1) * HEAD_DIM)
        seg = xc[:, cs]
        o_ref[:, cs] = seg * lax.rsqrt(jnp.sum(seg * seg, axis=-1, keepdims=True) + EPS)


def _conv_prompt_body(x_ref, prev_ref, w_ref, o_ref, ext_scr):
    i = pl.program_id(1)
    part = pl.program_id(2)
    x = x_ref[...]
    ext_scr[0:8, :] = jnp.where(i == 0, 0.0, prev_ref[...])
    ext_scr[8:8 + GDN_TS, :] = x
    w = w_ref[...]
    acc = x * w[3:4]
    for tap in range(CONV_TAPS - 1):
        back = CONV_TAPS - 1 - tap
        acc = acc + ext_scr[8 - back:8 - back + GDN_TS, :] * w[tap:tap + 1]
    xc = acc * _sigmoid(acc)

    @pl.when(part < 2)
    def _():
        _l2norm_heads(xc, o_ref)

    @pl.when(part == 2)
    def _():
        o_ref[...] = xc


def _conv_prompt(proj, conv_w, layer):
    n_t = SEQ // GDN_TS
    col0 = C_QKV // WIDTH

    def prev_index(b, i, p):
        return (jnp.maximum(b * (SEQ // 8) + i * (GDN_TS // 8) - 1, 0), col0 + p)

    return pl.pallas_call(
        _conv_prompt_body,
        grid=(BATCH, n_t, 3),
        in_specs=[
            pl.BlockSpec((GDN_TS, WIDTH), lambda b, i, p: (b * n_t + i, col0 + p)),
            pl.BlockSpec((8, WIDTH), prev_index),
            pl.BlockSpec((None, 8, WIDTH), lambda b, i, p: (layer, 0, p)),
        ],
        out_specs=pl.BlockSpec((GDN_TS, WIDTH), lambda b, i, p: (b * n_t + i, p)),
        out_shape=jax.ShapeDtypeStruct((N_PROMPT, CONV_WIDTH), F32),
        scratch_shapes=[pltpu.VMEM((GDN_TS + 8, WIDTH), F32)],
        compiler_params=_params("parallel", "parallel", "parallel"),
        name="gdn_conv_prompt",
    )(proj, proj, conv_w)


def _conv_sample_body(x_ref, hist_ref, w_ref, o_ref):
    part = pl.program_id(0)
    w = w_ref[...]
    acc = x_ref[...] * w[3:4]
    for tap in range(CONV_TAPS - 1):
        acc = acc + hist_ref[tap] * w[tap:tap + 1]
    xc = acc * _sigmoid(acc)

    @pl.when(part < 2)
    def _():
        _l2norm_heads(xc, o_ref)

    @pl.when(part == 2)
    def _():
        o_ref[...] = xc


def _conv_sample(proj, hist, conv_w, layer):
    col0 = C_QKV // WIDTH
    return pl.pallas_call(
        _conv_sample_body,
        grid=(3,),
        in_specs=[
            pl.BlockSpec((DEC_BATCH, WIDTH), lambda p: (SAMPLE_ROW_BLOCK, col0 + p)),
            pl.BlockSpec((None, CONV_TAPS - 1, DEC_BATCH, WIDTH), lambda p: (layer, 0, 0, p)),
            pl.BlockSpec((None, 8, WIDTH), lambda p: (layer, 0, p)),
        ],
        out_specs=pl.BlockSpec((DEC_BATCH, WIDTH), lambda p: (0, p)),
        out_shape=jax.ShapeDtypeStruct((DEC_BATCH, CONV_WIDTH), F32),
        compiler_params=_params("parallel"),
        name="gdn_conv_sample",
    )(proj, hist, conv_w)


GDN_CHUNK = 128
GDN_ROW_SLOTS = 16


def _gdn_gates_body(ba_ref, par_ref, bg_ref, gcrow_ref, *, rows, chunk):
    ba = ba_ref[...]
    par = par_ref[...]
    g = -jnp.exp(par[0:1]) * _softplus(ba + par[1:2])
    if chunk > 1:
        r = lax.broadcasted_iota(jnp.int32, (rows, rows), 0)
        c = lax.broadcasted_iota(jnp.int32, (rows, rows), 1)
        tri = jnp.where((r >= c) & ((r & -chunk) == (c & -chunk)), 1.0, 0.0)
        g = _dot(tri, g, precision=HIGHEST)
        g_t = g.T
        for cc in range(rows // chunk):
            gcrow_ref[cc * GDN_ROW_SLOTS:(cc + 1) * GDN_ROW_SLOTS, :] = (
                g_t[0:GDN_ROW_SLOTS, cc * chunk:(cc + 1) * chunk])
    lane = lax.broadcasted_iota(jnp.int32, ba.shape, 1)
    bg_ref[...] = jnp.where(lane < HEADS, _sigmoid(ba), g)


def _gdn_gates(proj, gate_par, layer, rows, n_blocks, first_block, chunk):
    out_specs = [pl.BlockSpec((rows, 128), lambda i: (i, 0))]
    out_shape = [jax.ShapeDtypeStruct((rows * n_blocks, 128), F32)]
    if chunk > 1:
        slots = rows // chunk * GDN_ROW_SLOTS
        out_specs.append(pl.BlockSpec((slots, chunk), lambda i: (i, 0)))
        out_shape.append(jax.ShapeDtypeStruct((slots * n_blocks, chunk), F32))
        body = functools.partial(_gdn_gates_body, rows=rows, chunk=chunk)
    else:
        body = functools.partial(_gdn_gates_body, gcrow_ref=None, rows=rows, chunk=chunk)
    return pl.pallas_call(
        body,
        grid=(n_blocks,),
        in_specs=[pl.BlockSpec((rows, 128), lambda i: (first_block + i, C_BA // 128)),
                  pl.BlockSpec((None, 8, 128), lambda i: (layer, 0, 0))],
        out_specs=out_specs,
        out_shape=out_shape,
        compiler_params=_params("parallel"),
        name="gdn_gates",
    )(proj, gate_par)


GDN_BASE = 8
GDN_GROUP = 2


def _gdn_chunk_body(q_ref, k_ref, v_ref, bg_ref, gcrow_ref, z_ref, s0_ref, gn_ref, o_ref, sfin_ref,
                    s_scr, mask_scr, u_scr, w_scr, qk_scr, qd_scr, kd_scr, *, n_tiles, tile_rows):
    ti = pl.program_id(1)
    n_chunks = tile_rows // GDN_CHUNK
    lanes = [slice(h * HEAD_DIM, (h + 1) * HEAD_DIM) for h in range(HEADS)]

    @pl.when(ti == 0)
    def _():
        s_scr[...] = s0_ref[...]

    row = lax.broadcasted_iota(jnp.int32, (GDN_CHUNK, GDN_CHUNK), 0)
    col = lax.broadcasted_iota(jnp.int32, (GDN_CHUNK, GDN_CHUNK), 1)

    def same_block(size):
        return (row & -size) == (col & -size)

    strict = row > col
    eye = jnp.where(row == col, 1.0, 0.0)
    mask_scr[0] = jnp.where(row >= col, 1.0, 0.0)
    mask_scr[1] = jnp.where(strict & same_block(GDN_BASE), 1.0, 0.0)
    merge_sizes = []
    size = GDN_BASE
    while size < GDN_CHUNK:
        mask_scr[2 + len(merge_sizes)] = jnp.where(strict & same_block(2 * size) & ~same_block(size), 1.0, 0.0)
        merge_sizes.append(size)
        size *= 2
    gain = gn_ref[...]

    tile = (GDN_CHUNK, HEAD_DIM)

    def prepare(pair, carry):
        units = [(dc, h) for dc in range(GDN_GROUP) for h in range(HEADS)]
        chains = range(len(units))
        rows, bg, gc_by_head = [], [], []
        for dc in range(GDN_GROUP):
            c = pair * GDN_GROUP + dc
            rows.append(pl.ds(pl.multiple_of(c * GDN_CHUNK, GDN_CHUNK), GDN_CHUNK))
            bg.append(bg_ref[rows[dc], :])
            gc_by_head.append(gcrow_ref[pl.ds(pl.multiple_of(c * GDN_ROW_SLOTS + HEADS, HEADS), HEADS), :])
        beta_of = lambda dc, h: jnp.broadcast_to(bg[dc][:, h:h + 1], tile)
        gc_of = lambda dc, h: jnp.broadcast_to(bg[dc][:, HEADS + h:HEADS + h + 1], tile)
        m = []
        for dc, h in units:
            cs = lanes[h]
            k = k_ref[rows[dc], cs]
            gc_lanes = gc_by_head[dc][h:h + 1, :]
            decay = mask_scr[0] * jnp.exp(jnp.minimum(gc_of(dc, h) - gc_lanes, 0.0))
            k16 = k.astype(BF16)
            m.append(_dot_nt((k * beta_of(dc, h)).astype(BF16), k16) * decay)
            qk_scr[rows[dc], cs] = (
                _dot_nt((q_ref[rows[dc], cs] * GDN_SCALE).astype(BF16), k16) * decay).astype(BF16)
        base = [m[i] * mask_scr[1] for i in chains]
        b16 = [base[i].astype(BF16) for i in chains]
        power = [_dot(b16[i], b16[i]).astype(BF16) for i in chains]
        inv = [eye - base[i] for i in chains]
        inv = [inv[i] + _dot(inv[i].astype(BF16), power[i]) for i in chains]
        for _ in range(int(math.log2(GDN_BASE)) - 2):
            power = [_dot(power[i], power[i]).astype(BF16) for i in chains]
            inv = [inv[i] + _dot(inv[i].astype(BF16), power[i]) for i in chains]
        for n in range(len(merge_sizes)):
            i16 = [inv[i].astype(BF16) for i in chains]
            half = [_dot(i16[i], (m[i] * mask_scr[2 + n]).astype(BF16)).astype(BF16) for i in chains]
            inv = [inv[i] - _dot(half[i], i16[i]) for i in chains]
        for i, (dc, h) in enumerate(units):
            cs = lanes[h]
            k = k_ref[rows[dc], cs]
            beta = beta_of(dc, h)
            gc = gc_of(dc, h)
            egc = jnp.exp(gc)
            rhs = jnp.concatenate([v_ref[rows[dc], cs] * beta, k * beta * egc], axis=-1)
            sol = rhs + _dot((inv[i] - eye).astype(BF16), rhs.astype(BF16))
            u_scr[rows[dc], cs] = sol[:, :HEAD_DIM]
            w_scr[rows[dc], cs] = sol[:, HEAD_DIM:].astype(BF16)
            gc_last = jnp.broadcast_to(gc[GDN_CHUNK - 1:GDN_CHUNK, :], tile)
            qd_scr[rows[dc], cs] = (q_ref[rows[dc], cs] * GDN_SCALE * egc).astype(BF16)
            kd_scr[rows[dc], cs] = (k * jnp.exp(gc_last - gc)).astype(BF16)
        return carry

    lax.fori_loop(0, n_chunks // GDN_GROUP, prepare, 0)

    def advance(c, carry):
        rows = pl.ds(pl.multiple_of(c * GDN_CHUNK, GDN_CHUNK), GDN_CHUNK)
        tail = pl.ds(pl.multiple_of(c * GDN_CHUNK + GDN_CHUNK - 8, 8), 8)
        heads = range(HEADS)
        gc_last = bg_ref[tail, :][7:8, :]
        state = [s_scr[h] for h in heads]
        s16 = [state[h].astype(BF16) for h in heads]
        vn16 = [(u_scr[rows, lanes[h]] - _dot(w_scr[rows, lanes[h]], s16[h])).astype(BF16) for h in heads]
        for h in heads:
            chunk_decay = jnp.exp(jnp.broadcast_to(gc_last[:, HEADS + h:HEADS + h + 1], (1, HEAD_DIM)))
            s_scr[h] = state[h] * chunk_decay + lax.dot_general(
                kd_scr[rows, lanes[h]], vn16[h], (((0,), (0,)), ((), ())), preferred_element_type=F32)
        for h in heads:
            cs = lanes[h]
            o = _dot(qd_scr[rows, cs], s16[h]) + _dot(qk_scr[rows, cs], vn16[h])
            zg = z_ref[rows, cs]
            o_ref[rows, cs] = (_rms(o, gain) * (zg * _sigmoid(zg))).astype(BF16)
        return carry

    lax.fori_loop(0, n_chunks, advance, 0)

    @pl.when(ti == n_tiles - 1)
    def _():
        sfin_ref[...] = s_scr[...]


def _gdn_chunks(qkv, bg, gcrow, z_src, z_col_block, s0, out_norm, layer, n_seq, seq_rows):
    tile_rows = min(seq_rows, GDN_TS)
    n_tiles = seq_rows // tile_rows
    n_masks = 2 + int(math.log2(GDN_CHUNK // GDN_BASE))

    def act(col_block):
        return pl.BlockSpec((tile_rows, WIDTH), lambda b, t: (b * n_tiles + t, col_block))

    slots = tile_rows // GDN_CHUNK * GDN_ROW_SLOTS
    state_spec = pl.BlockSpec((None, HEADS, HEAD_DIM, HEAD_DIM), lambda b, t: (b, 0, 0, 0))
    return pl.pallas_call(
        functools.partial(_gdn_chunk_body, n_tiles=n_tiles, tile_rows=tile_rows),
        grid=(n_seq, n_tiles),
        in_specs=[act(0), act(1), act(2),
                  pl.BlockSpec((tile_rows, 128), lambda b, t: (b * n_tiles + t, 0)),
                  pl.BlockSpec((slots, GDN_CHUNK), lambda b, t: (b * n_tiles + t, 0)),
                  act(z_col_block), state_spec,
                  pl.BlockSpec((None, 1, HEAD_DIM), lambda b, t: (layer, 0, 0))],
        out_specs=[act(0), state_spec],
        out_shape=[jax.ShapeDtypeStruct((n_seq * seq_rows, WIDTH), BF16),
                   jax.ShapeDtypeStruct((n_seq, HEADS, HEAD_DIM, HEAD_DIM), F32)],
        scratch_shapes=[pltpu.VMEM((HEADS, HEAD_DIM, HEAD_DIM), F32),
                        pltpu.VMEM((n_masks, GDN_CHUNK, GDN_CHUNK), F32),
                        pltpu.VMEM((tile_rows, WIDTH), F32),
                        pltpu.VMEM((tile_rows, WIDTH), BF16),
                        pltpu.VMEM((tile_rows, WIDTH), BF16),
                        pltpu.VMEM((tile_rows, WIDTH), BF16),
                        pltpu.VMEM((tile_rows, WIDTH), BF16)],
        compiler_params=_params("parallel", "arbitrary"),
        name="gdn_chunks",
    )(qkv, qkv, qkv, bg, gcrow, z_src, s0, out_norm)


GDN_STEP_COLS = CONV_WIDTH + 128 + WIDTH


def _gdn_step_body(x_ref, s0_ref, gn_ref, o_ref, s_ref):
    heads = range(HEADS)
    col = lambda part, h: slice(part * WIDTH + h * HEAD_DIM, part * WIDTH + (h + 1) * HEAD_DIM)
    gain = gn_ref[...]
    bg = x_ref[:, CONV_WIDTH:CONV_WIDTH + 128]
    z0 = CONV_WIDTH + 128
    k = [x_ref[:, col(1, h)] for h in heads]
    q = [x_ref[:, col(0, h)] * GDN_SCALE for h in heads]
    k16 = [k[h].astype(BF16) for h in heads]
    state = [s0_ref[h] for h in heads]
    s16 = [state[h].astype(BF16) for h in heads]
    k_s = [_dot(k16[h], s16[h]) for h in heads]
    q_s = [_dot(q[h].astype(BF16), s16[h]) for h in heads]
    decay = [jnp.exp(jnp.broadcast_to(bg[:, HEADS + h:HEADS + h + 1], (8, HEAD_DIM))) for h in heads]
    beta = [jnp.broadcast_to(bg[:, h:h + 1], (8, HEAD_DIM)) for h in heads]
    v_new = [beta[h] * (x_ref[:, col(2, h)] - decay[h] * k_s[h]) for h in heads]
    for h in heads:
        s_ref[h] = state[h] * decay[h][0:1, :] + lax.dot_general(
            k16[h], v_new[h].astype(BF16), (((0,), (0,)), ((), ())), preferred_element_type=F32)
    for h in heads:
        qk = jnp.sum(q[h].astype(BF16).astype(F32) * k16[h].astype(F32), axis=-1, keepdims=True)
        o = decay[h] * q_s[h] + qk * v_new[h].astype(BF16).astype(F32)
        zg = x_ref[:, z0 + h * HEAD_DIM:z0 + (h + 1) * HEAD_DIM]
        o_ref[:, h * HEAD_DIM:(h + 1) * HEAD_DIM] =(_rms(o, gain) * (zg * _sigmoid(zg))).astype(BF16)


def _gdn_step(x8, state, out_norm, layer):
    return pl.pallas_call(
        _gdn_step_body,
        grid=(DEC_BATCH,),
        in_specs=[pl.BlockSpec((None, 8, GDN_STEP_COLS), lambda b: (b, 0, 0)),
                  pl.BlockSpec((None, None, HEADS, HEAD_DIM, HEAD_DIM), lambda b: (layer, b, 0, 0, 0)),
                  pl.BlockSpec((None, 1, HEAD_DIM), lambda b: (layer, 0, 0))],
        out_specs=[pl.BlockSpec((None, 8, WIDTH), lambda b: (b, 0, 0)),
                   pl.BlockSpec((None, HEADS, HEAD_DIM, HEAD_DIM), lambda b: (b, 0, 0, 0))],
        out_shape=[jax.ShapeDtypeStruct((DEC_BATCH, 8, WIDTH), BF16),
                   jax.ShapeDtypeStruct((DEC_BATCH, HEADS, HEAD_DIM, HEAD_DIM), F32)],
        compiler_params=_params("parallel"),
        name="gdn_step",
    )(x8, state, out_norm)


def _pack_body(a_ref, b_ref, o_ref):
    n = pl.program_id(1)
    first_gate = C_GATE // PACK_TN
    last = PACK_TILES - 1
    shift = 2 * HEADS
    a = a_ref[...]
    shifted = jnp.concatenate([a[shift:, :], b_ref[...]], axis=0)
    row = lax.broadcasted_iota(jnp.int32, a.shape, 0)
    logits = jnp.where(row < shift, a, 0.0)
    tile = jnp.where(n < first_gate, a, jnp.where(n == last, logits, shifted))
    o_ref[...] = tile.T.astype(BF16)


def _pack_w_in(w_in):
    first_gate = C_GATE // PACK_TN
    last = PACK_TILES - 1
    shift = 2 * HEADS
    w_t = jnp.swapaxes(w_in, 1, 2)

    def a_index(d, n):
        return (d, jnp.where(n == last, first_gate, n), 0)

    def b_index(d, n):
        return (d, jnp.where((n >= first_gate) & (n < last), (n + 1) * (PACK_TN // shift), 0), 0)

    return pl.pallas_call(
        _pack_body,
        grid=(DEPTH, PACK_TILES),
        in_specs=[pl.BlockSpec((None, PACK_TN, D_MODEL), a_index),
                  pl.BlockSpec((None, shift, D_MODEL), b_index)],
        out_specs=pl.BlockSpec((None, D_MODEL, PACK_TN), lambda d, n: (d, 0, n)),
        out_shape=jax.ShapeDtypeStruct((DEPTH, D_MODEL, PROJ_COLS), BF16),
        compiler_params=_params("parallel", "arbitrary"),
        name="pack_w_in",
    )(w_t, w_t)


def _s5_params(a_re, a_im, log_dt, b_re, b_im, c_re, c_im, d):
    eye = jnp.eye(S5_GB, dtype=F32)

    def state_rows(x):
        return x.reshape(DEPTH, S5_NGB, 1, S5_ST)

    par = jnp.concatenate(
        [state_rows(a_re), state_rows(a_im),
         state_rows(jnp.repeat(log_dt, SSM_STATE, axis=-1)),
         jnp.zeros((DEPTH, S5_NGB, 5, S5_ST), F32)], axis=2)

    def b_blockdiag(b):
        b = b.reshape(DEPTH, S5_NGB, S5_GB, SSM_STATE, SSM_GROUP)
        return jnp.einsum('dbgpc,gh->dbgchp', b, eye).reshape(DEPTH, S5_NGB, S5_CH, S5_ST)

    def c_blockdiag(c):
        c = c.reshape(DEPTH, S5_NGB, S5_GB, SSM_GROUP, SSM_STATE)
        return jnp.einsum('dbgcp,gh->dbgphc', c, eye).reshape(DEPTH, S5_NGB, S5_ST, S5_CH)

    return (par, b_blockdiag(b_re), b_blockdiag(b_im), c_blockdiag(c_re), c_blockdiag(c_im),
            d.reshape(DEPTH, S5_NGB, 1, S5_CH))


def _segment_major(x):
    w = x.shape[-1]
    return x.reshape(BATCH, S5_LANES, S5_SEG, w).transpose(0, 2, 1, 3).reshape(N_PROMPT, w)


def _token_major(x):
    w = x.shape[-1]
    return x.reshape(BATCH, S5_SEG, S5_LANES, w).transpose(0, 2, 1, 3).reshape(N_PROMPT, w)


def _pad_rows(x):
    return jnp.concatenate([x, jnp.zeros((ROWS - x.shape[0], x.shape[1]), x.dtype)], axis=0)


def kernel(x_prompt, x_sample, cache_k, cache_v, page_table, state_ssm, state_conv, state_gdn, ffn1_norm, ffn1_w_gate, ffn1_w_up, ffn1_w_down, mix_norm, w_in, sb_bias, ssm_a_re, ssm_a_im, ssm_log_dt, ssm_b_re, ssm_b_im, ssm_c_re, ssm_c_im, ssm_d, ssm_w_glu, gdn_conv_w, gdn_a_log, gdn_dt_bias, gdn_out_norm, w_branch, w_out, ffn2_norm, ffn2_w_gate, ffn2_w_up, ffn2_w_down, final_norm):
    bf = lambda w: w.astype(BF16)
    gain3 = lambda g: g.reshape(DEPTH, 1, D_MODEL)
    w_in_p = _pack_w_in(w_in)
    ffn1 = (gain3(ffn1_norm), bf(ffn1_w_gate), bf(ffn1_w_up), bf(ffn1_w_down))
    ffn2 = (gain3(ffn2_norm), bf(ffn2_w_gate), bf(ffn2_w_up), bf(ffn2_w_down))
    mix_gain = gain3(mix_norm)
    w_glu, w_br, w_o = bf(ssm_w_glu), bf(w_branch).reshape(DEPTH, 3, WIDTH, D_MODEL), bf(w_out)
    s5p = _s5_params(ssm_a_re, ssm_a_im, ssm_log_dt, ssm_b_re, ssm_b_im, ssm_c_re, ssm_c_im, ssm_d)
    bias_flat = sb_bias.reshape(DEPTH * HEADS)
    bias_lanes = jnp.tile(sb_bias, (1, PAGE)).reshape(DEPTH, 1, DEC_LANES)
    page_flat = page_table.reshape(DEC_BATCH * N_PAGES)
    conv_w8 = jnp.pad(gdn_conv_w, ((0, 0), (0, 8 - CONV_TAPS), (0, 0)))
    conv_hist = state_conv.transpose(0, 2, 1, 3)
    lane_pad = jnp.zeros((DEPTH, 128 - 2 * HEADS), F32)
    gate_par = jnp.stack([jnp.concatenate([jnp.zeros((DEPTH, HEADS), F32), gdn_a_log, lane_pad], axis=1),
                          jnp.concatenate([jnp.zeros((DEPTH, HEADS), F32), gdn_dt_bias, lane_pad], axis=1)]
                         + [jnp.zeros((DEPTH, 128), F32)] * 6, axis=1)
    out_norm = gdn_out_norm.reshape(DEPTH, 1, HEAD_DIM)
    ssm_h0r = state_ssm[..., 0].reshape(DEPTH, DEC_BATCH, WIDTH * SSM_STATE // SSM_GROUP)
    ssm_h0i = state_ssm[..., 1].reshape(DEPTH, DEC_BATCH, WIDTH * SSM_STATE // SSM_GROUP)

    x = _pad_rows(jnp.concatenate([x_prompt.reshape(N_PROMPT, D_MODEL),
                                   x_sample.reshape(DEC_BATCH, D_MODEL)], axis=0))
    sample = slice(N_PROMPT, N_PROMPT + DEC_BATCH)
    ks, vs, sp, ss, cp, cs_, gp, gs = ([] for _ in range(8))
    k_all = jnp.zeros((DEPTH, N_PROMPT, WIDTH), F32)
    v_all = jnp.zeros((DEPTH, N_PROMPT, WIDTH), F32)
    for l in range(DEPTH):
        x = _ffn(x, *ffn1, l)
        proj = _in_proj(x, mix_gain, w_in_p, l)

        oa_p, k_all, v_all = _sb_prompt(proj, bias_flat, l, k_all, v_all)
        q_s = proj[sample, C_Q:C_Q + WIDTH].reshape(DEC_BATCH, HEADS, HEAD_DIM)
        oa_s = _sb_decode(q_s, cache_k, cache_v, page_flat, bias_lanes, l)
        oa = _pad_rows(jnp.concatenate([oa_p, oa_s.reshape(DEC_BATCH, WIDTH).astype(BF16)], axis=0))

        gb_p, hr_p, hi_p = _s5_prompt(_segment_major(proj[:N_PROMPT, C_U:C_U + WIDTH]), s5p, l)
        gb_s, hr_s, hi_s = _s5_sample(proj, s5p, ssm_h0r[l], ssm_h0i[l], l)
        ob = _glu(_pad_rows(jnp.concatenate([_token_major(gb_p), gb_s], axis=0)), w_glu, l)

        qkv_p = _conv_prompt(proj, conv_w8, l)
        bg_p, gcrow_p = _gdn_gates(proj, gate_par, l, GDN_TS, N_PROMPT // GDN_TS, 0, GDN_CHUNK)
        oc_p, gdn_p = _gdn_chunks(qkv_p, bg_p, gcrow_p, proj, C_Z // WIDTH,
                                  jnp.zeros((BATCH, HEADS, HEAD_DIM, HEAD_DIM), F32), out_norm, l, BATCH, SEQ)
        qkv_s = _conv_sample(proj, conv_hist, conv_w8, l)
        (bg_s,) = _gdn_gates(proj, gate_par, l, DEC_BATCH, 1, SAMPLE_ROW_BLOCK, 1)
        step_in = jnp.concatenate([qkv_s, bg_s, proj[sample, C_Z:C_Z + WIDTH]], axis=1)
        oc_s, gdn_s = _gdn_step(jnp.pad(step_in[:, None, :], ((0, 0), (0, 7), (0, 0))), state_gdn, out_norm, l)
        oc = _pad_rows(jnp.concatenate([oc_p, oc_s[:, 0, :]], axis=0))

        x = _merge(x, oa, ob, oc, proj, w_br, w_o, l)
        x = _ffn(x, *ffn2, l)

        ks.append(proj[sample, C_K:C_K + WIDTH].reshape(DEC_BATCH, 1, HEADS, HEAD_DIM))
        vs.append(proj[sample, C_V:C_V + WIDTH].reshape(DEC_BATCH, 1, HEADS, HEAD_DIM))
        state = lambda r, i, n: jnp.stack([r.reshape(n, WIDTH // SSM_GROUP, SSM_STATE),
                                           i.reshape(n, WIDTH // SSM_GROUP, SSM_STATE)], axis=-1)
        sp.append(state(hr_p, hi_p, BATCH))
        ss.append(state(hr_s, hi_s, DEC_BATCH))
        qkv_cols = slice(C_QKV, C_QKV + CONV_WIDTH)
        cp.append(jnp.stack([proj[(b + 1) * SEQ - (CONV_TAPS - 1):(b + 1) * SEQ, qkv_cols] for b in range(BATCH)]))
        cs_.append(jnp.concatenate([state_conv[l][:, 1:], proj[sample, qkv_cols][:, None, :]], axis=1))
        gp.append(gdn_p)
        gs.append(gdn_s)

    gain = final_norm.reshape(1, D_MODEL)
    y_p = _final_norm(x, gain, GDN_TS, N_PROMPT // GDN_TS, 0)
    y_s = _final_norm(x, gain, DEC_BATCH, 1, SAMPLE_ROW_BLOCK)
    kv_shape = (DEPTH, BATCH, SEQ, HEADS, HEAD_DIM)
    return (y_p.reshape(BATCH, SEQ, D_MODEL), y_s.reshape(DEC_BATCH, 1, D_MODEL),
            k_all.reshape(kv_shape), v_all.reshape(kv_shape), jnp.stack(ks), jnp.stack(vs),
            jnp.stack(sp), jnp.stack(ss), jnp.stack(cp), jnp.stack(cs_),
            jnp.stack(gp), jnp.stack(gs))
```
